```python
import jax, jax.numpy as jnp
from jax import lax
import numpy as np

D_MODEL = 1024
BATCH = 8
SEQ = 2048
DEPTH = 4
DEC_BATCH = 32
DEC_SEQ = 8
PAST_LEN = 8192
PAGE_SIZE = 128

N_MIXERS = 3
N_HEADS = 8
HEAD_DIM = D_MODEL // N_HEADS
ROPE_THETA = 10000.0
RMS_EPS = 1e-6
Q_CHUNK = 64

MOBA_KV_HEADS = 4
GROUP_A = N_HEADS // MOBA_KV_HEADS
MOBA_BLOCK = 256
MOBA_TOPK = 3

LRU_WIDTH = D_MODEL
LRU_BLOCK_WIDTH = 256
LRU_BLOCKS = LRU_WIDTH // LRU_BLOCK_WIDTH
CONV_WIDTH = 4
LRU_C = 8.0

NSA_KV_HEADS = 2
GROUP_C = N_HEADS // NSA_KV_HEADS
CMP_BLOCK = 32
CMP_STRIDE = 16
CMP_HIDDEN = HEAD_DIM
SEL_BLOCK = 64
SEL_TOPN = 16
WINDOW = 512

FFN_HIDDEN = -(-8 * D_MODEL // (3 * 256)) * 256
PLE_DIM = 256

N_LAYERS_A = (DEPTH + N_MIXERS - 1) // N_MIXERS
N_LAYERS_B = (DEPTH + N_MIXERS - 2) // N_MIXERS
N_LAYERS_C = (DEPTH + N_MIXERS - 3) // N_MIXERS

kernel_name = 'hybrid_moba_rglru_nsa_decode_step'


def rms_norm(x, g):
    xf = x.astype(jnp.float32)
    y = xf * lax.rsqrt(jnp.mean(xf * xf, axis=-1, keepdims=True) + RMS_EPS)
    return (y * g.astype(jnp.float32)).astype(x.dtype)


def rope(x, pos):
    half = HEAD_DIM // 2
    freq = ROPE_THETA ** (-jnp.arange(half, dtype=jnp.float32) / half)
    ang = pos.astype(jnp.float32)[:, None] * freq[None, :]
    cos = jnp.cos(ang)[:, None, :]
    sin = jnp.sin(ang)[:, None, :]
    xf = x.astype(jnp.float32)
    x1, x2 = xf[..., :half], xf[..., half:]
    return jnp.concatenate([x1 * cos - x2 * sin, x2 * cos + x1 * sin], axis=-1).astype(x.dtype)


def masked_softmax(s, mask, axis):
    s = jnp.where(mask, s, -jnp.inf)
    m = jnp.max(s, axis=axis, keepdims=True)
    m = jnp.where(jnp.isfinite(m), m, 0.0)
    e = jnp.where(mask, jnp.exp(s - m), 0.0)
    return e / jnp.maximum(jnp.sum(e, axis=axis, keepdims=True), 1e-30)


def map_query_chunks(fn, *xs):
    n = xs[0].shape[0]
    c = Q_CHUNK if n % Q_CHUNK == 0 else n
    out = lax.map(fn, tuple(a.reshape((n // c, c) + a.shape[1:]) for a in xs))
    return out.reshape((n,) + out.shape[2:])


def moba_seq(q, k, v, qpos):
    L = k.shape[0]
    nb = -(-L // MOBA_BLOCK)
    pad = nb * MOBA_BLOCK - L
    kb = jnp.pad(k, ((0, pad), (0, 0), (0, 0))).reshape(nb, MOBA_BLOCK, MOBA_KV_HEADS, HEAD_DIM).transpose(2, 0, 1, 3)
    vb = jnp.pad(v, ((0, pad), (0, 0), (0, 0))).reshape(nb, MOBA_BLOCK, MOBA_KV_HEADS, HEAD_DIM).transpose(2, 0, 1, 3)
    kmean = jnp.mean(kb.astype(jnp.float32), axis=2)
    head_kv = jnp.arange(N_HEADS) // GROUP_A
    kmean_h = kmean[head_kv]
    ktop = min(MOBA_TOPK, nb)
    scale = HEAD_DIM ** -0.5

    def chunk(args):
        qq, pp = args
        m = qq.shape[0]
        cur = pp // MOBA_BLOCK
        gate = jnp.einsum('mhd,hbd->mhb', qq.astype(jnp.float32), kmean_h)
        fully_past = jnp.arange(nb)[None, None, :] < cur[:, None, None]
        gate = jnp.where(fully_past, gate, -jnp.inf)
        _, idx = lax.top_k(gate, ktop)
        blocks = jnp.concatenate([idx, jnp.broadcast_to(cur[:, None, None], (m, N_HEADS, 1))], axis=-1)
        kg = kb[head_kv[None, :, None], blocks]
        vg = vb[head_kv[None, :, None], blocks]
        s = jnp.einsum('mhd,mhjsd->mhjs', qq, kg, preferred_element_type=jnp.float32) * scale
        keypos = blocks[..., None] * MOBA_BLOCK + jnp.arange(MOBA_BLOCK)
        rank_ok = jnp.concatenate([
            jnp.broadcast_to(jnp.arange(ktop)[None, None, :] < cur[:, None, None], (m, N_HEADS, ktop)),
            jnp.ones((m, N_HEADS, 1), dtype=bool)], axis=-1)
        mask = rank_ok[..., None] & (keypos <= pp[:, None, None, None])
        p = masked_softmax(s, mask, axis=(2, 3))
        return jnp.einsum('mhjs,mhjsd->mhd', p.astype(vg.dtype), vg).astype(qq.dtype)

    return map_query_chunks(chunk, q, qpos)


def moba_project(xn, w_qkv, pos):
    B, T, _ = xn.shape
    hq, hk = N_HEADS * HEAD_DIM, MOBA_KV_HEADS * HEAD_DIM
    z = xn @ w_qkv
    q = rope(z[..., :hq].reshape(B, T, N_HEADS, HEAD_DIM), pos)
    k = rope(z[..., hq:hq + hk].reshape(B, T, MOBA_KV_HEADS, HEAD_DIM), pos)
    v = z[..., hq + hk:].reshape(B, T, MOBA_KV_HEADS, HEAD_DIM)
    return q, k, v


def moba_prompt(xn, w_qkv, w_o):
    B, T, _ = xn.shape
    pos = jnp.arange(T, dtype=jnp.int32)
    q, k, v = moba_project(xn, w_qkv, pos)
    o = lax.map(lambda a: moba_seq(a[0], a[1], a[2], pos), (q, k, v))
    return o.reshape(B, T, -1) @ w_o, jnp.stack([k, v], axis=2)


def moba_sample(xn, pool, page_table, w_qkv, w_o):
    B, T, _ = xn.shape
    pos = PAST_LEN + jnp.arange(T, dtype=jnp.int32)
    q, k, v = moba_project(xn, w_qkv, pos)

    def one(a):
        pt, qq, kk, vv = a
        past = pool[pt].reshape(-1, 2, MOBA_KV_HEADS, HEAD_DIM)
        kf = jnp.concatenate([past[:, 0], kk], axis=0)
        vf = jnp.concatenate([past[:, 1], vv], axis=0)
        return moba_seq(qq, kf, vf, pos)

    o = lax.map(one, (page_table, q, k, v))
    return o.reshape(B, T, -1) @ w_o, jnp.stack([k, v], axis=2)


def rglru_core(u, conv_state, h0, conv_w, conv_b, w_a, b_a, w_x, b_x, lam):
    B, T, _ = u.shape
    up = jnp.concatenate([conv_state.astype(u.dtype), u], axis=1)
    xc = conv_b
    for j in range(CONV_WIDTH):
        xc = xc + up[:, j:j + T] * conv_w[j]
    xb = xc.reshape(B, T, LRU_BLOCKS, LRU_BLOCK_WIDTH)
    r = jax.nn.sigmoid(jnp.einsum('btnk,nkj->btnj', xb, w_a).reshape(B, T, LRU_WIDTH) + b_a).astype(jnp.float32)
    ig = jax.nn.sigmoid(jnp.einsum('btnk,nkj->btnj', xb, w_x).reshape(B, T, LRU_WIDTH) + b_x).astype(jnp.float32)
    log_a = -LRU_C * r * jax.nn.softplus(-lam.astype(jnp.float32))
    a = jnp.exp(log_a)
    b = jnp.sqrt(-jnp.expm1(2.0 * log_a)) * ig * xc.astype(jnp.float32)

    def step(h, ab):
        h = ab[0] * h + ab[1]
        return h, h

    hT, hs = lax.scan(step, h0.astype(jnp.float32), (jnp.swapaxes(a, 0, 1), jnp.swapaxes(b, 0, 1)))
    return jnp.swapaxes(hs, 0, 1).astype(u.dtype), up[:, T:], hT.astype(h0.dtype)


def rglru_mixer(xn, conv_state, h0, w_in, conv_w, conv_b, w_a, b_a, w_x, b_x, lam, w_o):
    z = xn @ w_in
    gate, u = z[..., :LRU_WIDTH], z[..., LRU_WIDTH:]
    y, new_conv, hT = rglru_core(u, conv_state, h0, conv_w, conv_b, w_a, b_a, w_x, b_x, lam)
    return (jax.nn.gelu(gate) * y) @ w_o, new_conv, hT


def nsa_compress(kc, pos_emb, w1, w2):
    L = kc.shape[0]
    nc = (L - CMP_BLOCK) // CMP_STRIDE + 1
    idx = jnp.arange(nc)[:, None] * CMP_STRIDE + jnp.arange(CMP_BLOCK)[None, :]
    blk = kc[idx] + pos_emb[None, :, None, :]
    flat = blk.transpose(0, 2, 1, 3).reshape(nc, NSA_KV_HEADS, CMP_BLOCK * HEAD_DIM)
    return jax.nn.gelu(flat @ w1) @ w2


def cmp_to_sel_matrix(nc, nsel):
    ci = jnp.arange(nc)[:, None]
    sj = jnp.arange(nsel)[None, :]
    w = jnp.zeros((nc, nsel), jnp.float32)
    for m in range(SEL_BLOCK // CMP_STRIDE):
        for n in range(CMP_BLOCK // CMP_STRIDE):
            w = w + (ci == sj * (SEL_BLOCK // CMP_STRIDE) + m - n).astype(jnp.float32)
    return w


def nsa_seq(q, g, qpos, kc, vc, ks, vs, kw, vw, kw_start, cmp_params):
    pk, pv, k_w1, k_w2, v_w1, v_w2 = cmp_params
    L = kc.shape[0]
    ck = nsa_compress(kc, pk, k_w1, k_w2)
    cv = nsa_compress(vc, pv, v_w1, v_w2)
    nc = ck.shape[0]
    c_end = jnp.arange(nc) * CMP_STRIDE + (CMP_BLOCK - 1)
    nsel = -(-L // SEL_BLOCK)
    sel_map = cmp_to_sel_matrix(nc, nsel)
    pad = nsel * SEL_BLOCK - L
    ksb = jnp.pad(ks, ((0, pad), (0, 0), (0, 0))).reshape(nsel, SEL_BLOCK, NSA_KV_HEADS, HEAD_DIM).transpose(2, 0, 1, 3)
    vsb = jnp.pad(vs, ((0, pad), (0, 0), (0, 0))).reshape(nsel, SEL_BLOCK, NSA_KV_HEADS, HEAD_DIM).transpose(2, 0, 1, 3)
    topn = min(SEL_TOPN, nsel)
    kwp = jnp.pad(kw, ((WINDOW, 0), (0, 0), (0, 0)))
    vwp = jnp.pad(vw, ((WINDOW, 0), (0, 0), (0, 0)))
    kv_ids = jnp.arange(NSA_KV_HEADS)[None, :, None]
    scale = HEAD_DIM ** -0.5

    def chunk(args):
        qq, gg, pp = args
        m = qq.shape[0]
        qg = qq.reshape(m, NSA_KV_HEADS, GROUP_C, HEAD_DIM)
        gg = gg.reshape(m, NSA_KV_HEADS, GROUP_C, 3)[..., None]
        s_c = jnp.einsum('mkgd,ckd->mkgc', qg, ck, preferred_element_type=jnp.float32) * scale
        avail = (c_end[None, :] <= pp[:, None])[:, None, None, :]
        p_c = masked_softmax(s_c, avail, axis=-1)
        o_c = jnp.einsum('mkgc,ckd->mkgd', p_c.astype(cv.dtype), cv)
        imp = jnp.einsum('mkgc,cj->mkj', p_c, sel_map)
        cur = (pp // SEL_BLOCK)[:, None, None]
        j = jnp.arange(nsel)[None, None, :]
        forced = (j == 0) | (j == cur) | (j == cur - 1)
        score = jnp.where(j > cur, -jnp.inf, jnp.where(forced, jnp.inf, imp))
        top_s, idx = lax.top_k(score, topn)
        kg = ksb[kv_ids, idx]
        vg = vsb[kv_ids, idx]
        s_s = jnp.einsum('mkgd,mknsd->mkgns', qg, kg, preferred_element_type=jnp.float32) * scale
        keypos = idx[..., None] * SEL_BLOCK + jnp.arange(SEL_BLOCK)
        sel_mask = ((top_s > -jnp.inf)[..., None] & (keypos <= pp[:, None, None, None]))[:, :, None]
        p_s = masked_softmax(s_s, sel_mask, axis=(3, 4))
        o_s = jnp.einsum('mkgns,mknsd->mkgd', p_s.astype(vg.dtype), vg)
        nw = WINDOW + m - 1
        start = pp[0] + 1 - kw_start
        kwin = lax.dynamic_slice_in_dim(kwp, start, nw, axis=0)
        vwin = lax.dynamic_slice_in_dim(vwp, start, nw, axis=0)
        wpos = pp[0] - WINDOW + 1 + jnp.arange(nw)
        wmask = (wpos[None, :] <= pp[:, None]) & (wpos[None, :] > pp[:, None] - WINDOW) & (wpos[None, :] >= kw_start)
        s_w = jnp.einsum('mkgd,wkd->mkgw', qg, kwin, preferred_element_type=jnp.float32) * scale
        p_w = masked_softmax(s_w, wmask[:, None, None, :], axis=-1)
        o_w = jnp.einsum('mkgw,wkd->mkgd', p_w.astype(vwin.dtype), vwin)
        o = gg[..., 0, :] * o_c + gg[..., 1, :] * o_s + gg[..., 2, :] * o_w
        return o.reshape(m, N_HEADS, HEAD_DIM).astype(qq.dtype)

    return map_query_chunks(chunk, q, g, qpos)


def nsa_project(xn, w_in, pos):
    B, T, _ = xn.shape
    hq, hk = N_HEADS * HEAD_DIM, NSA_KV_HEADS * HEAD_DIM
    z = xn @ w_in
    q = rope(z[..., :hq].reshape(B, T, N_HEADS, HEAD_DIM), pos)
    kvs = z[..., hq:hq + 6 * hk].reshape(B, T, 6, NSA_KV_HEADS, HEAD_DIM)
    g = jax.nn.sigmoid(z[..., hq + 6 * hk:].reshape(B, T, N_HEADS, 3))
    kc = rope(kvs[:, :, 0], pos)
    ks = rope(kvs[:, :, 2], pos)
    kw = rope(kvs[:, :, 4], pos)
    return q, g, kc, kvs[:, :, 1], ks, kvs[:, :, 3], kw, kvs[:, :, 5]


def nsa_prompt(xn, w_in, w_o, cmp_params):
    B, T, _ = xn.shape
    pos = jnp.arange(T, dtype=jnp.int32)
    q, g, kc, vc, ks, vs, kw, vw = nsa_project(xn, w_in, pos)
    o = lax.map(lambda a: nsa_seq(a[0], a[1], pos, a[2], a[3], a[4], a[5], a[6], a[7], 0, cmp_params),
                (q, g, kc, vc, ks, vs, kw, vw))
    wb = min(WINDOW, T)
    win = jnp.stack([kw, vw], axis=2)[:, T - wb:]
    return o.reshape(B, T, -1) @ w_o, jnp.stack([kc, vc], axis=2), jnp.stack([ks, vs], axis=2), win


def nsa_sample(xn, cmp_pool, sel_pool, win_state, page_table, w_in, w_o, cmp_params):
    B, T, _ = xn.shape
    pos = PAST_LEN + jnp.arange(T, dtype=jnp.int32)
    q, g, kc, vc, ks, vs, kw, vw = nsa_project(xn, w_in, pos)
    wb = win_state.shape[1]

    def one(a):
        pt, qq, gg, kc_, vc_, ks_, vs_, kw_, vw_, ws = a
        cp = cmp_pool[pt].reshape(-1, 2, NSA_KV_HEADS, HEAD_DIM)
        sp = sel_pool[pt].reshape(-1, 2, NSA_KV_HEADS, HEAD_DIM)
        return nsa_seq(qq, gg, pos,
                       jnp.concatenate([cp[:, 0], kc_], axis=0), jnp.concatenate([cp[:, 1], vc_], axis=0),
                       jnp.concatenate([sp[:, 0], ks_], axis=0), jnp.concatenate([sp[:, 1], vs_], axis=0),
                       jnp.concatenate([ws[:, 0], kw_], axis=0), jnp.concatenate([ws[:, 1], vw_], axis=0),
                       PAST_LEN - wb, cmp_params)

    o = lax.map(one, (page_table, q, g, kc, vc, ks, vs, kw, vw, win_state))
    new_win = jnp.concatenate([win_state, jnp.stack([kw, vw], axis=2)], axis=1)[:, T:]
    return o.reshape(B, T, -1) @ w_o, jnp.stack([kc, vc], axis=2), jnp.stack([ks, vs], axis=2), new_win


def block_tail(h, p, g_ffn, g_ple, w_in, w_out, w_gate, w_proj):
    z = rms_norm(h, g_ffn) @ w_in
    h = h + (jax.nn.silu(z[..., :FFN_HIDDEN]) * z[..., FFN_HIDDEN:]) @ w_out
    return h + jax.nn.sigmoid(rms_norm(h, g_ple) @ w_gate) * (p @ w_proj)


def setup_inputs(seed: int = 0) -> dict:
    key = jax.random.key(seed)
    keys = list(jax.random.split(key, 48))

    def nrm(shape, scale):
        return jax.random.normal(keys.pop(), shape, jnp.float32) * scale

    n_pages = PAST_LEN // PAGE_SIZE
    in_use = DEC_BATCH * n_pages
    n_pool = in_use + max(1, in_use // 4)
    wb = min(WINDOW, PAST_LEN)
    hd = HEAD_DIM
    u = jax.random.uniform(keys.pop(), (N_LAYERS_B, LRU_WIDTH), jnp.float32, 0.9, 0.999)
    a0 = u ** (1.0 / LRU_C)
    page_table = jax.random.permutation(keys.pop(), n_pool)[:in_use].reshape(DEC_BATCH, n_pages).astype(jnp.int32)
    nsa_in = N_HEADS * hd + 6 * NSA_KV_HEADS * hd + 3 * N_HEADS
    return {
        'x_prompt': nrm((BATCH, SEQ, D_MODEL), 1.0),
        'x_sample': nrm((DEC_BATCH, DEC_SEQ, D_MODEL), 1.0),
        'p_prompt': nrm((DEPTH, BATCH, SEQ, PLE_DIM), 1.0),
        'p_sample': nrm((DEPTH, DEC_BATCH, DEC_SEQ, PLE_DIM), 1.0),
        'cache_moba_kv': nrm((N_LAYERS_A, n_pool, PAGE_SIZE, 2, MOBA_KV_HEADS, hd), 1.0),
        'cache_nsa_cmp_kv': nrm((N_LAYERS_C, n_pool, PAGE_SIZE, 2, NSA_KV_HEADS, hd), 1.0),
        'cache_nsa_sel_kv': nrm((N_LAYERS_C, n_pool, PAGE_SIZE, 2, NSA_KV_HEADS, hd), 1.0),
        'state_nsa_win_kv': nrm((N_LAYERS_C, DEC_BATCH, wb, 2, NSA_KV_HEADS, hd), 1.0),
        'state_lru_conv': nrm((N_LAYERS_B, DEC_BATCH, CONV_WIDTH - 1, LRU_WIDTH), 1.0),
        'state_lru_h': nrm((N_LAYERS_B, DEC_BATCH, LRU_WIDTH), 0.5),
        'page_table': page_table,
        'norm_mix': 1.0 + nrm((DEPTH, D_MODEL), 0.05),
        'norm_ffn': 1.0 + nrm((DEPTH, D_MODEL), 0.05),
        'norm_ple': 1.0 + nrm((DEPTH, D_MODEL), 0.05),
        'norm_out': 1.0 + nrm((D_MODEL,), 0.05),
        'moba_w_qkv': nrm((N_LAYERS_A, D_MODEL, (N_HEADS + 2 * MOBA_KV_HEADS) * hd), D_MODEL ** -0.5),
        'moba_w_o': nrm((N_LAYERS_A, N_HEADS * hd, D_MODEL), (N_HEADS * hd) ** -0.5),
        'lru_w_in': nrm((N_LAYERS_B, D_MODEL, 2 * LRU_WIDTH), D_MODEL ** -0.5),
        'lru_conv_w': nrm((N_LAYERS_B, CONV_WIDTH, LRU_WIDTH), CONV_WIDTH ** -0.5),
        'lru_conv_b': nrm((N_LAYERS_B, LRU_WIDTH), 0.01),
        'lru_w_a': nrm((N_LAYERS_B, LRU_BLOCKS, LRU_BLOCK_WIDTH, LRU_BLOCK_WIDTH), LRU_BLOCK_WIDTH ** -0.5),
        'lru_b_a': nrm((N_LAYERS_B, LRU_WIDTH), 0.01),
        'lru_w_x': nrm((N_LAYERS_B, LRU_BLOCKS, LRU_BLOCK_WIDTH, LRU_BLOCK_WIDTH), LRU_BLOCK_WIDTH ** -0.5),
        'lru_b_x': nrm((N_LAYERS_B, LRU_WIDTH), 0.01),
        'lru_lambda': jnp.log(a0) - jnp.log1p(-a0),
        'lru_w_o': nrm((N_LAYERS_B, LRU_WIDTH, D_MODEL), LRU_WIDTH ** -0.5),
        'nsa_w_in': nrm((N_LAYERS_C, D_MODEL, nsa_in), D_MODEL ** -0.5),
        'nsa_w_o': nrm((N_LAYERS_C, N_HEADS * hd, D_MODEL), (N_HEADS * hd) ** -0.5),
        'nsa_cmp_pos_k': nrm((N_LAYERS_C, CMP_BLOCK, hd), 0.1),
        'nsa_cmp_pos_v': nrm((N_LAYERS_C, CMP_BLOCK, hd), 0.1),
        'nsa_cmp_k_w1': nrm((N_LAYERS_C, CMP_BLOCK * hd, CMP_HIDDEN), (CMP_BLOCK * hd) ** -0.5),
        'nsa_cmp_k_w2': nrm((N_LAYERS_C, CMP_HIDDEN, hd), CMP_HIDDEN ** -0.5),
        'nsa_cmp_v_w1': nrm((N_LAYERS_C, CMP_BLOCK * hd, CMP_HIDDEN), (CMP_BLOCK * hd) ** -0.5),
        'nsa_cmp_v_w2': nrm((N_LAYERS_C, CMP_HIDDEN, hd), CMP_HIDDEN ** -0.5),
        'ffn_w_in': nrm((DEPTH, D_MODEL, 2 * FFN_HIDDEN), D_MODEL ** -0.5),
        'ffn_w_out': nrm((DEPTH, FFN_HIDDEN, D_MODEL), FFN_HIDDEN ** -0.5),
        'ple_w_gate': nrm((DEPTH, D_MODEL, D_MODEL), D_MODEL ** -0.5),
        'ple_w_proj': nrm((DEPTH, PLE_DIM, D_MODEL), PLE_DIM ** -0.5),
    }


def reference(x_prompt, x_sample, p_prompt, p_sample, cache_moba_kv, cache_nsa_cmp_kv, cache_nsa_sel_kv,
              state_nsa_win_kv, state_lru_conv, state_lru_h, page_table, norm_mix, norm_ffn, norm_ple, norm_out,
              moba_w_qkv, moba_w_o, lru_w_in, lru_conv_w, lru_conv_b, lru_w_a, lru_b_a, lru_w_x, lru_b_x,
              lru_lambda, lru_w_o, nsa_w_in, nsa_w_o, nsa_cmp_pos_k, nsa_cmp_pos_v, nsa_cmp_k_w1, nsa_cmp_k_w2,
              nsa_cmp_v_w1, nsa_cmp_v_w2, ffn_w_in, ffn_w_out, ple_w_gate, ple_w_proj):
    hp, hs = x_prompt, x_sample
    moba_p, moba_s = [], []
    cmp_p, cmp_s, sel_p, sel_s, win_p, win_s = [], [], [], [], [], []
    conv_p, conv_s, hh_p, hh_s = [], [], [], []
    for i in range(DEPTH):
        kind, j = i % N_MIXERS, i // N_MIXERS
        xp = rms_norm(hp, norm_mix[i])
        xs = rms_norm(hs, norm_mix[i])
        if kind == 0:
            mp, kvp = moba_prompt(xp, moba_w_qkv[j], moba_w_o[j])
            ms, kvs = moba_sample(xs, cache_moba_kv[j], page_table, moba_w_qkv[j], moba_w_o[j])
            moba_p.append(kvp)
            moba_s.append(kvs)
        elif kind == 1:
            lru_args = (lru_w_in[j], lru_conv_w[j], lru_conv_b[j], lru_w_a[j], lru_b_a[j], lru_w_x[j],
                        lru_b_x[j], lru_lambda[j], lru_w_o[j])
            zc = jnp.zeros((xp.shape[0], CONV_WIDTH - 1, LRU_WIDTH), xp.dtype)
            zh = jnp.zeros((xp.shape[0], LRU_WIDTH), xp.dtype)
            mp, cvp, hp_state = rglru_mixer(xp, zc, zh, *lru_args)
            ms, cvs, hs_state = rglru_mixer(xs, state_lru_conv[j], state_lru_h[j], *lru_args)
            conv_p.append(cvp)
            conv_s.append(cvs)
            hh_p.append(hp_state)
            hh_s.append(hs_state)
        else:
            cmp_params = (nsa_cmp_pos_k[j], nsa_cmp_pos_v[j], nsa_cmp_k_w1[j], nsa_cmp_k_w2[j],
                          nsa_cmp_v_w1[j], nsa_cmp_v_w2[j])
            mp, c_p, s_p, w_p = nsa_prompt(xp, nsa_w_in[j], nsa_w_o[j], cmp_params)
            ms, c_s, s_s, w_s = nsa_sample(xs, cache_nsa_cmp_kv[j], cache_nsa_sel_kv[j], state_nsa_win_kv[j],
                                           page_table, nsa_w_in[j], nsa_w_o[j], cmp_params)
            cmp_p.append(c_p)
            cmp_s.append(c_s)
            sel_p.append(s_p)
            sel_s.append(s_s)
            win_p.append(w_p)
            win_s.append(w_s)
        hp = block_tail(hp + mp, p_prompt[i], norm_ffn[i], norm_ple[i], ffn_w_in[i], ffn_w_out[i],
                        ple_w_gate[i], ple_w_proj[i])
        hs = block_tail(hs + ms, p_sample[i], norm_ffn[i], norm_ple[i], ffn_w_in[i], ffn_w_out[i],
                        ple_w_gate[i], ple_w_proj[i])
    y_prompt = rms_norm(hp, norm_out)
    y_sample = rms_norm(hs, norm_out)
    return (y_prompt, y_sample,
            jnp.stack(moba_p), jnp.stack(moba_s),
            jnp.stack(cmp_p), jnp.stack(cmp_s),
            jnp.stack(sel_p), jnp.stack(sel_s),
            jnp.stack(win_p), jnp.stack(win_s),
            jnp.stack(conv_p), jnp.stack(conv_s),
            jnp.stack(hh_p), jnp.stack(hh_s))
```

```python
import functools

import jax
import jax.numpy as jnp
from jax import lax
from jax.experimental import pallas as pl
from jax.experimental.pallas import tpu as pltpu

F32 = jnp.float32
BF16 = jnp.bfloat16

D_MODEL = 1024
N_HEADS = 8
HEAD_DIM = 128
ROPE_THETA = 10000.0
RMS_EPS = 1e-6
N_MIXERS = 3
PAGE_SIZE = 128
MOBA_KV_HEADS = 4
MOBA_BLOCK = 256
MOBA_TOPK = 3
LRU_BLOCK_WIDTH = 256
CONV_WIDTH = 4
LRU_C = 8.0
NSA_KV_HEADS = 2
GROUP_C = N_HEADS // NSA_KV_HEADS
CMP_BLOCK = 32
CMP_STRIDE = 16
SEL_BLOCK = 64
SEL_TOPN = 16
WINDOW = 512

LANES = 128
SUBLANES = 8
ROW_TILE = 256
VMEM_LIMIT = 56 * 1024 * 1024

NEG = -1e30
SCALE = HEAD_DIM ** -0.5


def _cparams(n_axes):
    return pltpu.CompilerParams(dimension_semantics=("arbitrary",) * n_axes, vmem_limit_bytes=VMEM_LIMIT)


def _whole():
    return pl.BlockSpec(memory_space=pltpu.VMEM)


def _gelu(x):
    return 0.5 * x * (1.0 + jnp.tanh(0.7978845608028654 * (x + 0.044715 * (x * x * x))))


def _sigmoid(x):
    return 1.0 / (1.0 + jnp.exp(-x))


def _dot(a, b):
    return jnp.dot(a, b, preferred_element_type=F32)


def _dot_t(a, b):
    return lax.dot_general(a, b, (((1,), (1,)), ((), ())), preferred_element_type=F32)


def _rms(x, g):
    ms = jnp.mean(x * x, axis=-1, keepdims=True)
    return x * lax.rsqrt(ms + RMS_EPS) * g


def _rank_before(score, lane, n):
    rank = jnp.zeros(score.shape, jnp.int32)
    for jp in range(n):
        col = score[:, jp:jp + 1]
        beats = (col > score) | ((col == score) & (jp < lane))
        rank = rank + beats.astype(jnp.int32)
    return rank


def _proj_kernel(x_ref, g_ref, w_ref, cos_ref, sin_ref, *out_refs, segs):
    xn = _rms(x_ref[...], g_ref[...]).astype(BF16)
    for (oi, ocol, wcol, width, kind) in segs:
        z = _dot(xn, w_ref[:, wcol:wcol + width])
        if kind == "rope":
            cos = cos_ref[...]
            sin = sin_ref[...]
            parts = []
            for c in range(width // HEAD_DIM):
                zh = z[:, c * HEAD_DIM:(c + 1) * HEAD_DIM]
                parts.append(zh * cos + pltpu.roll(zh, HEAD_DIM // 2, 1) * sin)
            z = parts[0] if len(parts) == 1 else jnp.concatenate(parts, axis=1)
        elif kind == "sigmoid":
            z = _sigmoid(z)
        elif kind == "gelu":
            z = _gelu(z)
        out_refs[oi][:, ocol:ocol + width] = z.astype(out_refs[oi].dtype)


def _norm_proj(h, g, w_bf16, cos_t, sin_t, segs, outs, n_prompt_tiles, tiles_per_seq):
    m = h.shape[0]
    n = w_bf16.shape[1]
    grid = (m // ROW_TILE,)

    def tab_map(i):
        return (jnp.where(i < n_prompt_tiles, i % tiles_per_seq, tiles_per_seq), 0)

    return pl.pallas_call(
        functools.partial(_proj_kernel, segs=tuple(segs)),
        grid=grid,
        in_specs=[
            pl.BlockSpec((ROW_TILE, D_MODEL), lambda i: (i, 0)),
            pl.BlockSpec((1, D_MODEL), lambda i: (0, 0)),
            pl.BlockSpec((D_MODEL, n), lambda i: (0, 0)),
            pl.BlockSpec((ROW_TILE, HEAD_DIM), tab_map),
            pl.BlockSpec((ROW_TILE, HEAD_DIM), tab_map),
        ],
        out_specs=[pl.BlockSpec((ROW_TILE, nc), lambda i: (i, 0)) for nc, _ in outs],
        out_shape=[jax.ShapeDtypeStruct((m, nc), dt) for nc, dt in outs],
        compiler_params=_cparams(1),
        name="norm_proj",
    )(h, g.reshape(1, D_MODEL), w_bf16, cos_t, sin_t)


def _tail_kernel(a_ref, h_ref, p_ref, wo_ref, gf_ref, wi_ref, wout_ref, gp_ref, wg_ref, wp_ref, gout_ref,
                 h_out_ref, *y_out_ref, ffn, chunk):
    h1 = h_ref[...] + _dot(a_ref[...], wo_ref[...])
    xn = _rms(h1, gf_ref[...]).astype(BF16)
    acc = jnp.zeros(h1.shape, F32)
    for c in range(ffn // chunk):
        zg = _dot(xn, wi_ref[:, c * chunk:(c + 1) * chunk])
        zu = _dot(xn, wi_ref[:, ffn + c * chunk:ffn + (c + 1) * chunk])
        act = (zg * _sigmoid(zg) * zu).astype(BF16)
        acc = acc + _dot(act, wout_ref[c * chunk:(c + 1) * chunk, :])
    h2 = h1 + acc
    xn2 = _rms(h2, gp_ref[...]).astype(BF16)
    gate = _sigmoid(_dot(xn2, wg_ref[...]))
    h3 = h2 + gate * _dot(p_ref[...].astype(BF16), wp_ref[...])
    h_out_ref[...] = h3
    if y_out_ref:
        y_out_ref[0][...] = _rms(h3, gout_ref[...])


def _tail(a, h, p, w_o, g_ffn, w_in, w_out, g_ple, w_gate, w_proj, g_out, final):
    m = h.shape[0]
    ffn = w_out.shape[0]
    ple = p.shape[1]
    row = lambda i: (i, 0)
    vec = pl.BlockSpec((1, D_MODEL), lambda i: (0, 0))
    n_out = 2 if final else 1
    res = pl.pallas_call(
        functools.partial(_tail_kernel, ffn=ffn, chunk=256),
        grid=(m // ROW_TILE,),
        in_specs=[
            pl.BlockSpec((ROW_TILE, D_MODEL), row),
            pl.BlockSpec((ROW_TILE, D_MODEL), row),
            pl.BlockSpec((ROW_TILE, ple), row),
            _whole(), vec, _whole(), _whole(), vec, _whole(), _whole(), vec,
        ],
        out_specs=[pl.BlockSpec((ROW_TILE, D_MODEL), row)] * n_out,
        out_shape=[jax.ShapeDtypeStruct((m, D_MODEL), F32)] * n_out,
        compiler_params=_cparams(1),
        name="tail",
    )(a, h, p, w_o, g_ffn.reshape(1, -1), w_in, w_out, g_ple.reshape(1, -1), w_gate, w_proj, g_out.reshape(1, -1))
    return res


def _moba_prompt_kernel(q_ref, k_ref, v_ref, o_ref, kmean_ref, kb_ref, vb_ref, *, nb):
    i = pl.program_id(2)
    blk = MOBA_BLOCK

    @pl.when(i == 0)
    def _():
        k = k_ref[...]
        km = jnp.sum(k.reshape(nb, blk, HEAD_DIM), axis=1) * (1.0 / blk)
        kmean_ref[...] = jnp.concatenate([km, jnp.zeros((LANES - nb, HEAD_DIM), F32)], axis=0).astype(BF16)
        kb_ref[...] = k.astype(BF16)
        vb_ref[...] = v_ref[...].astype(BF16)

    lane = lax.broadcasted_iota(jnp.int32, (blk, LANES), 1)
    valid = lane < i
    row = lax.broadcasted_iota(jnp.int32, (blk, blk), 0)
    col = lax.broadcasted_iota(jnp.int32, (blk, blk), 1)
    causal = col <= row
    for g in range(2):
        q = q_ref[:, g * HEAD_DIM:(g + 1) * HEAD_DIM]
        gate = jnp.where(valid, _dot_t(q, kmean_ref[...]), NEG)
        sel = jnp.where(valid & (_rank_before(gate, lane, nb) < MOBA_TOPK), 1.0, 0.0)

        def block(s, mask, vblk, carry):
            m, l, acc = carry
            s = jnp.where(mask, s * SCALE, NEG)
            m_new = jnp.maximum(m, jnp.max(s, axis=1, keepdims=True))
            p = jnp.where(mask, jnp.exp(s - m_new), 0.0)
            alpha = jnp.exp(m - m_new)
            l = alpha * l + jnp.sum(p, axis=1, keepdims=True)
            acc = alpha * acc + _dot(p.astype(BF16), vblk)
            return m_new, l, acc

        def past(b, carry):
            off = pl.multiple_of(b * blk, blk)
            kblk = kb_ref[pl.ds(off, blk), :]
            vblk = vb_ref[pl.ds(off, blk), :]
            selcol = jnp.max(jnp.where(lane == b, sel, 0.0), axis=1, keepdims=True)
            return block(_dot_t(q, kblk), selcol > 0.5, vblk, carry)

        init = (jnp.full((blk, 1), NEG, F32), jnp.zeros((blk, 1), F32), jnp.zeros((blk, HEAD_DIM), F32))
        carry = lax.fori_loop(0, i, past, init)
        off = pl.multiple_of(i * blk, blk)
        m, l, acc = block(_dot_t(q, kb_ref[pl.ds(off, blk), :]), causal, vb_ref[pl.ds(off, blk), :], carry)
        o_ref[:, g * HEAD_DIM:(g + 1) * HEAD_DIM] = (acc / jnp.maximum(l, 1e-30)).astype(o_ref.dtype)


def _moba_prompt(q, kv, batch, seq):
    nb = seq // MOBA_BLOCK
    kvh = MOBA_KV_HEADS
    return pl.pallas_call(
        functools.partial(_moba_prompt_kernel, nb=nb),
        grid=(batch, kvh, nb),
        in_specs=[
            pl.BlockSpec((MOBA_BLOCK, 2 * HEAD_DIM), lambda b, j, i: (b * nb + i, j)),
            pl.BlockSpec((seq, HEAD_DIM), lambda b, j, i: (b, j)),
            pl.BlockSpec((seq, HEAD_DIM), lambda b, j, i: (b, kvh + j)),
        ],
        out_specs=pl.BlockSpec((MOBA_BLOCK, 2 * HEAD_DIM), lambda b, j, i: (b * nb + i, j)),
        out_shape=jax.ShapeDtypeStruct((batch * seq, N_HEADS * HEAD_DIM), BF16),
        scratch_shapes=[
            pltpu.VMEM((LANES, HEAD_DIM), BF16),
            pltpu.VMEM((seq, HEAD_DIM), BF16),
            pltpu.VMEM((seq, HEAD_DIM), BF16),
        ],
        compiler_params=_cparams(3),
        name="moba_prompt",
    )(q, kv, kv)


def _head_rows(q, n_kv, group):
    t = q.shape[0]
    rows = []
    for h in range(n_kv * group):
        qh = q[:, h * HEAD_DIM:(h + 1) * HEAD_DIM]
        z = jnp.zeros((t, HEAD_DIM), q.dtype)
        rows.append(jnp.concatenate([qh if j == h // group else z for j in range(n_kv)], axis=1))
    return jnp.concatenate(rows, axis=0)


def _diag_heads(x, n_kv, group, t):
    parts = []
    for h in range(n_kv * group):
        j = h // group
        parts.append(x[h * t:(h + 1) * t, j * HEAD_DIM:(j + 1) * HEAD_DIM])
    return jnp.concatenate(parts, axis=0)


def _rows_to_tokens(o, n_heads, t):
    return jnp.concatenate([o[h * t:(h + 1) * t, :] for h in range(n_heads)], axis=1)


def _pad_rows(x, rows):
    return jnp.concatenate([x, jnp.zeros((rows - x.shape[0], x.shape[1]), x.dtype)], axis=0)


def _moba_sample_kernel(pt_ref, q_ref, kvn_ref, page_ref, o_ref,
                        qf_ref, qb_ref, g_ref, mx_ref, lx_ref, po_ref, ksum_ref, m_ref, l_ref, acc_ref,
                        *, n_pages, t_new, past_len):
    del pt_ref
    p = pl.program_id(1)
    kvh = MOBA_KV_HEADS
    group = N_HEADS // kvh
    kw = kvh * HEAD_DIM
    rows = N_HEADS * t_new
    pages_per_block = MOBA_BLOCK // PAGE_SIZE
    n_past = n_pages // pages_per_block

    @pl.when(p == 0)
    def _():
        qr = _head_rows(q_ref[...], kvh, group)
        qf_ref[...] = qr
        qb_ref[...] = qr.astype(BF16)
        g_ref[...] = jnp.zeros(g_ref.shape, F32)
        mx_ref[...] = jnp.zeros(mx_ref.shape, F32)
        lx_ref[...] = jnp.zeros(lx_ref.shape, F32)

    @pl.when(p % pages_per_block == 0)
    def _():
        m_ref[...] = jnp.full(m_ref.shape, NEG, F32)
        l_ref[...] = jnp.zeros(l_ref.shape, F32)
        acc_ref[...] = jnp.zeros(acc_ref.shape, F32)
        ksum_ref[...] = jnp.zeros(ksum_ref.shape, F32)

    kp = page_ref[:, :kw]
    vp = page_ref[:, kw:]
    ksum_ref[...] += jnp.sum(kp, axis=0, keepdims=True)
    s = _dot_t(qb_ref[...], kp.astype(BF16)) * SCALE
    m_old = m_ref[...]
    m_new = jnp.maximum(m_old, jnp.max(s, axis=1, keepdims=True))
    e = jnp.exp(s - m_new)
    alpha = jnp.exp(m_old - m_new)
    l_ref[...] = alpha * l_ref[...] + jnp.sum(e, axis=1, keepdims=True)
    pv = _diag_heads(_dot(e.astype(BF16), vp.astype(BF16)), kvh, group, t_new)
    acc_ref[...] = alpha * acc_ref[...] + pv
    m_ref[...] = m_new

    lane = lax.broadcasted_iota(jnp.int32, (rows, LANES), 1)

    @pl.when(p % pages_per_block == pages_per_block - 1)
    def _():
        b = p // pages_per_block
        kmean = ksum_ref[...] * (1.0 / MOBA_BLOCK)
        gate = jnp.sum(qf_ref[...] * kmean, axis=1, keepdims=True)
        here = lane == b
        g_ref[...] = jnp.where(here, gate, g_ref[...])
        mx_ref[...] = jnp.where(here, m_ref[...], mx_ref[...])
        lx_ref[...] = jnp.where(here, l_ref[...], lx_ref[...])
        po_ref[b] = acc_ref[...]

    @pl.when(p == n_pages - 1)
    def _():
        cur = past_len // MOBA_BLOCK
        valid = lane < cur
        gate = jnp.where(valid, g_ref[...], NEG)
        sel = valid & (_rank_before(gate, lane, n_past) < MOBA_TOPK)
        kn = _pad_rows(kvn_ref[:, :kw], LANES).astype(BF16)
        vn = _pad_rows(kvn_ref[:, kw:], LANES).astype(BF16)
        rowi = lax.broadcasted_iota(jnp.int32, (rows, LANES), 0)
        mask_n = lane <= (rowi % t_new)
        sn = jnp.where(mask_n, _dot_t(qb_ref[...], kn) * SCALE, NEG)
        m_tot = jnp.maximum(jnp.max(sn, axis=1, keepdims=True),
                            jnp.max(jnp.where(sel, mx_ref[...], NEG), axis=1, keepdims=True))
        pn = jnp.where(mask_n, jnp.exp(sn - m_tot), 0.0)
        w = jnp.where(sel, jnp.exp(mx_ref[...] - m_tot), 0.0)
        l_tot = jnp.sum(pn, axis=1, keepdims=True) + jnp.sum(w * lx_ref[...], axis=1, keepdims=True)
        o = _diag_heads(_dot(pn.astype(BF16), vn), kvh, group, t_new)
        for b in range(n_past):
            o = o + w[:, b:b + 1] * po_ref[b]
        o = o / jnp.maximum(l_tot, 1e-30)
        o_ref[...] = _rows_to_tokens(o, N_HEADS, t_new)


def _moba_sample(q_s, kv_s, cache, layer, page_table, past_len):
    n_seq, t_new, _ = q_s.shape
    n_pages = page_table.shape[1]
    kw2 = 2 * MOBA_KV_HEADS * HEAD_DIM
    rows = N_HEADS * t_new
    n_past = n_pages * PAGE_SIZE // MOBA_BLOCK
    assert n_past <= LANES and past_len == n_pages * PAGE_SIZE and past_len % MOBA_BLOCK == 0
    grid_spec = pltpu.PrefetchScalarGridSpec(
        num_scalar_prefetch=1,
        grid=(n_seq, n_pages),
        in_specs=[
            pl.BlockSpec((None, t_new, N_HEADS * HEAD_DIM), lambda s, p, pt: (s, 0, 0)),
            pl.BlockSpec((None, t_new, kw2), lambda s, p, pt: (s, 0, 0)),
            pl.BlockSpec((None, None, PAGE_SIZE, kw2), lambda s, p, pt: (layer, pt[s, p], 0, 0)),
        ],
        out_specs=pl.BlockSpec((None, t_new, N_HEADS * HEAD_DIM), lambda s, p, pt: (s, 0, 0)),
        scratch_shapes=[
            pltpu.VMEM((rows, kw2 // 2), F32),
            pltpu.VMEM((rows, kw2 // 2), BF16),
            pltpu.VMEM((rows, LANES), F32),
            pltpu.VMEM((rows, LANES), F32),
            pltpu.VMEM((rows, LANES), F32),
            pltpu.VMEM((n_past, rows, HEAD_DIM), F32),
            pltpu.VMEM((1, kw2 // 2), F32),
            pltpu.VMEM((rows, 1), F32),
            pltpu.VMEM((rows, 1), F32),
            pltpu.VMEM((rows, HEAD_DIM), F32),
        ],
    )
    return pl.pallas_call(
        functools.partial(_moba_sample_kernel, n_pages=n_pages, t_new=t_new, past_len=past_len),
        grid_spec=grid_spec,
        out_shape=jax.ShapeDtypeStruct((n_seq, t_new, N_HEADS * HEAD_DIM), F32),
        compiler_params=_cparams(2),
        name="moba_sample",
    )(page_table, q_s, kv_s, cache)


def _lru_kernel(u_ref, gact_ref, prev0_ref, h0_ref, cw_ref, cb_ref, wa_ref, ba_ref, wx_ref, bx_ref, lam_ref,
                a_out_ref, hlast_ref, prev_s, h_s, a_s, b_s, y_s, *, tt):
    j = pl.program_id(1)
    width = u_ref.shape[1]

    @pl.when(j == 0)
    def _():
        if tt > SUBLANES:
            prev_s[0:tt - SUBLANES, :] = jnp.zeros((tt - SUBLANES, width), F32)
        prev_s[tt - SUBLANES:tt, :] = prev0_ref[...]
        h_s[...] = h0_ref[...]

    u = u_ref[...]
    prev = prev_s[...]
    rowi = lax.broadcasted_iota(jnp.int32, u.shape, 0)
    xc = cb_ref[...]
    for k in range(CONV_WIDTH):
        sh = CONV_WIDTH - 1 - k
        if sh == 0:
            us = u
        else:
            us = jnp.where(rowi < sh, pltpu.roll(prev, sh, 0), pltpu.roll(u, sh, 0))
        xc = xc + us * cw_ref[k:k + 1, :]
    prev_s[...] = u

    xb = xc.astype(BF16)
    nblk = width // LRU_BLOCK_WIDTH
    ra, rx = [], []
    for n in range(nblk):
        xs = xb[:, n * LRU_BLOCK_WIDTH:(n + 1) * LRU_BLOCK_WIDTH]
        ra.append(_dot(xs, wa_ref[n]))
        rx.append(_dot(xs, wx_ref[n]))
    r = _sigmoid(jnp.concatenate(ra, axis=1) + ba_ref[...])
    ig = _sigmoid(jnp.concatenate(rx, axis=1) + bx_ref[...])
    nl = -lam_ref[...]
    softplus = jnp.maximum(nl, 0.0) + jnp.log(1.0 + jnp.exp(-jnp.abs(nl)))
    log_a = -LRU_C * r * softplus
    a = jnp.exp(log_a)
    a_s[...] = a
    b_s[...] = jnp.sqrt(1.0 - a * a) * ig * xc

    def step(t, h):
        h = a_s[pl.ds(t, 1), :] * h + b_s[pl.ds(t, 1), :]
        y_s[pl.ds(t, 1), :] = h
        return h

    h = lax.fori_loop(0, tt, step, h_s[...], unroll=8)
    h_s[...] = h
    hlast_ref[...] = h
    a_out_ref[...] = (gact_ref[...] * y_s[...]).astype(a_out_ref.dtype)


def _lru(u, gact, row0, n_seq, seq_len, tt, prev0, h0, cw, cb, wa, ba, wx, bx, lam, out_dtype):
    width = u.shape[1]
    tps = seq_len // tt
    base = row0 // tt
    rowmap = lambda s, j: (base + s * tps + j, 0)
    vec = pl.BlockSpec((1, width), lambda s, j: (0, 0))
    return pl.pallas_call(
        functools.partial(_lru_kernel, tt=tt),
        grid=(n_seq, tps),
        in_specs=[
            pl.BlockSpec((tt, width), rowmap),
            pl.BlockSpec((tt, width), rowmap),
            pl.BlockSpec((None, SUBLANES, width), lambda s, j: (s, 0, 0)),
            pl.BlockSpec((None, 1, width), lambda s, j: (s, 0, 0)),
            pl.BlockSpec((CONV_WIDTH, width), lambda s, j: (0, 0)),
            vec, _whole(), vec, _whole(), vec, vec,
        ],
        out_specs=[
            pl.BlockSpec((tt, width), lambda s, j: (s * tps + j, 0)),
            pl.BlockSpec((None, 1, width), lambda s, j: (s, 0, 0)),
        ],
        out_shape=[
            jax.ShapeDtypeStruct((n_seq * seq_len, width), out_dtype),
            jax.ShapeDtypeStruct((n_seq, 1, width), F32),
        ],
        scratch_shapes=[
            pltpu.VMEM((tt, width), F32),
            pltpu.VMEM((1, width), F32),
            pltpu.VMEM((tt, width), F32),
            pltpu.VMEM((tt, width), F32),
            pltpu.VMEM((tt, width), F32),
        ],
        compiler_params=_cparams(2),
        name="rglru",
    )(u, gact, prev0, h0, cw, cb.reshape(1, -1), wa, ba.reshape(1, -1), wx, bx.reshape(1, -1), lam.reshape(1, -1))


def _page_copy_kernel(pt_ref, x_ref, o_ref):
    del pt_ref
    o_ref[...] = x_ref[...]


def _gather_pages(cache, layer, page_table):
    n_seq, n_pages = page_table.shape
    c = cache.shape[-1]
    grid_spec = pltpu.PrefetchScalarGridSpec(
        num_scalar_prefetch=1,
        grid=(n_seq, n_pages),
        in_specs=[pl.BlockSpec((None, None, PAGE_SIZE, c), lambda s, p, pt: (layer, pt[s, p], 0, 0))],
        out_specs=pl.BlockSpec((PAGE_SIZE, c), lambda s, p, pt: (s * n_pages + p, 0)),
    )
    return pl.pallas_call(
        _page_copy_kernel,
        grid_spec=grid_spec,
        out_shape=jax.ShapeDtypeStruct((n_seq * n_pages * PAGE_SIZE, c), cache.dtype),
        compiler_params=_cparams(2),
        name="gather_pages",
    )(page_table, cache)


def _cmp_half_kernel(*refs, n_half):
    x_refs, (wk_ref, wv_ref, o_ref) = refs[:-3], refs[-3:]
    for c in range(2 * NSA_KV_HEADS):
        w_ref = wk_ref if c < NSA_KV_HEADS else wv_ref
        acc = jnp.zeros((n_half, 2 * HEAD_DIM), F32)
        for l in range(CMP_STRIDE):
            x = x_refs[c][pl.ds(l, n_half, stride=CMP_STRIDE), :]
            acc = acc + _dot(x.astype(BF16), w_ref[l])
        o_ref[:, c * 2 * HEAD_DIM:(c + 1) * 2 * HEAD_DIM] = acc


def _cmp_finish_kernel(ab_ref, pek_ref, pev_ref, w1k_ref, w1v_ref, w2k_ref, w2v_ref, o_ref, *, n_half):
    nc = n_half - (CMP_BLOCK // CMP_STRIDE - 1)
    rowi = lax.broadcasted_iota(jnp.int32, (n_half, HEAD_DIM), 0)
    for c in range(2 * NSA_KV_HEADS):
        is_k = c < NSA_KV_HEADS
        pe = (pek_ref if is_k else pev_ref)[...]
        w1 = (w1k_ref if is_k else w1v_ref)[...]
        w2 = (w2k_ref if is_k else w2v_ref)[...]
        bias = _dot(pe.astype(BF16), w1)[0:1, :]
        a = ab_ref[:, c * 2 * HEAD_DIM:c * 2 * HEAD_DIM + HEAD_DIM]
        b = ab_ref[:, c * 2 * HEAD_DIM + HEAD_DIM:(c + 1) * 2 * HEAD_DIM]
        hid = a + pltpu.roll(b, n_half - 1, 0) + bias
        tok = _dot(_gelu(hid).astype(BF16), w2)
        o_ref[c] = jnp.where(rowi < nc, tok, 0.0)


def _nsa_compress(x, row0, n_seq, seq_len, pe_k, pe_v, w1k_cat, w1v_cat, w1k, w1v, w2k, w2v):
    rows_per_step = 2048
    assert CMP_BLOCK == 2 * CMP_STRIDE and seq_len % rows_per_step == 0 and row0 % rows_per_step == 0
    n_half_step = rows_per_step // CMP_STRIDE
    n_steps = n_seq * seq_len // rows_per_step
    c4 = 2 * NSA_KV_HEADS
    ab = pl.pallas_call(
        functools.partial(_cmp_half_kernel, n_half=n_half_step),
        grid=(n_steps,),
        in_specs=[pl.BlockSpec((rows_per_step, HEAD_DIM), functools.partial(lambda i, c: (row0 // rows_per_step + i, c), c=c))
                  for c in range(c4)] + [_whole(), _whole()],
        out_specs=pl.BlockSpec((n_half_step, c4 * 2 * HEAD_DIM), lambda i: (i, 0)),
        out_shape=jax.ShapeDtypeStruct((n_steps * n_half_step, c4 * 2 * HEAD_DIM), F32),
        compiler_params=_cparams(1),
        name="nsa_cmp_half",
    )(*([x] * c4), w1k_cat, w1v_cat)
    n_half = seq_len // CMP_STRIDE
    return pl.pallas_call(
        functools.partial(_cmp_finish_kernel, n_half=n_half),
        grid=(n_seq,),
        in_specs=[pl.BlockSpec((n_half, c4 * 2 * HEAD_DIM), lambda s: (s, 0)),
                  _whole(), _whole(), _whole(), _whole(), _whole(), _whole()],
        out_specs=pl.BlockSpec((None, c4, n_half, HEAD_DIM), lambda s: (s, 0, 0, 0)),
        out_shape=jax.ShapeDtypeStruct((n_seq, c4, n_half, HEAD_DIM), F32),
        compiler_params=_cparams(1),
        name="nsa_cmp_finish",
    )(ab, pe_k, pe_v, w1k, w1v, w2k, w2v)


def _sel_map(n_cmp, n_lanes):
    c = lax.broadcasted_iota(jnp.int32, (n_cmp, n_lanes), 0)
    j = lax.broadcasted_iota(jnp.int32, (n_cmp, n_lanes), 1)
    d = c - j * (SEL_BLOCK // CMP_STRIDE)
    w = jnp.zeros((n_cmp, n_lanes), F32)
    for m in range(SEL_BLOCK // CMP_STRIDE):
        for n in range(CMP_BLOCK // CMP_STRIDE):
            w = w + jnp.where(d == m - n, 1.0, 0.0)
    return w


def _importance(p_sum, sel_map_bf16):
    hi = p_sum.astype(BF16)
    r1 = p_sum - hi.astype(F32)
    mid = r1.astype(BF16)
    lo = (r1 - mid.astype(F32)).astype(BF16)
    return _dot(hi, sel_map_bf16) + _dot(mid, sel_map_bf16) + _dot(lo, sel_map_bf16)


def _select_blocks(imp, cur, lane, n_sel):
    forced = (lane == 0) | (lane == cur) | (lane == cur - 1)
    allowed = lane <= cur
    score = jnp.where(allowed, jnp.where(forced, 1e30, imp), NEG)
    return allowed & (_rank_before(score, lane, n_sel) < SEL_TOPN)


def _masked_softmax_rows(s, mask):
    s = jnp.where(mask, s, NEG)
    m = jnp.max(s, axis=1, keepdims=True)
    e = jnp.where(mask, jnp.exp(s - m), 0.0)
    return e / jnp.maximum(jnp.sum(e, axis=1, keepdims=True), 1e-30)


def _nsa_prompt_kernel(q_ref, g_ref, ck_ref, cv_ref, ks_ref, vs_ref, kw_ref, vw_ref, o_ref,
                       ksb_ref, vsb_ref, kwb_ref, vwb_ref, *, seq, tq):
    i = pl.program_id(2)
    grp = GROUP_C
    n_cmp = ck_ref.shape[0]
    nc = n_cmp - (CMP_BLOCK // CMP_STRIDE - 1)
    n_sel = seq // SEL_BLOCK

    @pl.when(i == 0)
    def _():
        ksb_ref[...] = ks_ref[...].astype(BF16)
        vsb_ref[...] = vs_ref[...].astype(BF16)
        kwb_ref[...] = kw_ref[...].astype(BF16)
        vwb_ref[...] = vw_ref[...].astype(BF16)

    q4 = jnp.concatenate([q_ref[:, g * HEAD_DIM:(g + 1) * HEAD_DIM] for g in range(grp)], axis=0)
    pos = i * tq + lax.broadcasted_iota(jnp.int32, (tq, 1), 0)

    lane_c = lax.broadcasted_iota(jnp.int32, (tq, n_cmp), 1)
    avail = (lane_c * CMP_STRIDE + (CMP_BLOCK - 1) <= pos) & (lane_c < nc)
    s_c = (_dot_t(q4, ck_ref[...].astype(BF16)) * SCALE).reshape(grp, tq, n_cmp)
    s_c = jnp.where(avail[None], s_c, NEG)
    m_c = jnp.max(s_c, axis=2, keepdims=True)
    e_c = jnp.where(avail[None], jnp.exp(s_c - m_c), 0.0)
    p_c = e_c / jnp.maximum(jnp.sum(e_c, axis=2, keepdims=True), 1e-30)
    o_c = _dot(p_c.reshape(grp * tq, n_cmp).astype(BF16), cv_ref[...].astype(BF16))

    imp = _importance(jnp.sum(p_c, axis=0), _sel_map(n_cmp, LANES).astype(BF16))
    lane = lax.broadcasted_iota(jnp.int32, (tq, LANES), 1)
    selected = _select_blocks(imp, pos // SEL_BLOCK, lane, n_sel)
    sel_bf = jnp.where(selected, 1.0, 0.0).astype(BF16)

    kcol = lax.broadcasted_iota(jnp.int32, (tq, tq), 1)
    ej = lax.broadcasted_iota(jnp.int32, (LANES, tq), 0)
    ec = lax.broadcasted_iota(jnp.int32, (LANES, tq), 1)

    def sweep(kb_ref, vb_ref, mask_fn, lo):
        def body(b, carry):
            m, l, acc = carry
            off = pl.multiple_of(b * tq, tq)
            mask = mask_fn(b)
            s = (_dot_t(q4, kb_ref[pl.ds(off, tq), :]) * SCALE).reshape(grp, tq, tq)
            s = jnp.where(mask[None], s, NEG)
            m_new = jnp.maximum(m, jnp.max(s, axis=2, keepdims=True))
            p = jnp.where(mask[None], jnp.exp(s - m_new), 0.0)
            alpha = jnp.exp(m - m_new)
            l = alpha * l + jnp.sum(p, axis=2, keepdims=True)
            pv = _dot(p.reshape(grp * tq, tq).astype(BF16), vb_ref[pl.ds(off, tq), :])
            acc = alpha * acc + pv.reshape(grp, tq, HEAD_DIM)
            return m_new, l, acc

        init = (jnp.full((grp, tq, 1), NEG, F32), jnp.zeros((grp, tq, 1), F32),
                jnp.zeros((grp, tq, HEAD_DIM), F32))
        m, l, acc = lax.fori_loop(lo, i + 1, body, init)
        return (acc / jnp.maximum(l, 1e-30)).reshape(grp * tq, HEAD_DIM)

    def sel_mask(b):
        expand = jnp.where(ej == b * (tq // SEL_BLOCK) + ec // SEL_BLOCK, 1.0, 0.0).astype(BF16)
        chosen = _dot(sel_bf, expand) > 0.5
        return chosen & (b * tq + kcol <= pos)

    def win_mask(b):
        kpos = b * tq + kcol
        return (kpos <= pos) & (kpos > pos - WINDOW)

    o_s = sweep(ksb_ref, vsb_ref, sel_mask, 0)
    o_w = sweep(kwb_ref, vwb_ref, win_mask, jnp.maximum(i - WINDOW // tq, 0))

    gates = g_ref[...]
    for g in range(grp):
        rs = slice(g * tq, (g + 1) * tq)
        o = (gates[:, 3 * g:3 * g + 1] * o_c[rs] + gates[:, 3 * g + 1:3 * g + 2] * o_s[rs]
             + gates[:, 3 * g + 2:3 * g + 3] * o_w[rs])
        o_ref[:, g * HEAD_DIM:(g + 1) * HEAD_DIM] = o.astype(o_ref.dtype)


def _nsa_prompt(q, gates, ckv, sel, win, batch, seq):
    tq = 256
    nq = seq // tq
    kvh = NSA_KV_HEADS
    gw = GROUP_C * HEAD_DIM
    n_cmp = ckv.shape[2]
    assert seq // SEL_BLOCK <= LANES and n_cmp <= LANES and WINDOW % tq == 0
    full = lambda off: pl.BlockSpec((seq, HEAD_DIM), lambda b, k, i: (b, off + k))
    return pl.pallas_call(
        functools.partial(_nsa_prompt_kernel, seq=seq, tq=tq),
        grid=(batch, kvh, nq),
        in_specs=[
            pl.BlockSpec((tq, gw), lambda b, k, i: (b * nq + i, k)),
            pl.BlockSpec((tq, LANES), lambda b, k, i: (b * nq + i, k)),
            pl.BlockSpec((None, None, n_cmp, HEAD_DIM), lambda b, k, i: (b, k, 0, 0)),
            pl.BlockSpec((None, None, n_cmp, HEAD_DIM), lambda b, k, i: (b, kvh + k, 0, 0)),
            full(0), full(kvh), full(0), full(kvh),
        ],
        out_specs=pl.BlockSpec((tq, gw), lambda b, k, i: (b * nq + i, k)),
        out_shape=jax.ShapeDtypeStruct((batch * seq, N_HEADS * HEAD_DIM), BF16),
        scratch_shapes=[pltpu.VMEM((seq, HEAD_DIM), BF16)] * 4,
        compiler_params=_cparams(3),
        name="nsa_prompt",
    )(q, gates, ckv, ckv, sel, sel, win, win)


def _nsa_sample_pre_kernel(q_ref, ckv_ref, wst_ref, wnew_ref, oc_ref, ow_ref, selm_ref, *, t_new, past_len):
    grp = GROUP_C
    kvh = NSA_KV_HEADS
    n_cmp = ckv_ref.shape[1]
    nc = n_cmp - (CMP_BLOCK // CMP_STRIDE - 1)
    wb = wst_ref.shape[0]
    n_sel = (past_len + t_new + SEL_BLOCK - 1) // SEL_BLOCK
    sel_lanes = selm_ref.shape[1]
    rows = grp * t_new
    rowi = lax.broadcasted_iota(jnp.int32, (rows, 1), 0)
    pos = past_len + rowi % t_new
    pos_t = past_len + lax.broadcasted_iota(jnp.int32, (t_new, 1), 0)
    lane_c = lax.broadcasted_iota(jnp.int32, (rows, n_cmp), 1)
    avail = (lane_c * CMP_STRIDE + (CMP_BLOCK - 1) <= pos) & (lane_c < nc)
    lane_s = lax.broadcasted_iota(jnp.int32, (t_new, sel_lanes), 1)
    smap = _sel_map(n_cmp, sel_lanes).astype(BF16)
    lane_w = lax.broadcasted_iota(jnp.int32, (rows, wb), 1)
    wpos = past_len - wb + lane_w
    mask_w = (wpos <= pos) & (wpos > pos - WINDOW)
    lane_n = lax.broadcasted_iota(jnp.int32, (rows, LANES), 1)
    npos = past_len + lane_n
    mask_n = (npos <= pos) & (npos > pos - WINDOW) & (lane_n < t_new)
    for k in range(kvh):
        q4 = jnp.concatenate(
            [q_ref[:, (k * grp + g) * HEAD_DIM:(k * grp + g + 1) * HEAD_DIM] for g in range(grp)], axis=0).astype(BF16)
        p_c = _masked_softmax_rows(_dot_t(q4, ckv_ref[k].astype(BF16)) * SCALE, avail)
        oc_ref[k * rows:(k + 1) * rows, :] = _dot(p_c.astype(BF16), ckv_ref[kvh + k].astype(BF16))
        p_sum = p_c[0:t_new]
        for g in range(1, grp):
            p_sum = p_sum + p_c[g * t_new:(g + 1) * t_new]
        imp = _importance(p_sum, smap)
        selected = _select_blocks(imp, pos_t // SEL_BLOCK, lane_s, n_sel)
        selm_ref[k * t_new:(k + 1) * t_new, :] = jnp.where(selected, 1.0, 0.0)
        kst = wst_ref[:, k * HEAD_DIM:(k + 1) * HEAD_DIM].astype(BF16)
        vst = wst_ref[:, (kvh + k) * HEAD_DIM:(kvh + k + 1) * HEAD_DIM].astype(BF16)
        kn = _pad_rows(wnew_ref[:, k * HEAD_DIM:(k + 1) * HEAD_DIM], LANES).astype(BF16)
        vn = _pad_rows(wnew_ref[:, (kvh + k) * HEAD_DIM:(kvh + k + 1) * HEAD_DIM], LANES).astype(BF16)
        s1 = jnp.where(mask_w, _dot_t(q4, kst) * SCALE, NEG)
        s2 = jnp.where(mask_n, _dot_t(q4, kn) * SCALE, NEG)
        m = jnp.maximum(jnp.max(s1, axis=1, keepdims=True), jnp.max(s2, axis=1, keepdims=True))
        e1 = jnp.where(mask_w, jnp.exp(s1 - m), 0.0)
        e2 = jnp.where(mask_n, jnp.exp(s2 - m), 0.0)
        den = jnp.maximum(jnp.sum(e1, axis=1, keepdims=True) + jnp.sum(e2, axis=1, keepdims=True), 1e-30)
        ow_ref[k * rows:(k + 1) * rows, :] = (_dot(e1.astype(BF16), vst) + _dot(e2.astype(BF16), vn)) / den


def _nsa_sample_pre(q_s, ckv, win_state, layer, win_new, past_len):
    n_seq, t_new, _ = q_s.shape
    n_cmp = ckv.shape[2]
    wb = win_state.shape[2]
    c4 = 2 * NSA_KV_HEADS * HEAD_DIM
    rows = N_HEADS * t_new
    n_sel = (past_len + t_new + SEL_BLOCK - 1) // SEL_BLOCK
    sel_lanes = -(-n_sel // LANES) * LANES
    return pl.pallas_call(
        functools.partial(_nsa_sample_pre_kernel, t_new=t_new, past_len=past_len),
        grid=(n_seq,),
        in_specs=[
            pl.BlockSpec((None, t_new, N_HEADS * HEAD_DIM), lambda s: (s, 0, 0)),
            pl.BlockSpec((None, 2 * NSA_KV_HEADS, n_cmp, HEAD_DIM), lambda s: (s, 0, 0, 0)),
            pl.BlockSpec((None, None, wb, c4), lambda s: (layer, s, 0, 0)),
            pl.BlockSpec((None, t_new, c4), lambda s: (s, 0, 0)),
        ],
        out_specs=[
            pl.BlockSpec((None, rows, HEAD_DIM), lambda s: (s, 0, 0)),
            pl.BlockSpec((None, rows, HEAD_DIM), lambda s: (s, 0, 0)),
            pl.BlockSpec((None, NSA_KV_HEADS * t_new, sel_lanes), lambda s: (s, 0, 0)),
        ],
        out_shape=[
            jax.ShapeDtypeStruct((n_seq, rows, HEAD_DIM), F32),
            jax.ShapeDtypeStruct((n_seq, rows, HEAD_DIM), F32),
            jax.ShapeDtypeStruct((n_seq, NSA_KV_HEADS * t_new, sel_lanes), F32),
        ],
        compiler_params=_cparams(1),
        name="nsa_sample_pre",
    )(q_s, ckv, win_state, win_new)


def _nsa_sample_sel_kernel(pt_ref, q_ref, g_ref, snew_ref, selm_ref, oc_ref, ow_ref, page_ref, o_ref,
                           qb_ref, selrows_ref, m_ref, l_ref, acc_ref, *, n_pages, t_new, past_len):
    del pt_ref
    p = pl.program_id(1)
    grp = GROUP_C
    kvh = NSA_KV_HEADS
    kw = kvh * HEAD_DIM
    rows = N_HEADS * t_new
    sel_lanes = selm_ref.shape[1]
    blocks_per_page = PAGE_SIZE // SEL_BLOCK

    @pl.when(p == 0)
    def _():
        qb_ref[...] = _head_rows(q_ref[...], kvh, grp).astype(BF16)
        selrows_ref[...] = jnp.concatenate(
            [selm_ref[k * t_new:(k + 1) * t_new, :] for k in range(kvh) for _ in range(grp)], axis=0).astype(BF16)
        m_ref[...] = jnp.full(m_ref.shape, NEG, F32)
        l_ref[...] = jnp.zeros(l_ref.shape, F32)
        acc_ref[...] = jnp.zeros(acc_ref.shape, F32)

    ej = lax.broadcasted_iota(jnp.int32, (sel_lanes, LANES), 0)
    ec = lax.broadcasted_iota(jnp.int32, (sel_lanes, LANES), 1)

    def update(first_block, kblk, vblk, extra_mask):
        expand = jnp.where(ej == first_block + ec // SEL_BLOCK, 1.0, 0.0).astype(BF16)
        mask = _dot(selrows_ref[...], expand) > 0.5
        if extra_mask is not None:
            mask = mask & extra_mask
        s = jnp.where(mask, _dot_t(qb_ref[...], kblk) * SCALE, NEG)
        m_old = m_ref[...]
        m_new = jnp.maximum(m_old, jnp.max(s, axis=1, keepdims=True))
        e = jnp.where(mask, jnp.exp(s - m_new), 0.0)
        alpha = jnp.exp(m_old - m_new)
        l_ref[...] = alpha * l_ref[...] + jnp.sum(e, axis=1, keepdims=True)
        acc_ref[...] = alpha * acc_ref[...] + _diag_heads(_dot(e.astype(BF16), vblk), kvh, grp, t_new)
        m_ref[...] = m_new

    update(p * blocks_per_page, page_ref[:, :kw].astype(BF16), page_ref[:, kw:].astype(BF16), None)

    @pl.when(p == n_pages - 1)
    def _():
        lane = lax.broadcasted_iota(jnp.int32, (rows, LANES), 1)
        rowi = lax.broadcasted_iota(jnp.int32, (rows, LANES), 0)
        causal = lane <= rowi % t_new
        kn = _pad_rows(snew_ref[:, :kw], LANES).astype(BF16)
        vn = _pad_rows(snew_ref[:, kw:], LANES).astype(BF16)
        update(past_len // SEL_BLOCK, kn, vn, causal)
        o_s = acc_ref[...] / jnp.maximum(l_ref[...], 1e-30)
        gates = g_ref[...]
        cols = []
        for br in range(3):
            cols.append(jnp.concatenate(
                [gates[:, (h // grp) * LANES + 3 * (h % grp) + br:(h // grp) * LANES + 3 * (h % grp) + br + 1]
                 for h in range(N_HEADS)], axis=0))
        o = cols[0] * oc_ref[...] + cols[1] * o_s + cols[2] * ow_ref[...]
        o_ref[...] = _rows_to_tokens(o, N_HEADS, t_new)


def _nsa_sample_sel(q_s, g_s, sel_new, selm, o_c, o_w, cache, layer, page_table, past_len):
    n_seq, t_new, _ = q_s.shape
    n_pages = page_table.shape[1]
    c4 = 2 * NSA_KV_HEADS * HEAD_DIM
    rows = N_HEADS * t_new
    sel_lanes = selm.shape[2]
    assert past_len % SEL_BLOCK == 0 and t_new <= SEL_BLOCK and past_len == n_pages * PAGE_SIZE
    per_seq = lambda shape: pl.BlockSpec((None,) + shape, lambda s, p, pt: (s, 0, 0))
    grid_spec = pltpu.PrefetchScalarGridSpec(
        num_scalar_prefetch=1,
        grid=(n_seq, n_pages),
        in_specs=[
            per_seq((t_new, N_HEADS * HEAD_DIM)),
            per_seq((t_new, NSA_KV_HEADS * LANES)),
            per_seq((t_new, c4)),
            per_seq((NSA_KV_HEADS * t_new, sel_lanes)),
            per_seq((rows, HEAD_DIM)),
            per_seq((rows, HEAD_DIM)),
            pl.BlockSpec((None, None, PAGE_SIZE, c4), lambda s, p, pt: (layer, pt[s, p], 0, 0)),
        ],
        out_specs=per_seq((t_new, N_HEADS * HEAD_DIM)),
        scratch_shapes=[
            pltpu.VMEM((rows, c4 // 2), BF16),
            pltpu.VMEM((rows, sel_lanes), BF16),
            pltpu.VMEM((rows, 1), F32),
            pltpu.VMEM((rows, 1), F32),
            pltpu.VMEM((rows, HEAD_DIM), F32),
        ],
    )
    return pl.pallas_call(
        functools.partial(_nsa_sample_sel_kernel, n_pages=n_pages, t_new=t_new, past_len=past_len),
        grid_spec=grid_spec,
        out_shape=jax.ShapeDtypeStruct((n_seq, t_new, N_HEADS * HEAD_DIM), F32),
        compiler_params=_cparams(2),
        name="nsa_sample_sel",
    )(page_table, q_s, g_s, sel_new, selm, o_c, o_w, cache)


def _rope_tables(seq, t_new, past_len, sample_rows):
    half = HEAD_DIM // 2
    freq = ROPE_THETA ** (-jnp.arange(half, dtype=F32) / half)

    def tab(pos):
        ang = pos.astype(F32)[:, None] * freq[None, :]
        c, s = jnp.cos(ang), jnp.sin(ang)
        return jnp.concatenate([c, c], axis=1), jnp.concatenate([-s, s], axis=1)

    cp, sp = tab(jnp.arange(seq, dtype=jnp.int32))
    cs, ss = tab(past_len + jnp.arange(t_new, dtype=jnp.int32))
    reps = sample_rows // t_new
    return (jnp.concatenate([cp, jnp.tile(cs, (reps, 1))], axis=0),
            jnp.concatenate([sp, jnp.tile(ss, (reps, 1))], axis=0))


def kernel(x_prompt, x_sample, p_prompt, p_sample, cache_moba_kv, cache_nsa_cmp_kv, cache_nsa_sel_kv, state_nsa_win_kv, state_lru_conv, state_lru_h, page_table, norm_mix, norm_ffn, norm_ple, norm_out, moba_w_qkv, moba_w_o, lru_w_in, lru_conv_w, lru_conv_b, lru_w_a, lru_b_a, lru_w_x, lru_b_x, lru_lambda, lru_w_o, nsa_w_in, nsa_w_o, nsa_cmp_pos_k, nsa_cmp_pos_v, nsa_cmp_k_w1, nsa_cmp_k_w2, nsa_cmp_v_w1, nsa_cmp_v_w2, ffn_w_in, ffn_w_out, ple_w_gate, ple_w_proj):
    batch, seq, d = x_prompt.shape
    n_seq, t_new, _ = x_sample.shape
    depth = norm_mix.shape[0]
    n_pages = page_table.shape[1]
    past_len = n_pages * PAGE_SIZE
    mp = batch * seq
    ms = n_seq * t_new
    assert d == D_MODEL and ms == ROW_TILE and seq % ROW_TILE == 0
    n_prompt_tiles = mp // ROW_TILE
    tiles_per_seq = seq // ROW_TILE
    hq = N_HEADS * HEAD_DIM

    h = jnp.concatenate([x_prompt.reshape(mp, d), x_sample.reshape(ms, d)], axis=0)
    cos_t, sin_t = _rope_tables(seq, t_new, past_len, ms)
    proj = functools.partial(_norm_proj, cos_t=cos_t, sin_t=sin_t, n_prompt_tiles=n_prompt_tiles,
                             tiles_per_seq=tiles_per_seq)

    moba_cache = cache_moba_kv.reshape(cache_moba_kv.shape[:3] + (-1,))
    cmp_cache = cache_nsa_cmp_kv.reshape(cache_nsa_cmp_kv.shape[:3] + (-1,))
    sel_cache = cache_nsa_sel_kv.reshape(cache_nsa_sel_kv.shape[:3] + (-1,))
    win_state = state_nsa_win_kv.reshape(state_nsa_win_kv.shape[:3] + (-1,))

    outs = {k: [] for k in ("moba_p", "moba_s", "cmp_p", "cmp_s", "sel_p", "sel_s", "win_p", "win_s",
                            "conv_p", "conv_s", "hh_p", "hh_s")}
    y = None
    for i in range(depth):
        kind, j = i % N_MIXERS, i // N_MIXERS
        if kind == 0:
            hk = MOBA_KV_HEADS * HEAD_DIM
            segs = [(0, c, c, 512, "rope") for c in range(0, hq, 512)]
            segs += [(1, 0, hq, hk, "rope"), (1, hk, hq + hk, hk, "none")]
            q, kv = proj(h, norm_mix[i], moba_w_qkv[j].astype(BF16), segs=segs, outs=[(hq, BF16), (2 * hk, F32)])
            a_p = _moba_prompt(q, kv, batch, seq)
            a_s = _moba_sample(q[mp:].astype(F32).reshape(n_seq, t_new, hq), kv[mp:].reshape(n_seq, t_new, 2 * hk),
                               moba_cache, j, page_table, past_len)
            w_o = moba_w_o[j]
            outs["moba_p"].append(kv[:mp].reshape(batch, seq, 2, MOBA_KV_HEADS, HEAD_DIM))
            outs["moba_s"].append(kv[mp:].reshape(n_seq, t_new, 2, MOBA_KV_HEADS, HEAD_DIM))
        elif kind == 1:
            width = lru_w_in.shape[2] // 2
            segs = [(0, c, c, 512, "gelu") for c in range(0, width, 512)]
            segs += [(1, c, width + c, 512, "none") for c in range(0, width, 512)]
            gact, u = proj(h, norm_mix[i], lru_w_in[j].astype(BF16), segs=segs, outs=[(width, F32), (width, F32)])
            lru_w = (lru_conv_w[j], lru_conv_b[j], lru_w_a[j].astype(BF16), lru_b_a[j], lru_w_x[j].astype(BF16),
                     lru_b_x[j], lru_lambda[j])
            zeros_c = jnp.zeros((batch, SUBLANES, width), F32)
            zeros_h = jnp.zeros((batch, 1, width), F32)
            a_p, hh_p = _lru(u, gact, 0, batch, seq, ROW_TILE, zeros_c, zeros_h, *lru_w, out_dtype=BF16)
            prev_s = jnp.concatenate([jnp.zeros((n_seq, SUBLANES - (CONV_WIDTH - 1), width), F32),
                                      state_lru_conv[j]], axis=1)
            a_s, hh_s = _lru(u, gact, mp, n_seq, t_new, t_new, prev_s, state_lru_h[j][:, None, :], *lru_w,
                             out_dtype=F32)
            a_s = a_s.reshape(n_seq, t_new, width)
            w_o = lru_w_o[j]
            u_p = u[:mp].reshape(batch, seq, width)
            u_s = jnp.concatenate([state_lru_conv[j], u[mp:].reshape(n_seq, t_new, width)], axis=1)
            outs["conv_p"].append(u_p[:, seq - (CONV_WIDTH - 1):])
            outs["conv_s"].append(u_s[:, t_new:])
            outs["hh_p"].append(hh_p.reshape(batch, width))
            outs["hh_s"].append(hh_s.reshape(n_seq, width))
        else:
            hk = NSA_KV_HEADS * HEAD_DIM
            w_in = nsa_w_in[j]
            wg = w_in[:, hq + 6 * hk:].reshape(d, NSA_KV_HEADS, GROUP_C * 3)
            wg = jnp.pad(wg, ((0, 0), (0, 0), (0, LANES - GROUP_C * 3))).reshape(d, NSA_KV_HEADS * LANES)
            w_all = jnp.concatenate([w_in[:, :hq + 6 * hk], wg], axis=1).astype(BF16)
            segs = [(0, c, c, 512, "rope") for c in range(0, hq, 512)]
            for br in range(3):
                segs += [(1 + br, 0, hq + 2 * br * hk, hk, "rope"), (1 + br, hk, hq + (2 * br + 1) * hk, hk, "none")]
            segs += [(4, 0, hq + 6 * hk, NSA_KV_HEADS * LANES, "sigmoid")]
            q, cmp_kv, sel_kv, win_kv, gates = proj(
                h, norm_mix[i], w_all, segs=segs,
                outs=[(hq, BF16), (2 * hk, F32), (2 * hk, F32), (2 * hk, F32), (NSA_KV_HEADS * LANES, F32)])
            w1k, w1v = nsa_cmp_k_w1[j].astype(BF16), nsa_cmp_v_w1[j].astype(BF16)

            def cat(w1):
                w3 = w1.reshape(CMP_BLOCK, HEAD_DIM, w1.shape[1])
                return jnp.concatenate([w3[:CMP_STRIDE], w3[CMP_STRIDE:]], axis=2)

            pe_k = jnp.broadcast_to(nsa_cmp_pos_k[j].reshape(1, -1), (SUBLANES, CMP_BLOCK * HEAD_DIM))
            pe_v = jnp.broadcast_to(nsa_cmp_pos_v[j].reshape(1, -1), (SUBLANES, CMP_BLOCK * HEAD_DIM))
            cmp_w = (pe_k, pe_v, cat(w1k), cat(w1v), w1k, w1v, nsa_cmp_k_w2[j].astype(BF16),
                     nsa_cmp_v_w2[j].astype(BF16))
            ckv_p = _nsa_compress(cmp_kv, 0, batch, seq, *cmp_w)
            a_p = _nsa_prompt(q, gates, ckv_p, sel_kv, win_kv, batch, seq)
            past_cmp = _gather_pages(cmp_cache, j, page_table)
            ckv_s = _nsa_compress(past_cmp, 0, n_seq, past_len, *cmp_w)
            q_s = q[mp:].astype(F32).reshape(n_seq, t_new, hq)
            o_c, o_w, selm = _nsa_sample_pre(q_s, ckv_s, win_state, j, win_kv[mp:].reshape(n_seq, t_new, 2 * hk),
                                             past_len)
            a_s = _nsa_sample_sel(q_s, gates[mp:].reshape(n_seq, t_new, -1), sel_kv[mp:].reshape(n_seq, t_new, 2 * hk),
                                  selm, o_c, o_w, sel_cache, j, page_table, past_len)
            w_o = nsa_w_o[j]
            shp_p = (batch, seq, 2, NSA_KV_HEADS, HEAD_DIM)
            shp_s = (n_seq, t_new, 2, NSA_KV_HEADS, HEAD_DIM)
            outs["cmp_p"].append(cmp_kv[:mp].reshape(shp_p))
            outs["cmp_s"].append(cmp_kv[mp:].reshape(shp_s))
            outs["sel_p"].append(sel_kv[:mp].reshape(shp_p))
            outs["sel_s"].append(sel_kv[mp:].reshape(shp_s))
            wb = min(WINDOW, seq)
            outs["win_p"].append(win_kv[:mp].reshape(shp_p)[:, seq - wb:])
            outs["win_s"].append(jnp.concatenate([state_nsa_win_kv[j], win_kv[mp:].reshape(shp_s)], axis=1)[:, t_new:])
        a = jnp.concatenate([a_p, a_s.reshape(ms, -1).astype(BF16)], axis=0)
        p = jnp.concatenate([p_prompt[i].reshape(mp, -1), p_sample[i].reshape(ms, -1)], axis=0)
        res = _tail(a, h, p, w_o.astype(BF16), norm_ffn[i], ffn_w_in[i].astype(BF16), ffn_w_out[i].astype(BF16),
                    norm_ple[i], ple_w_gate[i].astype(BF16), ple_w_proj[i].astype(BF16), norm_out,
                    final=(i == depth - 1))
        h = res[0]
        if i == depth - 1:
            y = res[1]
    return (y[:mp].reshape(batch, seq, d), y[mp:].reshape(n_seq, t_new, d),
            jnp.stack(outs["moba_p"]), jnp.stack(outs["moba_s"]),
            jnp.stack(outs["cmp_p"]), jnp.stack(outs["cmp_s"]),
            jnp.stack(outs["sel_p"]), jnp.stack(outs["sel_s"]),
            jnp.stack(outs["win_p"]), jnp.stack(outs["win_s"]),
            jnp.stack(outs["conv_p"]), jnp.stack(outs["conv_s"]),
            jnp.stack(outs["hh_p"]), jnp.stack(outs["hh_s"]))
```

```python
import functools

import jax
import jax.numpy as jnp
from jax import lax
from jax.experimental import pallas as pl
from jax.experimental.pallas import tpu as pltpu

F32 = jnp.float32
BF16 = jnp.bfloat16

D_MODEL = 1024
N_HEADS = 8
HEAD_DIM = 128
ROPE_THETA = 10000.0
RMS_EPS = 1e-6
N_MIXERS = 3
PAGE_SIZE = 128
MOBA_KV_HEADS = 4
MOBA_BLOCK = 256
MOBA_TOPK = 3
LRU_BLOCK_WIDTH = 256
CONV_WIDTH = 4
LRU_C = 8.0
NSA_KV_HEADS = 2
GROUP_C = N_HEADS // NSA_KV_HEADS
CMP_BLOCK = 32
CMP_STRIDE = 16
SEL_BLOCK = 64
SEL_TOPN = 16
WINDOW = 512

LANES = 128
SUBLANES = 8
ROW_TILE = 256
VMEM_LIMIT = 56 * 1024 * 1024

NEG = -1e30
SCALE = HEAD_DIM ** -0.5


def _cparams(n_axes):
    return pltpu.CompilerParams(dimension_semantics=("arbitrary",) * n_axes, vmem_limit_bytes=VMEM_LIMIT)


def _whole():
    return pl.BlockSpec(memory_space=pltpu.VMEM)


def _gelu(x):
    return 0.5 * x * (1.0 + jnp.tanh(0.7978845608028654 * (x + 0.044715 * (x * x * x))))


def _sigmoid(x):
    return 1.0 / (1.0 + jnp.exp(-x))


def _dot(a, b):
    return jnp.dot(a, b, preferred_element_type=F32)


def _dot_t(a, b):
    return lax.dot_general(a, b, (((1,), (1,)), ((), ())), preferred_element_type=F32)


def _rms(x, g):
    ms = jnp.mean(x * x, axis=-1, keepdims=True)
    return x * lax.rsqrt(ms + RMS_EPS) * g


def _rank_before(score, lane, n):
    rank = jnp.zeros(score.shape, jnp.int32)
    for jp in range(n):
        col = score[:, jp:jp + 1]
        beats = (col > score) | ((col == score) & (jp < lane))
        rank = rank + beats.astype(jnp.int32)
    return rank


def _rank_before_t(score, n):
    row = lax.broadcasted_iota(jnp.int32, score.shape, 0)
    rank = jnp.zeros(score.shape, jnp.int32)
    for jp in range(n):
        r = score[jp:jp + 1, :]
        rank = rank + ((r > score) | ((r == score) & (jp < row))).astype(jnp.int32)
    return rank


def _proj_kernel(x_ref, g_ref, w_ref, cos_ref, sin_ref, *out_refs, segs):
    xn = _rms(x_ref[...], g_ref[...]).astype(BF16)
    for (oi, ocol, wcol, width, kind) in segs:
        z = _dot(xn, w_ref[:, wcol:wcol + width])
        if kind in ("rope", "rope_query"):
            cos = cos_ref[...]
            sin = sin_ref[...]
            if kind == "rope_query":
                cos = cos * SCALE
                sin = sin * SCALE
            parts = []
            for c in range(width // HEAD_DIM):
                zh = z[:, c * HEAD_DIM:(c + 1) * HEAD_DIM]
                parts.append(zh * cos + pltpu.roll(zh, HEAD_DIM // 2, 1) * sin)
            z = parts[0] if len(parts) == 1 else jnp.concatenate(parts, axis=1)
        elif kind == "sigmoid":
            z = _sigmoid(z)
        elif kind == "gelu":
            z = _gelu(z)
        out_refs[oi][:, ocol:ocol + width] = z.astype(out_refs[oi].dtype)


def _norm_proj(h, g, w_bf16, cos_t, sin_t, segs, outs, n_prompt_tiles, tiles_per_seq):
    m = h.shape[0]
    n = w_bf16.shape[1]
    grid = (m // ROW_TILE,)

    def tab_map(i):
        return (jnp.where(i < n_prompt_tiles, i % tiles_per_seq, tiles_per_seq), 0)

    return pl.pallas_call(
        functools.partial(_proj_kernel, segs=tuple(segs)),
        grid=grid,
        in_specs=[
            pl.BlockSpec((ROW_TILE, D_MODEL), lambda i: (i, 0)),
            pl.BlockSpec((1, D_MODEL), lambda i: (0, 0)),
            pl.BlockSpec((D_MODEL, n), lambda i: (0, 0)),
            pl.BlockSpec((ROW_TILE, HEAD_DIM), tab_map),
            pl.BlockSpec((ROW_TILE, HEAD_DIM), tab_map),
        ],
        out_specs=[pl.BlockSpec((ROW_TILE, nc), lambda i: (i, 0)) for nc, _ in outs],
        out_shape=[jax.ShapeDtypeStruct((m, nc), dt) for nc, dt in outs],
        compiler_params=_cparams(1),
        name="norm_proj",
    )(h, g.reshape(1, D_MODEL), w_bf16, cos_t, sin_t)


def _tail_kernel(a_ref, h_ref, p_ref, wo_ref, gf_ref, wi_ref, wout_ref, gp_ref, wg_ref, wp_ref, gout_ref,
                 h_out_ref, *y_out_ref, ffn, chunk):
    h1 = h_ref[...] + _dot(a_ref[...], wo_ref[...])
    xn = _rms(h1, gf_ref[...]).astype(BF16)
    acc = jnp.zeros(h1.shape, F32)
    for c in range(ffn // chunk):
        zg = _dot(xn, wi_ref[:, c * chunk:(c + 1) * chunk])
        zu = _dot(xn, wi_ref[:, ffn + c * chunk:ffn + (c + 1) * chunk])
        act = (zg * _sigmoid(zg) * zu).astype(BF16)
        acc = acc + _dot(act, wout_ref[c * chunk:(c + 1) * chunk, :])
    h2 = h1 + acc
    xn2 = _rms(h2, gp_ref[...]).astype(BF16)
    gate = _sigmoid(_dot(xn2, wg_ref[...]))
    h3 = h2 + gate * _dot(p_ref[...].astype(BF16), wp_ref[...])
    h_out_ref[...] = h3
    if y_out_ref:
        y_out_ref[0][...] = _rms(h3, gout_ref[...])


def _tail(a, h, p, w_o, g_ffn, w_in, w_out, g_ple, w_gate, w_proj, g_out, final):
    m = h.shape[0]
    ffn = w_out.shape[0]
    ple = p.shape[1]
    row = lambda i: (i, 0)
    vec = pl.BlockSpec((1, D_MODEL), lambda i: (0, 0))
    n_out = 2 if final else 1
    res = pl.pallas_call(
        functools.partial(_tail_kernel, ffn=ffn, chunk=256),
        grid=(m // ROW_TILE,),
        in_specs=[
            pl.BlockSpec((ROW_TILE, D_MODEL), row),
            pl.BlockSpec((ROW_TILE, D_MODEL), row),
            pl.BlockSpec((ROW_TILE, ple), row),
            _whole(), vec, _whole(), _whole(), vec, _whole(), _whole(), vec,
        ],
        out_specs=[pl.BlockSpec((ROW_TILE, D_MODEL), row)] * n_out,
        out_shape=[jax.ShapeDtypeStruct((m, D_MODEL), F32)] * n_out,
        compiler_params=_cparams(1),
        name="tail",
    )(a, h, p, w_o, g_ffn.reshape(1, -1), w_in, w_out, g_ple.reshape(1, -1), w_gate, w_proj, g_out.reshape(1, -1))
    return res


def _moba_prompt_kernel(q_ref, k_ref, v_ref, o_ref, kmean_ref, kb_ref, vt_ref, sel_ref, *, nb):
    i = pl.program_id(2)
    blk = MOBA_BLOCK
    grp = N_HEADS // MOBA_KV_HEADS
    nq = grp * blk
    nb_pad = kmean_ref.shape[0]

    @pl.when(i == 0)
    def _():
        k = k_ref[...]
        km = jnp.sum(k.reshape(nb, blk, HEAD_DIM), axis=1) * (1.0 / blk)
        kmean_ref[...] = jnp.concatenate([km, jnp.zeros((nb_pad - nb, HEAD_DIM), F32)], axis=0).astype(BF16)
        kb_ref[...] = k.astype(BF16)
        for b in range(nb):
            vt_ref[b] = v_ref[b * blk:(b + 1) * blk, :].T.astype(BF16)

    q = jnp.concatenate([q_ref[:, g * HEAD_DIM:(g + 1) * HEAD_DIM] for g in range(grp)], axis=0)
    blk_row = lax.broadcasted_iota(jnp.int32, (nb_pad, nq), 0)
    valid = blk_row < i
    gate = jnp.where(valid, _dot_t(kmean_ref[...], q), NEG)
    sel_ref[...] = jnp.where(valid & (_rank_before_t(gate, nb) < MOBA_TOPK), 1.0, 0.0)

    def past(b, carry):
        m, l, acc = carry
        s = _dot_t(kb_ref[pl.ds(pl.multiple_of(b * blk, blk), blk), :], q)
        on = sel_ref[pl.ds(b, 1), :] > 0.5
        m_new = jnp.where(on, jnp.maximum(m, jnp.max(s, axis=0, keepdims=True)), m)
        p = jnp.exp(s - jnp.where(on, m_new, -NEG))
        alpha = jnp.exp(m - m_new)
        l = alpha * l + jnp.sum(p, axis=0, keepdims=True)
        return m_new, l, alpha * acc + _dot(vt_ref[b], p.astype(BF16))

    init = (jnp.full((1, nq), NEG, F32), jnp.zeros((1, nq), F32), jnp.zeros((HEAD_DIM, nq), F32))
    m, l, acc = lax.fori_loop(0, i, past, init)
    key = lax.broadcasted_iota(jnp.int32, (blk, nq), 0)
    qi = lax.broadcasted_iota(jnp.int32, (blk, nq), 1) % blk
    s = jnp.where(key <= qi, _dot_t(kb_ref[pl.ds(pl.multiple_of(i * blk, blk), blk), :], q), NEG)
    m_new = jnp.maximum(m, jnp.max(s, axis=0, keepdims=True))
    p = jnp.exp(s - m_new)
    alpha = jnp.exp(m - m_new)
    l = alpha * l + jnp.sum(p, axis=0, keepdims=True)
    acc = alpha * acc + _dot(vt_ref[i], p.astype(BF16))
    o = acc / jnp.maximum(l, 1e-30)
    for g in range(grp):
        o_ref[:, g * HEAD_DIM:(g + 1) * HEAD_DIM] = o[:, g * blk:(g + 1) * blk].T.astype(o_ref.dtype)


def _moba_prompt(q, kv, batch, seq):
    nb = seq // MOBA_BLOCK
    kvh = MOBA_KV_HEADS
    assert nb <= 2 * SUBLANES
    return pl.pallas_call(
        functools.partial(_moba_prompt_kernel, nb=nb),
        grid=(batch, kvh, nb),
        in_specs=[
            pl.BlockSpec((MOBA_BLOCK, 2 * HEAD_DIM), lambda b, j, i: (b * nb + i, j)),
            pl.BlockSpec((seq, HEAD_DIM), lambda b, j, i: (b, j)),
            pl.BlockSpec((seq, HEAD_DIM), lambda b, j, i: (b, kvh + j)),
        ],
        out_specs=pl.BlockSpec((MOBA_BLOCK, 2 * HEAD_DIM), lambda b, j, i: (b * nb + i, j)),
        out_shape=jax.ShapeDtypeStruct((batch * seq, N_HEADS * HEAD_DIM), BF16),
        scratch_shapes=[
            pltpu.VMEM((2 * SUBLANES, HEAD_DIM), BF16),
            pltpu.VMEM((seq, HEAD_DIM), BF16),
            pltpu.VMEM((nb, HEAD_DIM, MOBA_BLOCK), BF16),
            pltpu.VMEM((2 * SUBLANES, (N_HEADS // kvh) * MOBA_BLOCK), F32),
        ],
        compiler_params=_cparams(3),
        name="moba_prompt",
    )(q, kv, kv)


PAGES_PER_STEP = 8


def _group_rows(q, first_head, group):
    return jnp.concatenate(
        [q[:, (first_head + g) * HEAD_DIM:(first_head + g + 1) * HEAD_DIM] for g in range(group)], axis=0)


def _pad_rows(x, rows):
    return jnp.concatenate([x, jnp.zeros((rows - x.shape[0], x.shape[1]), x.dtype)], axis=0)


def _page_specs(n, rows, layer):
    return [pl.BlockSpec((None, None, rows, HEAD_DIM),
                         functools.partial(lambda s, p, pt, g: (layer, pt[s, n * p + g], 0, 0), g=g))
            for g in range(n)]


def _moba_sample_kernel(pt_ref, q_ref, kvn_ref, *refs, n_pages, n_step, t_new):
    del pt_ref
    page_refs = refs[:n_step]
    o_ref, qf_ref, qb_ref, gx_ref, mx_ref, lx_ref, po_ref = refs[n_step:]
    p = pl.program_id(1)
    kvh = MOBA_KV_HEADS
    group = N_HEADS // kvh
    rows = group * t_new
    stride = 2 * kvh

    @pl.when(p == 0)
    def _():
        for j in range(kvh):
            qj = _group_rows(q_ref[...], j * group, group)
            qf_ref[j] = qj
            qb_ref[j] = qj.astype(BF16)
        gx_ref[...] = jnp.zeros(gx_ref.shape, F32)
        mx_ref[...] = jnp.zeros(mx_ref.shape, F32)
        lx_ref[...] = jnp.zeros(lx_ref.shape, F32)

    lane = lax.broadcasted_iota(jnp.int32, (rows, LANES), 1)
    zero = jnp.zeros((rows, PAGE_SIZE), BF16)
    for j in range(kvh):
        qf = qf_ref[j]
        gx, mx, lx = gx_ref[j], mx_ref[j], lx_ref[j]
        ks = [page_refs[g][pl.ds(j, PAGE_SIZE, stride=stride), :] for g in range(n_step)]
        vs = [page_refs[g][pl.ds(kvh + j, PAGE_SIZE, stride=stride), :] for g in range(n_step)]
        s = _dot_t(qb_ref[j], jnp.concatenate(ks, axis=0).astype(BF16))
        e_rows = []
        for g in range(n_step):
            gate = jnp.sum(qf * jnp.sum(ks[g], axis=0, keepdims=True), axis=1, keepdims=True)
            sg = s[:, g * PAGE_SIZE:(g + 1) * PAGE_SIZE]
            m = jnp.max(sg, axis=1, keepdims=True)
            e = jnp.exp(sg - m)
            e_rows.append(jnp.concatenate([e.astype(BF16) if gg == g else zero for gg in range(n_step)], axis=1))
            here = lane == p * n_step + g
            gx = jnp.where(here, gate, gx)
            mx = jnp.where(here, m, mx)
            lx = jnp.where(here, jnp.sum(e, axis=1, keepdims=True), lx)
        o_all = _dot(jnp.concatenate(e_rows, axis=0), jnp.concatenate(vs, axis=0).astype(BF16))
        for g in range(n_step):
            po_ref[p * n_step + g, j] = o_all[g * rows:(g + 1) * rows, :]
        gx_ref[j] = gx
        mx_ref[j] = mx
        lx_ref[j] = lx

    @pl.when(p == n_pages // n_step - 1)
    def _():
        n_past = n_pages // 2
        rowi = lax.broadcasted_iota(jnp.int32, (rows, LANES), 0)
        mask_n = lane <= (rowi % t_new)
        first = (lane % 2 == 0) & (lane < n_pages)
        for j in range(kvh):
            g2 = gx_ref[j]
            gate = jnp.where(first, (g2 + pltpu.roll(g2, LANES - 1, 1)) * (1.0 / MOBA_BLOCK), NEG)
            rank = jnp.zeros(gate.shape, jnp.int32)
            for bp in range(n_past):
                col = gate[:, 2 * bp:2 * bp + 1]
                rank = rank + ((col > gate) | ((col == gate) & (2 * bp < lane))).astype(jnp.int32)
            sel_first = jnp.where(first & (rank < MOBA_TOPK), 1.0, 0.0)
            sel = (sel_first + pltpu.roll(sel_first, 1, 1)) > 0.5
            kn = _pad_rows(kvn_ref[:, j * HEAD_DIM:(j + 1) * HEAD_DIM], LANES).astype(BF16)
            vn = _pad_rows(kvn_ref[:, (kvh + j) * HEAD_DIM:(kvh + j + 1) * HEAD_DIM], LANES).astype(BF16)
            sn = jnp.where(mask_n, _dot_t(qb_ref[j], kn), NEG)
            mx = mx_ref[j]
            m_tot = jnp.maximum(jnp.max(sn, axis=1, keepdims=True),
                                jnp.max(jnp.where(sel, mx, NEG), axis=1, keepdims=True))
            pn = jnp.where(mask_n, jnp.exp(sn - m_tot), 0.0)
            w = jnp.where(sel, jnp.exp(mx - m_tot), 0.0)
            l_tot = jnp.sum(pn, axis=1, keepdims=True) + jnp.sum(w * lx_ref[j], axis=1, keepdims=True)
            o = _dot(pn.astype(BF16), vn)
            for pg in range(n_pages):
                o = o + w[:, pg:pg + 1] * po_ref[pg, j]
            o = o / jnp.maximum(l_tot, 1e-30)
            for g in range(group):
                h = j * group + g
                o_ref[:, h * HEAD_DIM:(h + 1) * HEAD_DIM] = o[g * t_new:(g + 1) * t_new, :]


def _moba_sample(q_s, kv_s, cache, layer, page_table, past_len):
    n_seq, t_new, _ = q_s.shape
    n_pages = page_table.shape[1]
    kvh = MOBA_KV_HEADS
    rows = (N_HEADS // kvh) * t_new
    n_step = PAGES_PER_STEP
    assert n_pages <= LANES and n_pages % n_step == 0 and past_len == n_pages * PAGE_SIZE
    assert MOBA_BLOCK == 2 * PAGE_SIZE and t_new <= MOBA_BLOCK
    per_seq = lambda w: pl.BlockSpec((None, t_new, w), lambda s, p, pt: (s, 0, 0))
    grid_spec = pltpu.PrefetchScalarGridSpec(
        num_scalar_prefetch=1,
        grid=(n_seq, n_pages // n_step),
        in_specs=[per_seq(N_HEADS * HEAD_DIM), per_seq(2 * kvh * HEAD_DIM)]
        + _page_specs(n_step, PAGE_SIZE * 2 * kvh, layer),
        out_specs=per_seq(N_HEADS * HEAD_DIM),
        scratch_shapes=[
            pltpu.VMEM((kvh, rows, HEAD_DIM), F32),
            pltpu.VMEM((kvh, rows, HEAD_DIM), BF16),
            pltpu.VMEM((kvh, rows, LANES), F32),
            pltpu.VMEM((kvh, rows, LANES), F32),
            pltpu.VMEM((kvh, rows, LANES), F32),
            pltpu.VMEM((n_pages, kvh, rows, HEAD_DIM), F32),
        ],
    )
    return pl.pallas_call(
        functools.partial(_moba_sample_kernel, n_pages=n_pages, n_step=n_step, t_new=t_new),
        grid_spec=grid_spec,
        out_shape=jax.ShapeDtypeStruct((n_seq, t_new, N_HEADS * HEAD_DIM), F32),
        compiler_params=_cparams(2),
        name="moba_sample",
    )(page_table, q_s, kv_s, *([cache] * n_step))


def _lru_kernel(u_ref, gact_ref, prev0_ref, h0_ref, cw_ref, cb_ref, wa_ref, ba_ref, wx_ref, bx_ref, lam_ref,
                a_out_ref, hlast_ref, prev_s, h_s, a_s, b_s, y_s, *, tt):
    j = pl.program_id(1)
    width = u_ref.shape[1]

    @pl.when(j == 0)
    def _():
        if tt > SUBLANES:
            prev_s[0:tt - SUBLANES, :] = jnp.zeros((tt - SUBLANES, width), F32)
        prev_s[tt - SUBLANES:tt, :] = prev0_ref[...]
        h_s[...] = h0_ref[...]

    u = u_ref[...]
    prev = prev_s[...]
    rowi = lax.broadcasted_iota(jnp.int32, u.shape, 0)
    xc = cb_ref[...]
    for k in range(CONV_WIDTH):
        sh = CONV_WIDTH - 1 - k
        if sh == 0:
            us = u
        else:
            us = jnp.where(rowi < sh, pltpu.roll(prev, sh, 0), pltpu.roll(u, sh, 0))
        xc = xc + us * cw_ref[k:k + 1, :]
    prev_s[...] = u

    xb = xc.astype(BF16)
    nblk = width // LRU_BLOCK_WIDTH
    ra, rx = [], []
    for n in range(nblk):
        xs = xb[:, n * LRU_BLOCK_WIDTH:(n + 1) * LRU_BLOCK_WIDTH]
        ra.append(_dot(xs, wa_ref[n]))
        rx.append(_dot(xs, wx_ref[n]))
    r = _sigmoid(jnp.concatenate(ra, axis=1) + ba_ref[...])
    ig = _sigmoid(jnp.concatenate(rx, axis=1) + bx_ref[...])
    nl = -lam_ref[...]
    softplus = jnp.maximum(nl, 0.0) + jnp.log(1.0 + jnp.exp(-jnp.abs(nl)))
    log_a = -LRU_C * r * softplus
    a = jnp.exp(log_a)
    a_s[...] = a
    b_s[...] = jnp.sqrt(1.0 - a * a) * ig * xc

    def step(t, h):
        h = a_s[pl.ds(t, 1), :] * h + b_s[pl.ds(t, 1), :]
        y_s[pl.ds(t, 1), :] = h
        return h

    h = lax.fori_loop(0, tt, step, h_s[...], unroll=8)
    h_s[...] = h
    hlast_ref[...] = h
    a_out_ref[...] = (gact_ref[...] * y_s[...]).astype(a_out_ref.dtype)


def _lru(u, gact, row0, n_seq, seq_len, tt, prev0, h0, cw, cb, wa, ba, wx, bx, lam, out_dtype):
    width = u.shape[1]
    tps = seq_len // tt
    base = row0 // tt
    rowmap = lambda s, j: (base + s * tps + j, 0)
    vec = pl.BlockSpec((1, width), lambda s, j: (0, 0))
    return pl.pallas_call(
        functools.partial(_lru_kernel, tt=tt),
        grid=(n_seq, tps),
        in_specs=[
            pl.BlockSpec((tt, width), rowmap),
            pl.BlockSpec((tt, width), rowmap),
            pl.BlockSpec((None, SUBLANES, width), lambda s, j: (s, 0, 0)),
            pl.BlockSpec((None, 1, width), lambda s, j: (s, 0, 0)),
            pl.BlockSpec((CONV_WIDTH, width), lambda s, j: (0, 0)),
            vec, _whole(), vec, _whole(), vec, vec,
        ],
        out_specs=[
            pl.BlockSpec((tt, width), lambda s, j: (s * tps + j, 0)),
            pl.BlockSpec((None, 1, width), lambda s, j: (s, 0, 0)),
        ],
        out_shape=[
            jax.ShapeDtypeStruct((n_seq * seq_len, width), out_dtype),
            jax.ShapeDtypeStruct((n_seq, 1, width), F32),
        ],
        scratch_shapes=[
            pltpu.VMEM((tt, width), F32),
            pltpu.VMEM((1, width), F32),
            pltpu.VMEM((tt, width), F32),
            pltpu.VMEM((tt, width), F32),
            pltpu.VMEM((tt, width), F32),
        ],
        compiler_params=_cparams(2),
        name="rglru",
    )(u, gact, prev0, h0, cw, cb.reshape(1, -1), wa, ba.reshape(1, -1), wx, bx.reshape(1, -1), lam.reshape(1, -1))


CMP_PAGES_PER_STEP = 16


def _cmp_half_pages_kernel(pt_ref, *refs, n_step):
    del pt_ref
    page_refs, (wk_ref, wv_ref, o_ref) = refs[:n_step], refs[n_step:]
    c4 = 2 * NSA_KV_HEADS
    halves = PAGE_SIZE // CMP_STRIDE
    for c in range(c4):
        w_ref = wk_ref if c < NSA_KV_HEADS else wv_ref
        acc = jnp.zeros((n_step * halves, 2 * HEAD_DIM), F32)
        for l in range(CMP_STRIDE):
            x = jnp.concatenate([pr[pl.ds(l * c4 + c, halves, stride=CMP_STRIDE * c4), :] for pr in page_refs], axis=0)
            acc = acc + _dot(x.astype(BF16), w_ref[l])
        o_ref[:, c * 2 * HEAD_DIM:(c + 1) * 2 * HEAD_DIM] = acc


def _cmp_half_pages(cache, layer, page_table, w1k_cat, w1v_cat):
    n_seq, n_pages = page_table.shape
    n_step = CMP_PAGES_PER_STEP
    c4 = 2 * NSA_KV_HEADS
    halves = PAGE_SIZE // CMP_STRIDE
    assert n_pages % n_step == 0 and PAGE_SIZE % CMP_STRIDE == 0 and halves == SUBLANES
    steps = n_pages // n_step
    grid_spec = pltpu.PrefetchScalarGridSpec(
        num_scalar_prefetch=1,
        grid=(n_seq, steps),
        in_specs=_page_specs(n_step, PAGE_SIZE * c4, layer) + [_whole(), _whole()],
        out_specs=pl.BlockSpec((n_step * halves, c4 * 2 * HEAD_DIM), lambda s, p, pt: (s * steps + p, 0)),
    )
    return pl.pallas_call(
        functools.partial(_cmp_half_pages_kernel, n_step=n_step),
        grid_spec=grid_spec,
        out_shape=jax.ShapeDtypeStruct((n_seq * n_pages * halves, c4 * 2 * HEAD_DIM), F32),
        compiler_params=_cparams(2),
        name="nsa_cmp_half_pages",
    )(page_table, *([cache] * n_step), w1k_cat, w1v_cat)


def _cmp_half_kernel(*refs, n_half):
    x_refs, (wk_ref, wv_ref, o_ref) = refs[:-3], refs[-3:]
    for c in range(2 * NSA_KV_HEADS):
        w_ref = wk_ref if c < NSA_KV_HEADS else wv_ref
        acc = jnp.zeros((n_half, 2 * HEAD_DIM), F32)
        for l in range(CMP_STRIDE):
            x = x_refs[c][pl.ds(l, n_half, stride=CMP_STRIDE), :]
            acc = acc + _dot(x.astype(BF16), w_ref[l])
        o_ref[:, c * 2 * HEAD_DIM:(c + 1) * 2 * HEAD_DIM] = acc


def _cmp_finish_kernel(ab_ref, pek_ref, pev_ref, w1k_ref, w1v_ref, w2k_ref, w2v_ref, o_ref, *, n_half):
    nc = n_half - (CMP_BLOCK // CMP_STRIDE - 1)
    rowi = lax.broadcasted_iota(jnp.int32, (n_half, HEAD_DIM), 0)
    for c in range(2 * NSA_KV_HEADS):
        is_k = c < NSA_KV_HEADS
        pe = (pek_ref if is_k else pev_ref)[...]
        w1 = (w1k_ref if is_k else w1v_ref)[...]
        w2 = (w2k_ref if is_k else w2v_ref)[...]
        bias = _dot(pe.astype(BF16), w1)[0:1, :]
        a = ab_ref[:, c * 2 * HEAD_DIM:c * 2 * HEAD_DIM + HEAD_DIM]
        b = ab_ref[:, c * 2 * HEAD_DIM + HEAD_DIM:(c + 1) * 2 * HEAD_DIM]
        hid = a + pltpu.roll(b, n_half - 1, 0) + bias
        tok = _dot(_gelu(hid).astype(BF16), w2)
        o_ref[c] = jnp.where(rowi < nc, tok, 0.0)


def _cmp_half(x, row0, n_seq, seq_len, w1k_cat, w1v_cat):
    rows_per_step = 2048
    assert CMP_BLOCK == 2 * CMP_STRIDE and seq_len % rows_per_step == 0 and row0 % rows_per_step == 0
    n_half_step = rows_per_step // CMP_STRIDE
    n_steps = n_seq * seq_len // rows_per_step
    c4 = 2 * NSA_KV_HEADS
    return pl.pallas_call(
        functools.partial(_cmp_half_kernel, n_half=n_half_step),
        grid=(n_steps,),
        in_specs=[pl.BlockSpec((rows_per_step, HEAD_DIM), functools.partial(lambda i, c: (row0 // rows_per_step + i, c), c=c))
                  for c in range(c4)] + [_whole(), _whole()],
        out_specs=pl.BlockSpec((n_half_step, c4 * 2 * HEAD_DIM), lambda i: (i, 0)),
        out_shape=jax.ShapeDtypeStruct((n_steps * n_half_step, c4 * 2 * HEAD_DIM), F32),
        compiler_params=_cparams(1),
        name="nsa_cmp_half",
    )(*([x] * c4), w1k_cat, w1v_cat)


def _cmp_finish(ab, n_seq, seq_len, pe_k, pe_v, w1k, w1v, w2k, w2v):
    n_half = seq_len // CMP_STRIDE
    c4 = 2 * NSA_KV_HEADS
    return pl.pallas_call(
        functools.partial(_cmp_finish_kernel, n_half=n_half),
        grid=(n_seq,),
        in_specs=[pl.BlockSpec((n_half, c4 * 2 * HEAD_DIM), lambda s: (s, 0)),
                  _whole(), _whole(), _whole(), _whole(), _whole(), _whole()],
        out_specs=pl.BlockSpec((None, c4, n_half, HEAD_DIM), lambda s: (s, 0, 0, 0)),
        out_shape=jax.ShapeDtypeStruct((n_seq, c4, n_half, HEAD_DIM), F32),
        compiler_params=_cparams(1),
        name="nsa_cmp_finish",
    )(ab, pe_k, pe_v, w1k, w1v, w2k, w2v)


def _sel_map(n_rows, n_cols, transposed=False):
    r = lax.broadcasted_iota(jnp.int32, (n_rows, n_cols), 0)
    k = lax.broadcasted_iota(jnp.int32, (n_rows, n_cols), 1)
    c, j = (k, r) if transposed else (r, k)
    d = c - j * (SEL_BLOCK // CMP_STRIDE)
    w = jnp.zeros((n_rows, n_cols), F32)
    for m in range(SEL_BLOCK // CMP_STRIDE):
        for n in range(CMP_BLOCK // CMP_STRIDE):
            w = w + jnp.where(d == m - n, 1.0, 0.0)
    return w


def _split3(x):
    hi = x.astype(BF16)
    r1 = x - hi.astype(F32)
    mid = r1.astype(BF16)
    return hi, mid, (r1 - mid.astype(F32)).astype(BF16)


def _importance(p_sum, sel_map_bf16):
    hi, mid, lo = _split3(p_sum)
    return _dot(hi, sel_map_bf16) + _dot(mid, sel_map_bf16) + _dot(lo, sel_map_bf16)


def _importance_t(sel_map_t_bf16, p_sum_t):
    hi, mid, lo = _split3(p_sum_t)
    return _dot(sel_map_t_bf16, hi) + _dot(sel_map_t_bf16, mid) + _dot(sel_map_t_bf16, lo)


def _select_blocks(imp, cur, lane, n_sel):
    forced = (lane == 0) | (lane == cur) | (lane == cur - 1)
    allowed = lane <= cur
    score = jnp.where(allowed, jnp.where(forced, 1e30, imp), NEG)
    return allowed & (_rank_before(score, lane, n_sel) < SEL_TOPN)


def _masked_softmax_rows(s, mask):
    s = jnp.where(mask, s, NEG)
    m = jnp.max(s, axis=1, keepdims=True)
    e = jnp.where(mask, jnp.exp(s - m), 0.0)
    return e / jnp.maximum(jnp.sum(e, axis=1, keepdims=True), 1e-30)


def _nsa_prompt_kernel(q_ref, g_ref, ck_ref, cv_ref, ks_ref, vs_ref, kw_ref, vw_ref, o_ref,
                       ckb_ref, cvt_ref, ksb_ref, vst_ref, kwb_ref, vwt_ref, *, seq, tq):
    i = pl.program_id(2)
    grp = GROUP_C
    nq = grp * tq
    n_cmp = ck_ref.shape[0]
    nc = n_cmp - (CMP_BLOCK // CMP_STRIDE - 1)
    n_sel = seq // SEL_BLOCK

    @pl.when(i == 0)
    def _():
        ckb_ref[...] = ck_ref[...].astype(BF16)
        cvt_ref[...] = cv_ref[...].T.astype(BF16)
        ksb_ref[...] = ks_ref[...].astype(BF16)
        kwb_ref[...] = kw_ref[...].astype(BF16)
        for b in range(seq // tq):
            vst_ref[b] = vs_ref[b * tq:(b + 1) * tq, :].T.astype(BF16)
            vwt_ref[b] = vw_ref[b * tq:(b + 1) * tq, :].T.astype(BF16)

    q4 = jnp.concatenate([q_ref[:, g * HEAD_DIM:(g + 1) * HEAD_DIM] for g in range(grp)], axis=0)
    pos = i * tq + lax.broadcasted_iota(jnp.int32, (1, tq), 1)
    pos4 = jnp.concatenate([pos] * grp, axis=1)

    c_row = lax.broadcasted_iota(jnp.int32, (n_cmp, nq), 0)
    avail = (c_row * CMP_STRIDE + (CMP_BLOCK - 1) <= pos4) & (c_row < nc)
    s_c = jnp.where(avail, _dot_t(ckb_ref[...], q4), NEG)
    m_c = jnp.max(s_c, axis=0, keepdims=True)
    e_c = jnp.where(avail, jnp.exp(s_c - m_c), 0.0)
    p_c = e_c / jnp.maximum(jnp.sum(e_c, axis=0, keepdims=True), 1e-30)
    o_c = _dot(cvt_ref[...], p_c.astype(BF16))

    p_sum = p_c[:, 0:tq]
    for g in range(1, grp):
        p_sum = p_sum + p_c[:, g * tq:(g + 1) * tq]
    imp = _importance_t(_sel_map(LANES, n_cmp, transposed=True).astype(BF16), p_sum)[0:n_sel, :]
    j_row = lax.broadcasted_iota(jnp.int32, (n_sel, tq), 0)
    cur = pos // SEL_BLOCK
    forced = (j_row == 0) | (j_row == cur) | (j_row == cur - 1)
    allowed = j_row <= cur
    score = jnp.where(allowed, jnp.where(forced, 1e30, imp), NEG)
    selected = allowed & (_rank_before_t(score, n_sel) < SEL_TOPN)
    sel_bf = jnp.concatenate([jnp.where(selected, 1.0, 0.0), jnp.zeros((LANES - n_sel, tq), F32)],
                             axis=0).astype(BF16)

    key = lax.broadcasted_iota(jnp.int32, (tq, tq), 0)
    ek = lax.broadcasted_iota(jnp.int32, (tq, LANES), 0)
    ej = lax.broadcasted_iota(jnp.int32, (tq, LANES), 1)

    def sweep(kb_ref, vt_ref, bias_fn, lo):
        def body(b, carry):
            m, l, acc = carry
            s = _dot_t(kb_ref[pl.ds(pl.multiple_of(b * tq, tq), tq), :], q4)
            s = s + jnp.concatenate([bias_fn(b)] * grp, axis=1)
            m_new = jnp.maximum(m, jnp.max(s, axis=0, keepdims=True))
            p = jnp.exp(s - m_new)
            alpha = jnp.exp(m - m_new)
            l = alpha * l + jnp.sum(p, axis=0, keepdims=True)
            return m_new, l, alpha * acc + _dot(vt_ref[b], p.astype(BF16))

        init = (jnp.full((1, nq), NEG, F32), jnp.zeros((1, nq), F32), jnp.zeros((HEAD_DIM, nq), F32))
        m, l, acc = lax.fori_loop(lo, i + 1, body, init)
        return acc / jnp.maximum(l, 1e-30)

    def sel_bias(b):
        expand = jnp.where(ej == b * (tq // SEL_BLOCK) + ek // SEL_BLOCK, 1.0, 0.0).astype(BF16)
        chosen = _dot(expand, sel_bf) > 0.5
        return jnp.where(chosen & (b * tq + key <= pos), 0.0, NEG)

    def win_bias(b):
        kpos = b * tq + key
        return jnp.where((kpos <= pos) & (kpos > pos - WINDOW), 0.0, NEG)

    o_s = sweep(ksb_ref, vst_ref, sel_bias, 0)
    o_w = sweep(kwb_ref, vwt_ref, win_bias, jnp.maximum(i - WINDOW // tq, 0))

    gates = g_ref[...].T
    for g in range(grp):
        cs = slice(g * tq, (g + 1) * tq)
        o = (gates[3 * g:3 * g + 1, :] * o_c[:, cs] + gates[3 * g + 1:3 * g + 2, :] * o_s[:, cs]
             + gates[3 * g + 2:3 * g + 3, :] * o_w[:, cs])
        o_ref[:, g * HEAD_DIM:(g + 1) * HEAD_DIM] = o.T.astype(o_ref.dtype)


def _nsa_prompt(q, gates, ckv, sel, win, batch, seq):
    tq = 256
    nq = seq // tq
    kvh = NSA_KV_HEADS
    gw = GROUP_C * HEAD_DIM
    n_cmp = ckv.shape[2]
    assert seq // SEL_BLOCK <= LANES and n_cmp <= LANES and WINDOW % tq == 0
    full = lambda off: pl.BlockSpec((seq, HEAD_DIM), lambda b, k, i: (b, off + k))
    return pl.pallas_call(
        functools.partial(_nsa_prompt_kernel, seq=seq, tq=tq),
        grid=(batch, kvh, nq),
        in_specs=[
            pl.BlockSpec((tq, gw), lambda b, k, i: (b * nq + i, k)),
            pl.BlockSpec((tq, LANES), lambda b, k, i: (b * nq + i, k)),
            pl.BlockSpec((None, None, n_cmp, HEAD_DIM), lambda b, k, i: (b, k, 0, 0)),
            pl.BlockSpec((None, None, n_cmp, HEAD_DIM), lambda b, k, i: (b, kvh + k, 0, 0)),
            full(0), full(kvh), full(0), full(kvh),
        ],
        out_specs=pl.BlockSpec((tq, gw), lambda b, k, i: (b * nq + i, k)),
        out_shape=jax.ShapeDtypeStruct((batch * seq, N_HEADS * HEAD_DIM), BF16),
        scratch_shapes=[
            pltpu.VMEM((n_cmp, HEAD_DIM), BF16),
            pltpu.VMEM((HEAD_DIM, n_cmp), BF16),
            pltpu.VMEM((seq, HEAD_DIM), BF16),
            pltpu.VMEM((nq, HEAD_DIM, tq), BF16),
            pltpu.VMEM((seq, HEAD_DIM), BF16),
            pltpu.VMEM((nq, HEAD_DIM, tq), BF16),
        ],
        compiler_params=_cparams(3),
        name="nsa_prompt",
    )(q, gates, ckv, ckv, sel, sel, win, win)


def _nsa_sample_pre_kernel(q_ref, ckv_ref, wst_ref, wnew_ref, oc_ref, ow_ref, selm_ref, *, t_new, past_len):
    grp = GROUP_C
    kvh = NSA_KV_HEADS
    n_cmp = ckv_ref.shape[1]
    nc = n_cmp - (CMP_BLOCK // CMP_STRIDE - 1)
    wb = wst_ref.shape[0] // (2 * kvh)
    n_sel = (past_len + t_new + SEL_BLOCK - 1) // SEL_BLOCK
    sel_lanes = selm_ref.shape[1]
    rows = grp * t_new
    rowi = lax.broadcasted_iota(jnp.int32, (rows, 1), 0)
    pos = past_len + rowi % t_new
    pos_t = past_len + lax.broadcasted_iota(jnp.int32, (t_new, 1), 0)
    lane_c = lax.broadcasted_iota(jnp.int32, (rows, n_cmp), 1)
    avail = (lane_c * CMP_STRIDE + (CMP_BLOCK - 1) <= pos) & (lane_c < nc)
    lane_s = lax.broadcasted_iota(jnp.int32, (t_new, sel_lanes), 1)
    smap = _sel_map(n_cmp, sel_lanes).astype(BF16)
    lane_w = lax.broadcasted_iota(jnp.int32, (rows, wb), 1)
    wpos = past_len - wb + lane_w
    mask_w = (wpos <= pos) & (wpos > pos - WINDOW)
    lane_n = lax.broadcasted_iota(jnp.int32, (rows, LANES), 1)
    npos = past_len + lane_n
    mask_n = (npos <= pos) & (npos > pos - WINDOW) & (lane_n < t_new)
    for k in range(kvh):
        q4 = jnp.concatenate(
            [q_ref[:, (k * grp + g) * HEAD_DIM:(k * grp + g + 1) * HEAD_DIM] for g in range(grp)], axis=0).astype(BF16)
        p_c = _masked_softmax_rows(_dot_t(q4, ckv_ref[k].astype(BF16)), avail)
        oc_ref[k * rows:(k + 1) * rows, :] = _dot(p_c.astype(BF16), ckv_ref[kvh + k].astype(BF16))
        p_sum = p_c[0:t_new]
        for g in range(1, grp):
            p_sum = p_sum + p_c[g * t_new:(g + 1) * t_new]
        imp = _importance(p_sum, smap)
        selected = _select_blocks(imp, pos_t // SEL_BLOCK, lane_s, n_sel)
        selm_ref[k * t_new:(k + 1) * t_new, :] = jnp.where(selected, 1.0, 0.0)
        kst = wst_ref[pl.ds(k, wb, stride=2 * kvh), :].astype(BF16)
        vst = wst_ref[pl.ds(kvh + k, wb, stride=2 * kvh), :].astype(BF16)
        kn = _pad_rows(wnew_ref[:, k * HEAD_DIM:(k + 1) * HEAD_DIM], LANES).astype(BF16)
        vn = _pad_rows(wnew_ref[:, (kvh + k) * HEAD_DIM:(kvh + k + 1) * HEAD_DIM], LANES).astype(BF16)
        s1 = jnp.where(mask_w, _dot_t(q4, kst), NEG)
        s2 = jnp.where(mask_n, _dot_t(q4, kn), NEG)
        m = jnp.maximum(jnp.max(s1, axis=1, keepdims=True), jnp.max(s2, axis=1, keepdims=True))
        e1 = jnp.where(mask_w, jnp.exp(s1 - m), 0.0)
        e2 = jnp.where(mask_n, jnp.exp(s2 - m), 0.0)
        den = jnp.maximum(jnp.sum(e1, axis=1, keepdims=True) + jnp.sum(e2, axis=1, keepdims=True), 1e-30)
        ow_ref[k * rows:(k + 1) * rows, :] = (_dot(e1.astype(BF16), vst) + _dot(e2.astype(BF16), vn)) / den


def _nsa_sample_pre(q_s, ckv, win_state, layer, win_new, past_len):
    n_seq, t_new, _ = q_s.shape
    n_cmp = ckv.shape[2]
    wrows = win_state.shape[2]
    c4 = 2 * NSA_KV_HEADS * HEAD_DIM
    rows = N_HEADS * t_new
    n_sel = (past_len + t_new + SEL_BLOCK - 1) // SEL_BLOCK
    sel_lanes = -(-n_sel // LANES) * LANES
    return pl.pallas_call(
        functools.partial(_nsa_sample_pre_kernel, t_new=t_new, past_len=past_len),
        grid=(n_seq,),
        in_specs=[
            pl.BlockSpec((None, t_new, N_HEADS * HEAD_DIM), lambda s: (s, 0, 0)),
            pl.BlockSpec((None, 2 * NSA_KV_HEADS, n_cmp, HEAD_DIM), lambda s: (s, 0, 0, 0)),
            pl.BlockSpec((None, None, wrows, HEAD_DIM), lambda s: (layer, s, 0, 0)),
            pl.BlockSpec((None, t_new, c4), lambda s: (s, 0, 0)),
        ],
        out_specs=[
            pl.BlockSpec((None, rows, HEAD_DIM), lambda s: (s, 0, 0)),
            pl.BlockSpec((None, rows, HEAD_DIM), lambda s: (s, 0, 0)),
            pl.BlockSpec((None, NSA_KV_HEADS * t_new, sel_lanes), lambda s: (s, 0, 0)),
        ],
        out_shape=[
            jax.ShapeDtypeStruct((n_seq, rows, HEAD_DIM), F32),
            jax.ShapeDtypeStruct((n_seq, rows, HEAD_DIM), F32),
            jax.ShapeDtypeStruct((n_seq, NSA_KV_HEADS * t_new, sel_lanes), F32),
        ],
        compiler_params=_cparams(1),
        name="nsa_sample_pre",
    )(q_s, ckv, win_state, win_new)


def _nsa_sample_sel_kernel(pt_ref, q_ref, g_ref, snew_ref, selm_ref, oc_ref, ow_ref, *refs,
                           n_pages, n_step, t_new, past_len):
    del pt_ref
    page_refs = refs[:n_step]
    o_ref, qb_ref, selrows_ref, mx_ref, lx_ref, po_ref = refs[n_step:]
    p = pl.program_id(1)
    grp = GROUP_C
    kvh = NSA_KV_HEADS
    rows = grp * t_new
    sel_lanes = selm_ref.shape[1]
    stride = 2 * kvh
    assert PAGE_SIZE == 2 * SEL_BLOCK

    @pl.when(p == 0)
    def _():
        for k in range(kvh):
            qb_ref[k] = _group_rows(q_ref[...], k * grp, grp).astype(BF16)
            selrows_ref[k] = jnp.concatenate([selm_ref[k * t_new:(k + 1) * t_new, :]] * grp, axis=0)
        mx_ref[...] = jnp.zeros(mx_ref.shape, F32)
        lx_ref[...] = jnp.zeros(lx_ref.shape, F32)

    lane = lax.broadcasted_iota(jnp.int32, (rows, LANES), 1)
    lane_s = lax.broadcasted_iota(jnp.int32, (rows, sel_lanes), 1)

    def sel_col(selrows, block):
        return jnp.sum(jnp.where(lane_s == block, selrows, 0.0), axis=1, keepdims=True)

    n_keys = n_step * PAGE_SIZE
    ej = lax.broadcasted_iota(jnp.int32, (sel_lanes, n_keys), 0)
    ec = lax.broadcasted_iota(jnp.int32, (sel_lanes, n_keys), 1)
    expand = jnp.where(ej == p * (n_keys // SEL_BLOCK) + ec // SEL_BLOCK, 1.0, 0.0).astype(BF16)
    here = lane == p
    for k in range(kvh):
        k_all = jnp.concatenate([pr[pl.ds(k, PAGE_SIZE, stride=stride), :] for pr in page_refs], axis=0)
        v_all = jnp.concatenate([pr[pl.ds(kvh + k, PAGE_SIZE, stride=stride), :] for pr in page_refs], axis=0)
        mask = _dot(selrows_ref[k].astype(BF16), expand) > 0.5
        s = jnp.where(mask, _dot_t(qb_ref[k], k_all.astype(BF16)), NEG)
        m = jnp.max(s, axis=1, keepdims=True)
        e = jnp.where(mask, jnp.exp(s - m), 0.0)
        po_ref[p, k] = _dot(e.astype(BF16), v_all.astype(BF16))
        mx_ref[k] = jnp.where(here, m, mx_ref[k])
        lx_ref[k] = jnp.where(here, jnp.sum(e, axis=1, keepdims=True), lx_ref[k])

    @pl.when(p == n_pages // n_step - 1)
    def _():
        rowi = lax.broadcasted_iota(jnp.int32, (rows, LANES), 0)
        causal = lane <= rowi % t_new
        is_page = lane < n_pages // n_step
        gates = g_ref[...]
        for k in range(kvh):
            selrows = selrows_ref[k]
            mask_n = causal & (sel_col(selrows, past_len // SEL_BLOCK) > 0.5)
            kn = _pad_rows(snew_ref[:, k * HEAD_DIM:(k + 1) * HEAD_DIM], LANES).astype(BF16)
            vn = _pad_rows(snew_ref[:, (kvh + k) * HEAD_DIM:(kvh + k + 1) * HEAD_DIM], LANES).astype(BF16)
            sn = jnp.where(mask_n, _dot_t(qb_ref[k], kn), NEG)
            mx = mx_ref[k]
            m_tot = jnp.maximum(jnp.max(sn, axis=1, keepdims=True),
                                jnp.max(jnp.where(is_page, mx, NEG), axis=1, keepdims=True))
            pn = jnp.where(mask_n, jnp.exp(sn - m_tot), 0.0)
            w = jnp.where(is_page, jnp.exp(mx - m_tot), 0.0)
            l_tot = jnp.sum(pn, axis=1, keepdims=True) + jnp.sum(w * lx_ref[k], axis=1, keepdims=True)
            o_s = _dot(pn.astype(BF16), vn)
            for st in range(n_pages // n_step):
                o_s = o_s + w[:, st:st + 1] * po_ref[st, k]
            o_s = o_s / jnp.maximum(l_tot, 1e-30)
            cols = [jnp.concatenate([gates[:, k * LANES + 3 * g + br:k * LANES + 3 * g + br + 1] for g in range(grp)],
                                    axis=0) for br in range(3)]
            rs = slice(k * rows, (k + 1) * rows)
            o = cols[0] * oc_ref[rs, :] + cols[1] * o_s + cols[2] * ow_ref[rs, :]
            for g in range(grp):
                h = k * grp + g
                o_ref[:, h * HEAD_DIM:(h + 1) * HEAD_DIM] = o[g * t_new:(g + 1) * t_new, :]


def _nsa_sample_sel(q_s, g_s, sel_new, selm, o_c, o_w, cache, layer, page_table, past_len):
    n_seq, t_new, _ = q_s.shape
    n_pages = page_table.shape[1]
    kvh = NSA_KV_HEADS
    c4 = 2 * kvh * HEAD_DIM
    rows = GROUP_C * t_new
    sel_lanes = selm.shape[2]
    n_step = PAGES_PER_STEP
    assert past_len % SEL_BLOCK == 0 and t_new <= SEL_BLOCK and past_len == n_pages * PAGE_SIZE
    assert n_pages <= LANES and n_pages % n_step == 0
    per_seq = lambda shape: pl.BlockSpec((None,) + shape, lambda s, p, pt: (s, 0, 0))
    grid_spec = pltpu.PrefetchScalarGridSpec(
        num_scalar_prefetch=1,
        grid=(n_seq, n_pages // n_step),
        in_specs=[
            per_seq((t_new, N_HEADS * HEAD_DIM)),
            per_seq((t_new, kvh * LANES)),
            per_seq((t_new, c4)),
            per_seq((kvh * t_new, sel_lanes)),
            per_seq((kvh * rows, HEAD_DIM)),
            per_seq((kvh * rows, HEAD_DIM)),
        ] + _page_specs(n_step, PAGE_SIZE * 2 * kvh, layer),
        out_specs=per_seq((t_new, N_HEADS * HEAD_DIM)),
        scratch_shapes=[
            pltpu.VMEM((kvh, rows, HEAD_DIM), BF16),
            pltpu.VMEM((kvh, rows, sel_lanes), F32),
            pltpu.VMEM((kvh, rows, LANES), F32),
            pltpu.VMEM((kvh, rows, LANES), F32),
            pltpu.VMEM((n_pages // n_step, kvh, rows, HEAD_DIM), F32),
        ],
    )
    return pl.pallas_call(
        functools.partial(_nsa_sample_sel_kernel, n_pages=n_pages, n_step=n_step, t_new=t_new, past_len=past_len),
        grid_spec=grid_spec,
        out_shape=jax.ShapeDtypeStruct((n_seq, t_new, N_HEADS * HEAD_DIM), F32),
        compiler_params=_cparams(2),
        name="nsa_sample_sel",
    )(page_table, q_s, g_s, sel_new, selm, o_c, o_w, *([cache] * n_step))


def _rope_tables(seq, t_new, past_len, sample_rows):
    half = HEAD_DIM // 2
    freq = ROPE_THETA ** (-jnp.arange(half, dtype=F32) / half)

    def tab(pos):
        ang = pos.astype(F32)[:, None] * freq[None, :]
        c, s = jnp.cos(ang), jnp.sin(ang)
        return jnp.concatenate([c, c], axis=1), jnp.concatenate([-s, s], axis=1)

    cp, sp = tab(jnp.arange(seq, dtype=jnp.int32))
    cs, ss = tab(past_len + jnp.arange(t_new, dtype=jnp.int32))
    reps = sample_rows // t_new
    return (jnp.concatenate([cp, jnp.tile(cs, (reps, 1))], axis=0),
            jnp.concatenate([sp, jnp.tile(ss, (reps, 1))], axis=0))


def kernel(x_prompt, x_sample, p_prompt, p_sample, cache_moba_kv, cache_nsa_cmp_kv, cache_nsa_sel_kv, state_nsa_win_kv, state_lru_conv, state_lru_h, page_table, norm_mix, norm_ffn, norm_ple, norm_out, moba_w_qkv, moba_w_o, lru_w_in, lru_conv_w, lru_conv_b, lru_w_a, lru_b_a, lru_w_x, lru_b_x, lru_lambda, lru_w_o, nsa_w_in, nsa_w_o, nsa_cmp_pos_k, nsa_cmp_pos_v, nsa_cmp_k_w1, nsa_cmp_k_w2, nsa_cmp_v_w1, nsa_cmp_v_w2, ffn_w_in, ffn_w_out, ple_w_gate, ple_w_proj):
    batch, seq, d = x_prompt.shape
    n_seq, t_new, _ = x_sample.shape
    depth = norm_mix.shape[0]
    n_pages = page_table.shape[1]
    past_len = n_pages * PAGE_SIZE
    mp = batch * seq
    ms = n_seq * t_new
    assert d == D_MODEL and ms == ROW_TILE and seq % ROW_TILE == 0
    n_prompt_tiles = mp // ROW_TILE
    tiles_per_seq = seq // ROW_TILE
    hq = N_HEADS * HEAD_DIM

    h = jnp.concatenate([x_prompt.reshape(mp, d), x_sample.reshape(ms, d)], axis=0)
    cos_t, sin_t = _rope_tables(seq, t_new, past_len, ms)
    proj = functools.partial(_norm_proj, cos_t=cos_t, sin_t=sin_t, n_prompt_tiles=n_prompt_tiles,
                             tiles_per_seq=tiles_per_seq)

    native = lambda c: c.reshape(c.shape[:2] + (-1, HEAD_DIM))
    moba_cache, cmp_cache, sel_cache, win_state = map(
        native, (cache_moba_kv, cache_nsa_cmp_kv, cache_nsa_sel_kv, state_nsa_win_kv))

    outs = {k: [] for k in ("moba_p", "moba_s", "cmp_p", "cmp_s", "sel_p", "sel_s", "win_p", "win_s",
                            "conv_p", "conv_s", "hh_p", "hh_s")}
    y = None
    for i in range(depth):
        kind, j = i % N_MIXERS, i // N_MIXERS
        if kind == 0:
            hk = MOBA_KV_HEADS * HEAD_DIM
            segs = [(0, c, c, 512, "rope_query") for c in range(0, hq, 512)]
            segs += [(1, 0, hq, hk, "rope"), (1, hk, hq + hk, hk, "none")]
            q, kv = proj(h, norm_mix[i], moba_w_qkv[j].astype(BF16), segs=segs, outs=[(hq, BF16), (2 * hk, F32)])
            a_p = _moba_prompt(q, kv, batch, seq)
            a_s = _moba_sample(q[mp:].astype(F32).reshape(n_seq, t_new, hq), kv[mp:].reshape(n_seq, t_new, 2 * hk),
                               moba_cache, j, page_table, past_len)
            w_o = moba_w_o[j]
            outs["moba_p"].append(kv[:mp].reshape(batch, seq, 2, MOBA_KV_HEADS, HEAD_DIM))
            outs["moba_s"].append(kv[mp:].reshape(n_seq, t_new, 2, MOBA_KV_HEADS, HEAD_DIM))
        elif kind == 1:
            width = lru_w_in.shape[2] // 2
            segs = [(0, c, c, 512, "gelu") for c in range(0, width, 512)]
            segs += [(1, c, width + c, 512, "none") for c in range(0, width, 512)]
            gact, u = proj(h, norm_mix[i], lru_w_in[j].astype(BF16), segs=segs, outs=[(width, F32), (width, F32)])
            lru_w = (lru_conv_w[j], lru_conv_b[j], lru_w_a[j].astype(BF16), lru_b_a[j], lru_w_x[j].astype(BF16),
                     lru_b_x[j], lru_lambda[j])
            zeros_c = jnp.zeros((batch, SUBLANES, width), F32)
            zeros_h = jnp.zeros((batch, 1, width), F32)
            a_p, hh_p = _lru(u, gact, 0, batch, seq, ROW_TILE, zeros_c, zeros_h, *lru_w, out_dtype=BF16)
            prev_s = jnp.concatenate([jnp.zeros((n_seq, SUBLANES - (CONV_WIDTH - 1), width), F32),
                                      state_lru_conv[j]], axis=1)
            a_s, hh_s = _lru(u, gact, mp, n_seq, t_new, t_new, prev_s, state_lru_h[j][:, None, :], *lru_w,
                             out_dtype=F32)
            a_s = a_s.reshape(n_seq, t_new, width)
            w_o = lru_w_o[j]
            u_p = u[:mp].reshape(batch, seq, width)
            u_s = jnp.concatenate([state_lru_conv[j], u[mp:].reshape(n_seq, t_new, width)], axis=1)
            outs["conv_p"].append(u_p[:, seq - (CONV_WIDTH - 1):])
            outs["conv_s"].append(u_s[:, t_new:])
            outs["hh_p"].append(hh_p.reshape(batch, width))
            outs["hh_s"].append(hh_s.reshape(n_seq, width))
        else:
            hk = NSA_KV_HEADS * HEAD_DIM
            w_in = nsa_w_in[j]
            wg = w_in[:, hq + 6 * hk:].reshape(d, NSA_KV_HEADS, GROUP_C * 3)
            wg = jnp.pad(wg, ((0, 0), (0, 0), (0, LANES - GROUP_C * 3))).reshape(d, NSA_KV_HEADS * LANES)
            w_all = jnp.concatenate([w_in[:, :hq + 6 * hk], wg], axis=1).astype(BF16)
            segs = [(0, c, c, 512, "rope_query") for c in range(0, hq, 512)]
            for br in range(3):
                segs += [(1 + br, 0, hq + 2 * br * hk, hk, "rope"), (1 + br, hk, hq + (2 * br + 1) * hk, hk, "none")]
            segs += [(4, 0, hq + 6 * hk, NSA_KV_HEADS * LANES, "sigmoid")]
            q, cmp_kv, sel_kv, win_kv, gates = proj(
                h, norm_mix[i], w_all, segs=segs,
                outs=[(hq, BF16), (2 * hk, F32), (2 * hk, F32), (2 * hk, F32), (NSA_KV_HEADS * LANES, F32)])
            w1k, w1v = nsa_cmp_k_w1[j].astype(BF16), nsa_cmp_v_w1[j].astype(BF16)

            def cat(w1):
                w3 = w1.reshape(CMP_BLOCK, HEAD_DIM, w1.shape[1])
                return jnp.concatenate([w3[:CMP_STRIDE], w3[CMP_STRIDE:]], axis=2)

            pe_k = jnp.broadcast_to(nsa_cmp_pos_k[j].reshape(1, -1), (SUBLANES, CMP_BLOCK * HEAD_DIM))
            pe_v = jnp.broadcast_to(nsa_cmp_pos_v[j].reshape(1, -1), (SUBLANES, CMP_BLOCK * HEAD_DIM))
            w1k_cat, w1v_cat = cat(w1k), cat(w1v)
            fin_w = (pe_k, pe_v, w1k, w1v, nsa_cmp_k_w2[j].astype(BF16), nsa_cmp_v_w2[j].astype(BF16))
            ckv_p = _cmp_finish(_cmp_half(cmp_kv, 0, batch, seq, w1k_cat, w1v_cat), batch, seq, *fin_w)
            a_p = _nsa_prompt(q, gates, ckv_p, sel_kv, win_kv, batch, seq)
            ckv_s = _cmp_finish(_cmp_half_pages(cmp_cache, j, page_table, w1k_cat, w1v_cat), n_seq, past_len, *fin_w)
            q_s = q[mp:].astype(F32).reshape(n_seq, t_new, hq)
            o_c, o_w, selm = _nsa_sample_pre(q_s, ckv_s, win_state, j, win_kv[mp:].reshape(n_seq, t_new, 2 * hk),
                                             past_len)
            a_s = _nsa_sample_sel(q_s, gates[mp:].reshape(n_seq, t_new, -1), sel_kv[mp:].reshape(n_seq, t_new, 2 * hk),
                                  selm, o_c, o_w, sel_cache, j, page_table, past_len)
            w_o = nsa_w_o[j]
            shp_p = (batch, seq, 2, NSA_KV_HEADS, HEAD_DIM)
            shp_s = (n_seq, t_new, 2, NSA_KV_HEADS, HEAD_DIM)
            outs["cmp_p"].append(cmp_kv[:mp].reshape(shp_p))
            outs["cmp_s"].append(cmp_kv[mp:].reshape(shp_s))
            outs["sel_p"].append(sel_kv[:mp].reshape(shp_p))
            outs["sel_s"].append(sel_kv[mp:].reshape(shp_s))
            wb = min(WINDOW, seq)
            outs["win_p"].append(win_kv[:mp].reshape(shp_p)[:, seq - wb:])
            outs["win_s"].append(jnp.concatenate([state_nsa_win_kv[j], win_kv[mp:].reshape(shp_s)], axis=1)[:, t_new:])
        a = jnp.concatenate([a_p, a_s.reshape(ms, -1).astype(BF16)], axis=0)
        p = jnp.concatenate([p_prompt[i].reshape(mp, -1), p_sample[i].reshape(ms, -1)], axis=0)
        res = _tail(a, h, p, w_o.astype(BF16), norm_ffn[i], ffn_w_in[i].astype(BF16), ffn_w_out[i].astype(BF16),
                    norm_ple[i], ple_w_gate[i].astype(BF16), ple_w_proj[i].astype(BF16), norm_out,
                    final=(i == depth - 1))
        h = res[0]
        if i == depth - 1:
            y = res[1]
    return (y[:mp].reshape(batch, seq, d), y[mp:].reshape(n_seq, t_new, d),
            jnp.stack(outs["moba_p"]), jnp.stack(outs["moba_s"]),
            jnp.stack(outs["cmp_p"]), jnp.stack(outs["cmp_s"]),
            jnp.stack(outs["sel_p"]), jnp.stack(outs["sel_s"]),
            jnp.stack(outs["win_p"]), jnp.stack(outs["win_s"]),
            jnp.stack(outs["conv_p"]), jnp.stack(outs["conv_s"]),
            jnp.stack(outs["hh_p"]), jnp.stack(outs["hh_s"]))
```

```python
import functools

import jax
import jax.numpy as jnp
from jax import lax
from jax.experimental import pallas as pl
from jax.experimental.pallas import tpu as pltpu

F32 = jnp.float32
BF16 = jnp.bfloat16

D_MODEL = 1024
N_HEADS = 8
HEAD_DIM = 128
ROPE_THETA = 10000.0
RMS_EPS = 1e-6
N_MIXERS = 3
PAGE_SIZE = 128
MOBA_KV_HEADS = 4
MOBA_BLOCK = 256
MOBA_TOPK = 3
LRU_BLOCK_WIDTH = 256
CONV_WIDTH = 4
LRU_C = 8.0
NSA_KV_HEADS = 2
GROUP_C = N_HEADS // NSA_KV_HEADS
CMP_BLOCK = 32
CMP_STRIDE = 16
SEL_BLOCK = 64
SEL_TOPN = 16
WINDOW = 512

LANES = 128
SUBLANES = 8
ROW_TILE = 256
VMEM_LIMIT = 56 * 1024 * 1024

NEG = -1e30
SCALE = HEAD_DIM ** -0.5


def _cparams(n_axes):
    return pltpu.CompilerParams(dimension_semantics=("arbitrary",) * n_axes, vmem_limit_bytes=VMEM_LIMIT)


def _whole():
    return pl.BlockSpec(memory_space=pltpu.VMEM)


def _gelu(x):
    return 0.5 * x * (1.0 + jnp.tanh(0.7978845608028654 * (x + 0.044715 * (x * x * x))))


def _sigmoid(x):
    return 1.0 / (1.0 + jnp.exp(-x))


def _dot(a, b):
    return jnp.dot(a, b, preferred_element_type=F32)


def _dot_t(a, b):
    return lax.dot_general(a, b, (((1,), (1,)), ((), ())), preferred_element_type=F32)


def _rms(x, g):
    ms = jnp.mean(x * x, axis=-1, keepdims=True)
    return x * lax.rsqrt(ms + RMS_EPS) * g


def _rank_before(score, lane, n):
    rank = jnp.zeros(score.shape, jnp.int32)
    for jp in range(n):
        col = score[:, jp:jp + 1]
        beats = (col > score) | ((col == score) & (jp < lane))
        rank = rank + beats.astype(jnp.int32)
    return rank


def _rank_before_t(score, n):
    row = lax.broadcasted_iota(jnp.int32, score.shape, 0)
    rank = jnp.zeros(score.shape, jnp.int32)
    for jp in range(n):
        r = score[jp:jp + 1, :]
        rank = rank + ((r > score) | ((r == score) & (jp < row))).astype(jnp.int32)
    return rank


def _part_specs(block, n_prompt_tiles, sample_block, lead=()):
    return [pl.BlockSpec((None,) * len(lead) + block, lambda i: lead + (jnp.minimum(i, n_prompt_tiles - 1), 0)),
            pl.BlockSpec((None,) * len(lead) + block, lambda i: lead + (sample_block, 0))]


def _store_part(is_prompt, ref_p, ref_s, cols, value):
    @pl.when(is_prompt)
    def _():
        ref_p[:, cols] = value.astype(ref_p.dtype)

    @pl.when(jnp.logical_not(is_prompt))
    def _():
        ref_s[:, cols] = value.astype(ref_s.dtype)


def _proj_kernel(xp_ref, xs_ref, g_ref, w_ref, cos_ref, sin_ref, *out_refs, segs, out_map, n_prompt_tiles):
    is_prompt = pl.program_id(0) < n_prompt_tiles
    xn = _rms(jnp.where(is_prompt, xp_ref[...], xs_ref[...]), g_ref[...]).astype(BF16)
    for (oi, ocol, wcol, width, kind) in segs:
        z = _dot(xn, w_ref[:, wcol:wcol + width])
        if kind in ("rope", "rope_query"):
            cos = cos_ref[...]
            sin = sin_ref[...]
            if kind == "rope_query":
                cos = cos * SCALE
                sin = sin * SCALE
            parts = []
            for c in range(width // HEAD_DIM):
                zh = z[:, c * HEAD_DIM:(c + 1) * HEAD_DIM]
                parts.append(zh * cos + pltpu.roll(zh, HEAD_DIM // 2, 1) * sin)
            z = parts[0] if len(parts) == 1 else jnp.concatenate(parts, axis=1)
        elif kind == "sigmoid":
            z = _sigmoid(z)
        elif kind == "gelu":
            z = _gelu(z)
        refs = [out_refs[r] for r in out_map[oi]]
        if len(refs) == 1:
            refs[0][:, ocol:ocol + width] = z.astype(refs[0].dtype)
        else:
            _store_part(is_prompt, refs[0], refs[1], slice(ocol, ocol + width), z)


def _norm_proj(h_parts, g, w_bf16, cos_t, sin_t, segs, outs, n_prompt_tiles, tiles_per_seq):
    hp, hs, hs_block = h_parts
    mp = n_prompt_tiles * ROW_TILE
    n = w_bf16.shape[1]
    grid = (n_prompt_tiles + 1,)

    def tab_map(i):
        return (jnp.where(i < n_prompt_tiles, i % tiles_per_seq, tiles_per_seq), 0)

    out_specs, out_shape, out_map = [], [], []
    for nc, dt, split in outs:
        if split:
            out_map.append((len(out_specs), len(out_specs) + 1))
            out_specs += _part_specs((ROW_TILE, nc), n_prompt_tiles, 0)
            out_shape += [jax.ShapeDtypeStruct((mp, nc), dt), jax.ShapeDtypeStruct((ROW_TILE, nc), dt)]
        else:
            out_map.append((len(out_specs),))
            out_specs.append(pl.BlockSpec((ROW_TILE, nc), lambda i: (i, 0)))
            out_shape.append(jax.ShapeDtypeStruct((mp + ROW_TILE, nc), dt))
    res = pl.pallas_call(
        functools.partial(_proj_kernel, segs=tuple(segs), out_map=tuple(out_map), n_prompt_tiles=n_prompt_tiles),
        grid=grid,
        in_specs=_part_specs((ROW_TILE, D_MODEL), n_prompt_tiles, hs_block) + [
            pl.BlockSpec((1, D_MODEL), lambda i: (0, 0)),
            pl.BlockSpec((D_MODEL, n), lambda i: (0, 0)),
            pl.BlockSpec((ROW_TILE, HEAD_DIM), tab_map),
            pl.BlockSpec((ROW_TILE, HEAD_DIM), tab_map),
        ],
        out_specs=out_specs,
        out_shape=out_shape,
        compiler_params=_cparams(1),
        name="norm_proj",
    )(hp, hs, g.reshape(1, D_MODEL), w_bf16, cos_t, sin_t)
    return [res[r[0]] if len(r) == 1 else (res[r[0]], res[r[1]]) for r in out_map]


def _tail_kernel(ap_ref, as_ref, hp_ref, hs_ref, pp_ref, ps_ref, wo_ref, gf_ref, wi_ref, wout_ref, gp_ref, wg_ref,
                 wp_ref, gout_ref, h_out_ref, *y_out_refs, ffn, chunk, n_prompt_tiles):
    is_prompt = pl.program_id(0) < n_prompt_tiles
    a = jnp.where(is_prompt, ap_ref[...], as_ref[...].astype(BF16))
    h = jnp.where(is_prompt, hp_ref[...], hs_ref[...])
    p = jnp.where(is_prompt, pp_ref[...], ps_ref[...])
    h1 = h + _dot(a, wo_ref[...])
    xn = _rms(h1, gf_ref[...]).astype(BF16)
    acc = jnp.zeros(h1.shape, F32)
    for c in range(ffn // chunk):
        zg = _dot(xn, wi_ref[:, c * chunk:(c + 1) * chunk])
        zu = _dot(xn, wi_ref[:, ffn + c * chunk:ffn + (c + 1) * chunk])
        act = (zg * _sigmoid(zg) * zu).astype(BF16)
        acc = acc + _dot(act, wout_ref[c * chunk:(c + 1) * chunk, :])
    h2 = h1 + acc
    xn2 = _rms(h2, gp_ref[...]).astype(BF16)
    gate = _sigmoid(_dot(xn2, wg_ref[...]))
    h3 = h2 + gate * _dot(p.astype(BF16), wp_ref[...])
    h_out_ref[...] = h3
    if y_out_refs:
        _store_part(is_prompt, y_out_refs[0], y_out_refs[1], slice(None), _rms(h3, gout_ref[...]))


def _tail(a_parts, h_parts, p_parts, layer, w_o, g_ffn, w_in, w_out, g_ple, w_gate, w_proj, g_out, n_prompt_tiles,
          final):
    mp = n_prompt_tiles * ROW_TILE
    ffn = w_out.shape[0]
    ple = p_parts[0].shape[2]
    vec = pl.BlockSpec((1, D_MODEL), lambda i: (0, 0))
    out_specs = [pl.BlockSpec((ROW_TILE, D_MODEL), lambda i: (i, 0))]
    out_shape = [jax.ShapeDtypeStruct((mp + ROW_TILE, D_MODEL), F32)]
    if final:
        out_specs += _part_specs((ROW_TILE, D_MODEL), n_prompt_tiles, 0)
        out_shape += [jax.ShapeDtypeStruct((mp, D_MODEL), F32), jax.ShapeDtypeStruct((ROW_TILE, D_MODEL), F32)]
    return pl.pallas_call(
        functools.partial(_tail_kernel, ffn=ffn, chunk=256, n_prompt_tiles=n_prompt_tiles),
        grid=(n_prompt_tiles + 1,),
        in_specs=_part_specs((ROW_TILE, D_MODEL), n_prompt_tiles, 0)
        + _part_specs((ROW_TILE, D_MODEL), n_prompt_tiles, h_parts[2])
        + _part_specs((ROW_TILE, ple), n_prompt_tiles, 0, lead=(layer,))
        + [_whole(), vec, _whole(), _whole(), vec, _whole(), _whole(), vec],
        out_specs=out_specs,
        out_shape=out_shape,
        compiler_params=_cparams(1),
        name="tail",
    )(a_parts[0], a_parts[1], h_parts[0], h_parts[1], p_parts[0], p_parts[1], w_o, g_ffn.reshape(1, -1), w_in, w_out,
      g_ple.reshape(1, -1), w_gate, w_proj, g_out.reshape(1, -1))


def _moba_prompt_kernel(q_ref, k_ref, v_ref, o_ref, kmean_ref, kb_ref, vt_ref, sel_ref, *, nb):
    i = pl.program_id(2)
    blk = MOBA_BLOCK
    grp = N_HEADS // MOBA_KV_HEADS
    nq = grp * blk
    nb_pad = kmean_ref.shape[0]

    @pl.when(i == 0)
    def _():
        k = k_ref[...]
        km = jnp.sum(k.reshape(nb, blk, HEAD_DIM), axis=1) * (1.0 / blk)
        kmean_ref[...] = jnp.concatenate([km, jnp.zeros((nb_pad - nb, HEAD_DIM), F32)], axis=0).astype(BF16)
        kb_ref[...] = k.astype(BF16)
        for c in range(nb // 2):
            vt_ref[c] = v_ref[2 * c * blk:2 * (c + 1) * blk, :].T.astype(BF16)

    q = jnp.concatenate([q_ref[:, g * HEAD_DIM:(g + 1) * HEAD_DIM] for g in range(grp)], axis=0)
    blk_row = lax.broadcasted_iota(jnp.int32, (nb_pad, nq), 0)
    valid = blk_row < i
    gate = jnp.where(valid, _dot_t(kmean_ref[...], q), NEG)
    sel_ref[...] = jnp.where(valid & (_rank_before_t(gate, nb) < MOBA_TOPK), 1.0, 0.0)

    def past(c, carry):
        m, l, acc = carry
        s = _dot_t(kb_ref[pl.ds(pl.multiple_of(c * 2 * blk, 2 * blk), 2 * blk), :], q)
        halves = []
        m_new = m
        for hh in range(2):
            on = sel_ref[pl.ds(2 * c + hh, 1), :] > 0.5
            sh = s[hh * blk:(hh + 1) * blk, :]
            m_new = jnp.maximum(m_new, jnp.where(on, jnp.max(sh, axis=0, keepdims=True), NEG))
            halves.append((on, sh))
        p = jnp.concatenate([jnp.exp(sh - jnp.where(on, m_new, -NEG)) for on, sh in halves], axis=0)
        alpha = jnp.exp(m - m_new)
        l = alpha * l + jnp.sum(p, axis=0, keepdims=True)
        return m_new, l, alpha * acc + _dot(vt_ref[c], p.astype(BF16))

    init = (jnp.full((1, nq), NEG, F32), jnp.zeros((1, nq), F32), jnp.zeros((HEAD_DIM, nq), F32))
    m, l, acc = lax.fori_loop(0, (i + 1) // 2, past, init)
    key = lax.broadcasted_iota(jnp.int32, (blk, nq), 0)
    qi = lax.broadcasted_iota(jnp.int32, (blk, nq), 1) % blk
    s = jnp.where(key <= qi, _dot_t(kb_ref[pl.ds(pl.multiple_of(i * blk, blk), blk), :], q), NEG)
    m_new = jnp.maximum(m, jnp.max(s, axis=0, keepdims=True))
    p = jnp.exp(s - m_new)
    alpha = jnp.exp(m - m_new)
    l = alpha * l + jnp.sum(p, axis=0, keepdims=True)
    vt_pair = vt_ref[i // 2]
    vt_own = jnp.where(i % 2 == 0, vt_pair[:, :blk], vt_pair[:, blk:])
    acc = alpha * acc + _dot(vt_own, p.astype(BF16))
    o = acc / jnp.maximum(l, 1e-30)
    for g in range(grp):
        o_ref[:, g * HEAD_DIM:(g + 1) * HEAD_DIM] = o[:, g * blk:(g + 1) * blk].T.astype(o_ref.dtype)


def _moba_prompt(q, kv, batch, seq):
    nb = seq // MOBA_BLOCK
    kvh = MOBA_KV_HEADS
    assert nb <= 2 * SUBLANES and nb % 2 == 0
    return pl.pallas_call(
        functools.partial(_moba_prompt_kernel, nb=nb),
        grid=(batch, kvh, nb),
        in_specs=[
            pl.BlockSpec((MOBA_BLOCK, 2 * HEAD_DIM), lambda b, j, i: (b * nb + i, j)),
            pl.BlockSpec((seq, HEAD_DIM), lambda b, j, i: (b, j)),
            pl.BlockSpec((seq, HEAD_DIM), lambda b, j, i: (b, kvh + j)),
        ],
        out_specs=pl.BlockSpec((MOBA_BLOCK, 2 * HEAD_DIM), lambda b, j, i: (b * nb + i, j)),
        out_shape=jax.ShapeDtypeStruct((batch * seq, N_HEADS * HEAD_DIM), BF16),
        scratch_shapes=[
            pltpu.VMEM((2 * SUBLANES, HEAD_DIM), BF16),
            pltpu.VMEM((seq, HEAD_DIM), BF16),
            pltpu.VMEM((nb // 2, HEAD_DIM, 2 * MOBA_BLOCK), BF16),
            pltpu.VMEM((2 * SUBLANES, (N_HEADS // kvh) * MOBA_BLOCK), F32),
        ],
        compiler_params=_cparams(3),
        name="moba_prompt",
    )(q, kv, kv)


PAGES_PER_STEP = 8


def _group_rows(q, first_head, group):
    return jnp.concatenate(
        [q[:, (first_head + g) * HEAD_DIM:(first_head + g + 1) * HEAD_DIM] for g in range(group)], axis=0)


def _pad_rows(x, rows):
    return jnp.concatenate([x, jnp.zeros((rows - x.shape[0], x.shape[1]), x.dtype)], axis=0)


def _page_specs(n, rows, layer):
    return [pl.BlockSpec((None, None, rows, HEAD_DIM),
                         functools.partial(lambda s, p, pt, g: (layer, pt[s, n * p + g], 0, 0), g=g))
            for g in range(n)]


def _moba_sample_kernel(pt_ref, q_ref, kvn_ref, *refs, n_pages, n_step, t_new):
    del pt_ref
    page_refs = refs[:n_step]
    o_ref, qf_ref, qb_ref, gx_ref, mx_ref, lx_ref, po_ref = refs[n_step:]
    p = pl.program_id(1)
    kvh = MOBA_KV_HEADS
    group = N_HEADS // kvh
    rows = group * t_new
    stride = 2 * kvh

    @pl.when(p == 0)
    def _():
        for j in range(kvh):
            qj = _group_rows(q_ref[...], j * group, group)
            qf_ref[j] = qj
            qb_ref[j] = qj.astype(BF16)
        gx_ref[...] = jnp.zeros(gx_ref.shape, F32)
        mx_ref[...] = jnp.zeros(mx_ref.shape, F32)
        lx_ref[...] = jnp.zeros(lx_ref.shape, F32)

    lane = lax.broadcasted_iota(jnp.int32, (rows, LANES), 1)
    zero = jnp.zeros((rows, PAGE_SIZE), BF16)
    for j in range(kvh):
        qf = qf_ref[j]
        gx, mx, lx = gx_ref[j], mx_ref[j], lx_ref[j]
        ks = [page_refs[g][pl.ds(j, PAGE_SIZE, stride=stride), :] for g in range(n_step)]
        vs = [page_refs[g][pl.ds(kvh + j, PAGE_SIZE, stride=stride), :] for g in range(n_step)]
        s = _dot_t(qb_ref[j], jnp.concatenate(ks, axis=0).astype(BF16))
        e_rows = []
        for g in range(n_step):
            gate = jnp.sum(qf * jnp.sum(ks[g], axis=0, keepdims=True), axis=1, keepdims=True)
            sg = s[:, g * PAGE_SIZE:(g + 1) * PAGE_SIZE]
            m = jnp.max(sg, axis=1, keepdims=True)
            e = jnp.exp(sg - m)
            e_rows.append(jnp.concatenate([e.astype(BF16) if gg == g else zero for gg in range(n_step)], axis=1))
            here = lane == p * n_step + g
            gx = jnp.where(here, gate, gx)
            mx = jnp.where(here, m, mx)
            lx = jnp.where(here, jnp.sum(e, axis=1, keepdims=True), lx)
        o_all = _dot(jnp.concatenate(e_rows, axis=0), jnp.concatenate(vs, axis=0).astype(BF16))
        for g in range(n_step):
            po_ref[p * n_step + g, j] = o_all[g * rows:(g + 1) * rows, :]
        gx_ref[j] = gx
        mx_ref[j] = mx
        lx_ref[j] = lx

    @pl.when(p == n_pages // n_step - 1)
    def _():
        n_past = n_pages // 2
        rowi = lax.broadcasted_iota(jnp.int32, (rows, LANES), 0)
        mask_n = lane <= (rowi % t_new)
        first = (lane % 2 == 0) & (lane < n_pages)
        for j in range(kvh):
            g2 = gx_ref[j]
            gate = jnp.where(first, (g2 + pltpu.roll(g2, LANES - 1, 1)) * (1.0 / MOBA_BLOCK), NEG)
            rank = jnp.zeros(gate.shape, jnp.int32)
            for bp in range(n_past):
                col = gate[:, 2 * bp:2 * bp + 1]
                rank = rank + ((col > gate) | ((col == gate) & (2 * bp < lane))).astype(jnp.int32)
            sel_first = jnp.where(first & (rank < MOBA_TOPK), 1.0, 0.0)
            sel = (sel_first + pltpu.roll(sel_first, 1, 1)) > 0.5
            kn = _pad_rows(kvn_ref[:, j * HEAD_DIM:(j + 1) * HEAD_DIM], LANES).astype(BF16)
            vn = _pad_rows(kvn_ref[:, (kvh + j) * HEAD_DIM:(kvh + j + 1) * HEAD_DIM], LANES).astype(BF16)
            sn = jnp.where(mask_n, _dot_t(qb_ref[j], kn), NEG)
            mx = mx_ref[j]
            m_tot = jnp.maximum(jnp.max(sn, axis=1, keepdims=True),
                                jnp.max(jnp.where(sel, mx, NEG), axis=1, keepdims=True))
            pn = jnp.where(mask_n, jnp.exp(sn - m_tot), 0.0)
            w = jnp.where(sel, jnp.exp(mx - m_tot), 0.0)
            l_tot = jnp.sum(pn, axis=1, keepdims=True) + jnp.sum(w * lx_ref[j], axis=1, keepdims=True)
            o = _dot(pn.astype(BF16), vn)
            for pg in range(n_pages):
                o = o + w[:, pg:pg + 1] * po_ref[pg, j]
            o = o / jnp.maximum(l_tot, 1e-30)
            for g in range(group):
                h = j * group + g
                o_ref[:, h * HEAD_DIM:(h + 1) * HEAD_DIM] = o[g * t_new:(g + 1) * t_new, :]


def _moba_sample(q_s, kv_s, cache, layer, page_table, past_len):
    n_seq, t_new, _ = q_s.shape
    n_pages = page_table.shape[1]
    kvh = MOBA_KV_HEADS
    rows = (N_HEADS // kvh) * t_new
    n_step = PAGES_PER_STEP
    assert n_pages <= LANES and n_pages % n_step == 0 and past_len == n_pages * PAGE_SIZE
    assert MOBA_BLOCK == 2 * PAGE_SIZE and t_new <= MOBA_BLOCK
    per_seq = lambda w: pl.BlockSpec((None, t_new, w), lambda s, p, pt: (s, 0, 0))
    grid_spec = pltpu.PrefetchScalarGridSpec(
        num_scalar_prefetch=1,
        grid=(n_seq, n_pages // n_step),
        in_specs=[per_seq(N_HEADS * HEAD_DIM), per_seq(2 * kvh * HEAD_DIM)]
        + _page_specs(n_step, PAGE_SIZE * 2 * kvh, layer),
        out_specs=per_seq(N_HEADS * HEAD_DIM),
        scratch_shapes=[
            pltpu.VMEM((kvh, rows, HEAD_DIM), F32),
            pltpu.VMEM((kvh, rows, HEAD_DIM), BF16),
            pltpu.VMEM((kvh, rows, LANES), F32),
            pltpu.VMEM((kvh, rows, LANES), F32),
            pltpu.VMEM((kvh, rows, LANES), F32),
            pltpu.VMEM((n_pages, kvh, rows, HEAD_DIM), F32),
        ],
    )
    return pl.pallas_call(
        functools.partial(_moba_sample_kernel, n_pages=n_pages, n_step=n_step, t_new=t_new),
        grid_spec=grid_spec,
        out_shape=jax.ShapeDtypeStruct((n_seq, t_new, N_HEADS * HEAD_DIM), F32),
        compiler_params=_cparams(2),
        name="moba_sample",
    )(page_table, q_s, kv_s, *([cache] * n_step))


def _lru_kernel(u_ref, gact_ref, prev0_ref, h0_ref, cw_ref, cb_ref, wa_ref, ba_ref, wx_ref, bx_ref, lam_ref,
                a_out_ref, hlast_ref, prev_s, h_s, a_s, b_s, y_s, *, tt):
    j = pl.program_id(1)
    width = u_ref.shape[1]

    @pl.when(j == 0)
    def _():
        if tt > SUBLANES:
            prev_s[0:tt - SUBLANES, :] = jnp.zeros((tt - SUBLANES, width), F32)
        prev_s[tt - SUBLANES:tt, :] = prev0_ref[...]
        h_s[...] = h0_ref[...]

    u = u_ref[...]
    prev = prev_s[...]
    rowi = lax.broadcasted_iota(jnp.int32, u.shape, 0)
    xc = cb_ref[...]
    for k in range(CONV_WIDTH):
        sh = CONV_WIDTH - 1 - k
        if sh == 0:
            us = u
        else:
            us = jnp.where(rowi < sh, pltpu.roll(prev, sh, 0), pltpu.roll(u, sh, 0))
        xc = xc + us * cw_ref[k:k + 1, :]
    prev_s[...] = u

    xb = xc.astype(BF16)
    nblk = width // LRU_BLOCK_WIDTH
    ra, rx = [], []
    for n in range(nblk):
        xs = xb[:, n * LRU_BLOCK_WIDTH:(n + 1) * LRU_BLOCK_WIDTH]
        ra.append(_dot(xs, wa_ref[n]))
        rx.append(_dot(xs, wx_ref[n]))
    r = _sigmoid(jnp.concatenate(ra, axis=1) + ba_ref[...])
    ig = _sigmoid(jnp.concatenate(rx, axis=1) + bx_ref[...])
    nl = -lam_ref[...]
    softplus = jnp.maximum(nl, 0.0) + jnp.log(1.0 + jnp.exp(-jnp.abs(nl)))
    log_a = -LRU_C * r * softplus
    a = jnp.exp(log_a)
    a_s[...] = a
    b_s[...] = jnp.sqrt(1.0 - a * a) * ig * xc

    def step(t, h):
        h = a_s[pl.ds(t, 1), :] * h + b_s[pl.ds(t, 1), :]
        y_s[pl.ds(t, 1), :] = h
        return h

    h = lax.fori_loop(0, tt, step, h_s[...], unroll=8)
    h_s[...] = h
    hlast_ref[...] = h
    a_out_ref[...] = (gact_ref[...] * y_s[...]).astype(a_out_ref.dtype)


def _lru(u, gact, row0, n_seq, seq_len, tt, prev0, h0, cw, cb, wa, ba, wx, bx, lam, out_dtype):
    width = u.shape[1]
    tps = seq_len // tt
    base = row0 // tt
    rowmap = lambda s, j: (base + s * tps + j, 0)
    vec = pl.BlockSpec((1, width), lambda s, j: (0, 0))
    return pl.pallas_call(
        functools.partial(_lru_kernel, tt=tt),
        grid=(n_seq, tps),
        in_specs=[
            pl.BlockSpec((tt, width), rowmap),
            pl.BlockSpec((tt, width), rowmap),
            pl.BlockSpec((None, SUBLANES, width), lambda s, j: (s, 0, 0)),
            pl.BlockSpec((None, 1, width), lambda s, j: (s, 0, 0)),
            pl.BlockSpec((CONV_WIDTH, width), lambda s, j: (0, 0)),
            vec, _whole(), vec, _whole(), vec, vec,
        ],
        out_specs=[
            pl.BlockSpec((tt, width), lambda s, j: (s * tps + j, 0)),
            pl.BlockSpec((None, 1, width), lambda s, j: (s, 0, 0)),
        ],
        out_shape=[
            jax.ShapeDtypeStruct((n_seq * seq_len, width), out_dtype),
            jax.ShapeDtypeStruct((n_seq, 1, width), F32),
        ],
        scratch_shapes=[
            pltpu.VMEM((tt, width), F32),
            pltpu.VMEM((1, width), F32),
            pltpu.VMEM((tt, width), F32),
            pltpu.VMEM((tt, width), F32),
            pltpu.VMEM((tt, width), F32),
        ],
        compiler_params=_cparams(2),
        name="rglru",
    )(u, gact, prev0, h0, cw, cb.reshape(1, -1), wa, ba.reshape(1, -1), wx, bx.reshape(1, -1), lam.reshape(1, -1))


CMP_PAGES_PER_STEP = 16


def _cmp_half_pages_kernel(pt_ref, *refs, n_step):
    del pt_ref
    page_refs, (wk_ref, wv_ref, o_ref) = refs[:n_step], refs[n_step:]
    c4 = 2 * NSA_KV_HEADS
    halves = PAGE_SIZE // CMP_STRIDE
    for c in range(c4):
        w_ref = wk_ref if c < NSA_KV_HEADS else wv_ref
        acc = jnp.zeros((n_step * halves, 2 * HEAD_DIM), F32)
        for l in range(CMP_STRIDE):
            x = jnp.concatenate([pr[pl.ds(l * c4 + c, halves, stride=CMP_STRIDE * c4), :] for pr in page_refs], axis=0)
            acc = acc + _dot(x.astype(BF16), w_ref[l])
        o_ref[:, c * 2 * HEAD_DIM:(c + 1) * 2 * HEAD_DIM] = acc


def _cmp_half_pages(cache, layer, page_table, w1k_cat, w1v_cat):
    n_seq, n_pages = page_table.shape
    n_step = CMP_PAGES_PER_STEP
    c4 = 2 * NSA_KV_HEADS
    halves = PAGE_SIZE // CMP_STRIDE
    assert n_pages % n_step == 0 and PAGE_SIZE % CMP_STRIDE == 0 and halves == SUBLANES
    steps = n_pages // n_step
    grid_spec = pltpu.PrefetchScalarGridSpec(
        num_scalar_prefetch=1,
        grid=(n_seq, steps),
        in_specs=_page_specs(n_step, PAGE_SIZE * c4, layer) + [_whole(), _whole()],
        out_specs=pl.BlockSpec((n_step * halves, c4 * 2 * HEAD_DIM), lambda s, p, pt: (s * steps + p, 0)),
    )
    return pl.pallas_call(
        functools.partial(_cmp_half_pages_kernel, n_step=n_step),
        grid_spec=grid_spec,
        out_shape=jax.ShapeDtypeStruct((n_seq * n_pages * halves, c4 * 2 * HEAD_DIM), F32),
        compiler_params=_cparams(2),
        name="nsa_cmp_half_pages",
    )(page_table, *([cache] * n_step), w1k_cat, w1v_cat)


def _cmp_half_kernel(*refs, n_half):
    x_refs, (wk_ref, wv_ref, o_ref) = refs[:-3], refs[-3:]
    for c in range(2 * NSA_KV_HEADS):
        w_ref = wk_ref if c < NSA_KV_HEADS else wv_ref
        acc = jnp.zeros((n_half, 2 * HEAD_DIM), F32)
        for l in range(CMP_STRIDE):
            x = x_refs[c][pl.ds(l, n_half, stride=CMP_STRIDE), :]
            acc = acc + _dot(x.astype(BF16), w_ref[l])
        o_ref[:, c * 2 * HEAD_DIM:(c + 1) * 2 * HEAD_DIM] = acc


def _cmp_finish_kernel(ab_ref, pek_ref, pev_ref, w1k_ref, w1v_ref, w2k_ref, w2v_ref, o_ref, *, n_half):
    nc = n_half - (CMP_BLOCK // CMP_STRIDE - 1)
    rowi = lax.broadcasted_iota(jnp.int32, (n_half, HEAD_DIM), 0)
    for c in range(2 * NSA_KV_HEADS):
        is_k = c < NSA_KV_HEADS
        pe = (pek_ref if is_k else pev_ref)[...]
        w1 = (w1k_ref if is_k else w1v_ref)[...]
        w2 = (w2k_ref if is_k else w2v_ref)[...]
        bias = _dot(pe.astype(BF16), w1)[0:1, :]
        a = ab_ref[:, c * 2 * HEAD_DIM:c * 2 * HEAD_DIM + HEAD_DIM]
        b = ab_ref[:, c * 2 * HEAD_DIM + HEAD_DIM:(c + 1) * 2 * HEAD_DIM]
        hid = a + pltpu.roll(b, n_half - 1, 0) + bias
        tok = _dot(_gelu(hid).astype(BF16), w2)
        o_ref[c] = jnp.where(rowi < nc, tok, 0.0)


def _cmp_half(x, row0, n_seq, seq_len, w1k_cat, w1v_cat):
    rows_per_step = 2048
    assert CMP_BLOCK == 2 * CMP_STRIDE and seq_len % rows_per_step == 0 and row0 % rows_per_step == 0
    n_half_step = rows_per_step // CMP_STRIDE
    n_steps = n_seq * seq_len // rows_per_step
    c4 = 2 * NSA_KV_HEADS
    return pl.pallas_call(
        functools.partial(_cmp_half_kernel, n_half=n_half_step),
        grid=(n_steps,),
        in_specs=[pl.BlockSpec((rows_per_step, HEAD_DIM), functools.partial(lambda i, c: (row0 // rows_per_step + i, c), c=c))
                  for c in range(c4)] + [_whole(), _whole()],
        out_specs=pl.BlockSpec((n_half_step, c4 * 2 * HEAD_DIM), lambda i: (i, 0)),
        out_shape=jax.ShapeDtypeStruct((n_steps * n_half_step, c4 * 2 * HEAD_DIM), F32),
        compiler_params=_cparams(1),
        name="nsa_cmp_half",
    )(*([x] * c4), w1k_cat, w1v_cat)


def _cmp_finish(ab, n_seq, seq_len, pe_k, pe_v, w1k, w1v, w2k, w2v):
    n_half = seq_len // CMP_STRIDE
    c4 = 2 * NSA_KV_HEADS
    return pl.pallas_call(
        functools.partial(_cmp_finish_kernel, n_half=n_half),
        grid=(n_seq,),
        in_specs=[pl.BlockSpec((n_half, c4 * 2 * HEAD_DIM), lambda s: (s, 0)),
                  _whole(), _whole(), _whole(), _whole(), _whole(), _whole()],
        out_specs=pl.BlockSpec((None, c4, n_half, HEAD_DIM), lambda s: (s, 0, 0, 0)),
        out_shape=jax.ShapeDtypeStruct((n_seq, c4, n_half, HEAD_DIM), F32),
        compiler_params=_cparams(1),
        name="nsa_cmp_finish",
    )(ab, pe_k, pe_v, w1k, w1v, w2k, w2v)


def _sel_map(n_rows, n_cols, transposed=False):
    r = lax.broadcasted_iota(jnp.int32, (n_rows, n_cols), 0)
    k = lax.broadcasted_iota(jnp.int32, (n_rows, n_cols), 1)
    c, j = (k, r) if transposed else (r, k)
    d = c - j * (SEL_BLOCK // CMP_STRIDE)
    w = jnp.zeros((n_rows, n_cols), F32)
    for m in range(SEL_BLOCK // CMP_STRIDE):
        for n in range(CMP_BLOCK // CMP_STRIDE):
            w = w + jnp.where(d == m - n, 1.0, 0.0)
    return w


def _split3(x):
    hi = x.astype(BF16)
    r1 = x - hi.astype(F32)
    mid = r1.astype(BF16)
    return hi, mid, (r1 - mid.astype(F32)).astype(BF16)


def _importance(p_sum, sel_map_bf16):
    hi, mid, lo = _split3(p_sum)
    return _dot(hi, sel_map_bf16) + _dot(mid, sel_map_bf16) + _dot(lo, sel_map_bf16)


def _importance_t(sel_map_t_bf16, p_sum_t):
    hi, mid, lo = _split3(p_sum_t)
    return _dot(sel_map_t_bf16, hi) + _dot(sel_map_t_bf16, mid) + _dot(sel_map_t_bf16, lo)


def _select_blocks(imp, cur, lane, n_sel):
    forced = (lane == 0) | (lane == cur) | (lane == cur - 1)
    allowed = lane <= cur
    score = jnp.where(allowed, jnp.where(forced, 1e30, imp), NEG)
    return allowed & (_rank_before(score, lane, n_sel) < SEL_TOPN)


def _masked_softmax_rows(s, mask):
    s = jnp.where(mask, s, NEG)
    m = jnp.max(s, axis=1, keepdims=True)
    e = jnp.where(mask, jnp.exp(s - m), 0.0)
    return e / jnp.maximum(jnp.sum(e, axis=1, keepdims=True), 1e-30)


def _nsa_prompt_kernel(q_ref, g_ref, ck_ref, cv_ref, ks_ref, vs_ref, kw_ref, vw_ref, o_ref,
                       ckb_ref, cvt_ref, ksb_ref, vst_ref, kwb_ref, vwt_ref, *, seq, tq):
    i = pl.program_id(2)
    grp = GROUP_C
    nq = grp * tq
    n_cmp = ck_ref.shape[0]
    nc = n_cmp - (CMP_BLOCK // CMP_STRIDE - 1)
    n_sel = seq // SEL_BLOCK

    @pl.when(i == 0)
    def _():
        ckb_ref[...] = ck_ref[...].astype(BF16)
        cvt_ref[...] = cv_ref[...].T.astype(BF16)
        ksb_ref[...] = ks_ref[...].astype(BF16)
        kwb_ref[...] = kw_ref[...].astype(BF16)
        for c in range(seq // (2 * tq)):
            vst_ref[c] = vs_ref[2 * c * tq:2 * (c + 1) * tq, :].T.astype(BF16)
        for b in range(seq // tq):
            vwt_ref[b] = vw_ref[b * tq:(b + 1) * tq, :].T.astype(BF16)

    q4 = jnp.concatenate([q_ref[:, g * HEAD_DIM:(g + 1) * HEAD_DIM] for g in range(grp)], axis=0)
    pos = i * tq + lax.broadcasted_iota(jnp.int32, (1, tq), 1)
    pos4 = jnp.concatenate([pos] * grp, axis=1)

    c_row = lax.broadcasted_iota(jnp.int32, (n_cmp, nq), 0)
    avail = (c_row * CMP_STRIDE + (CMP_BLOCK - 1) <= pos4) & (c_row < nc)
    s_c = jnp.where(avail, _dot_t(ckb_ref[...], q4), NEG)
    m_c = jnp.max(s_c, axis=0, keepdims=True)
    e_c = jnp.where(avail, jnp.exp(s_c - m_c), 0.0)
    p_c = e_c / jnp.maximum(jnp.sum(e_c, axis=0, keepdims=True), 1e-30)
    o_c = _dot(cvt_ref[...], p_c.astype(BF16))

    p_sum = p_c[:, 0:tq]
    for g in range(1, grp):
        p_sum = p_sum + p_c[:, g * tq:(g + 1) * tq]
    imp = _importance_t(_sel_map(LANES, n_cmp, transposed=True).astype(BF16), p_sum)[0:n_sel, :]
    j_row = lax.broadcasted_iota(jnp.int32, (n_sel, tq), 0)
    cur = pos // SEL_BLOCK
    forced = (j_row == 0) | (j_row == cur) | (j_row == cur - 1)
    allowed = j_row <= cur
    score = jnp.where(allowed, jnp.where(forced, 1e30, imp), NEG)
    selected = allowed & (_rank_before_t(score, n_sel) < SEL_TOPN)
    sel_bf = jnp.concatenate([jnp.where(selected, 1.0, 0.0), jnp.zeros((LANES - n_sel, tq), F32)],
                             axis=0).astype(BF16)

    tk = 2 * tq
    key = lax.broadcasted_iota(jnp.int32, (tk, tq), 0)
    ek = lax.broadcasted_iota(jnp.int32, (tk, LANES), 0)
    ej = lax.broadcasted_iota(jnp.int32, (tk, LANES), 1)

    def sel_trip(c, carry):
        m, l, acc = carry
        expand = jnp.where(ej == c * (tk // SEL_BLOCK) + ek // SEL_BLOCK, 1.0, 0.0).astype(BF16)
        chosen = _dot(expand, sel_bf) > 0.5
        bias = jnp.where(chosen & (c * tk + key <= pos), 0.0, NEG)
        s = _dot_t(ksb_ref[pl.ds(pl.multiple_of(c * tk, tk), tk), :], q4) + jnp.concatenate([bias] * grp, axis=1)
        m_new = jnp.maximum(m, jnp.max(s, axis=0, keepdims=True))
        p = jnp.exp(s - m_new)
        alpha = jnp.exp(m - m_new)
        l = alpha * l + jnp.sum(p, axis=0, keepdims=True)
        return m_new, l, alpha * acc + _dot(vst_ref[c], p.astype(BF16))

    init = (jnp.full((1, nq), NEG, F32), jnp.zeros((1, nq), F32), jnp.zeros((HEAD_DIM, nq), F32))
    m_s, l_s, acc_s = lax.fori_loop(0, i // 2 + 1, sel_trip, init)
    o_s = acc_s / jnp.maximum(l_s, 1e-30)

    n_wb = WINDOW // tq + 1
    start = jnp.minimum(jnp.maximum(i - (n_wb - 1), 0), seq // tq - n_wb)
    wkey = start * tq + lax.broadcasted_iota(jnp.int32, (n_wb * tq, tq), 0)
    wbias = jnp.where((wkey <= pos) & (wkey > pos - WINDOW), 0.0, NEG)
    s_w = _dot_t(kwb_ref[pl.ds(pl.multiple_of(start * tq, tq), n_wb * tq), :], q4)
    s_w = s_w + jnp.concatenate([wbias] * grp, axis=1)
    p_w = jnp.exp(s_w - jnp.max(s_w, axis=0, keepdims=True))
    vwt = jnp.concatenate([vwt_ref[start + b] for b in range(n_wb)], axis=1)
    o_w = _dot(vwt, p_w.astype(BF16)) / jnp.maximum(jnp.sum(p_w, axis=0, keepdims=True), 1e-30)

    gates = g_ref[...].T
    for g in range(grp):
        cs = slice(g * tq, (g + 1) * tq)
        o = (gates[3 * g:3 * g + 1, :] * o_c[:, cs] + gates[3 * g + 1:3 * g + 2, :] * o_s[:, cs]
             + gates[3 * g + 2:3 * g + 3, :] * o_w[:, cs])
        o_ref[:, g * HEAD_DIM:(g + 1) * HEAD_DIM] = o.T.astype(o_ref.dtype)


def _nsa_prompt(q, gates, ckv, sel, win, batch, seq):
    tq = 256
    nq = seq // tq
    kvh = NSA_KV_HEADS
    gw = GROUP_C * HEAD_DIM
    n_cmp = ckv.shape[2]
    assert seq // SEL_BLOCK <= LANES and n_cmp <= LANES and WINDOW % tq == 0
    assert nq % 2 == 0 and nq >= WINDOW // tq + 1
    full = lambda off: pl.BlockSpec((seq, HEAD_DIM), lambda b, k, i: (b, off + k))
    return pl.pallas_call(
        functools.partial(_nsa_prompt_kernel, seq=seq, tq=tq),
        grid=(batch, kvh, nq),
        in_specs=[
            pl.BlockSpec((tq, gw), lambda b, k, i: (b * nq + i, k)),
            pl.BlockSpec((tq, LANES), lambda b, k, i: (b * nq + i, k)),
            pl.BlockSpec((None, None, n_cmp, HEAD_DIM), lambda b, k, i: (b, k, 0, 0)),
            pl.BlockSpec((None, None, n_cmp, HEAD_DIM), lambda b, k, i: (b, kvh + k, 0, 0)),
            full(0), full(kvh), full(0), full(kvh),
        ],
        out_specs=pl.BlockSpec((tq, gw), lambda b, k, i: (b * nq + i, k)),
        out_shape=jax.ShapeDtypeStruct((batch * seq, N_HEADS * HEAD_DIM), BF16),
        scratch_shapes=[
            pltpu.VMEM((n_cmp, HEAD_DIM), BF16),
            pltpu.VMEM((HEAD_DIM, n_cmp), BF16),
            pltpu.VMEM((seq, HEAD_DIM), BF16),
            pltpu.VMEM((nq // 2, HEAD_DIM, 2 * tq), BF16),
            pltpu.VMEM((seq, HEAD_DIM), BF16),
            pltpu.VMEM((nq, HEAD_DIM, tq), BF16),
        ],
        compiler_params=_cparams(3),
        name="nsa_prompt",
    )(q, gates, ckv, ckv, sel, sel, win, win)


def _nsa_sample_pre_kernel(q_ref, ckv_ref, wst_ref, wnew_ref, oc_ref, ow_ref, selm_ref, *, t_new, past_len):
    grp = GROUP_C
    kvh = NSA_KV_HEADS
    n_cmp = ckv_ref.shape[1]
    nc = n_cmp - (CMP_BLOCK // CMP_STRIDE - 1)
    wb = wst_ref.shape[0] // (2 * kvh)
    n_sel = (past_len + t_new + SEL_BLOCK - 1) // SEL_BLOCK
    sel_lanes = selm_ref.shape[1]
    rows = grp * t_new
    rowi = lax.broadcasted_iota(jnp.int32, (rows, 1), 0)
    pos = past_len + rowi % t_new
    pos_t = past_len + lax.broadcasted_iota(jnp.int32, (t_new, 1), 0)
    lane_c = lax.broadcasted_iota(jnp.int32, (rows, n_cmp), 1)
    avail = (lane_c * CMP_STRIDE + (CMP_BLOCK - 1) <= pos) & (lane_c < nc)
    lane_s = lax.broadcasted_iota(jnp.int32, (t_new, sel_lanes), 1)
    smap = _sel_map(n_cmp, sel_lanes).astype(BF16)
    lane_w = lax.broadcasted_iota(jnp.int32, (rows, wb), 1)
    wpos = past_len - wb + lane_w
    mask_w = (wpos <= pos) & (wpos > pos - WINDOW)
    lane_n = lax.broadcasted_iota(jnp.int32, (rows, LANES), 1)
    npos = past_len + lane_n
    mask_n = (npos <= pos) & (npos > pos - WINDOW) & (lane_n < t_new)
    for k in range(kvh):
        q4 = jnp.concatenate(
            [q_ref[:, (k * grp + g) * HEAD_DIM:(k * grp + g + 1) * HEAD_DIM] for g in range(grp)], axis=0).astype(BF16)
        p_c = _masked_softmax_rows(_dot_t(q4, ckv_ref[k].astype(BF16)), avail)
        oc_ref[k * rows:(k + 1) * rows, :] = _dot(p_c.astype(BF16), ckv_ref[kvh + k].astype(BF16))
        p_sum = p_c[0:t_new]
        for g in range(1, grp):
            p_sum = p_sum + p_c[g * t_new:(g + 1) * t_new]
        imp = _importance(p_sum, smap)
        selected = _select_blocks(imp, pos_t // SEL_BLOCK, lane_s, n_sel)
        selm_ref[k * t_new:(k + 1) * t_new, :] = jnp.where(selected, 1.0, 0.0)
        kst = wst_ref[pl.ds(k, wb, stride=2 * kvh), :].astype(BF16)
        vst = wst_ref[pl.ds(kvh + k, wb, stride=2 * kvh), :].astype(BF16)
        kn = _pad_rows(wnew_ref[:, k * HEAD_DIM:(k + 1) * HEAD_DIM], LANES).astype(BF16)
        vn = _pad_rows(wnew_ref[:, (kvh + k) * HEAD_DIM:(kvh + k + 1) * HEAD_DIM], LANES).astype(BF16)
        s1 = jnp.where(mask_w, _dot_t(q4, kst), NEG)
        s2 = jnp.where(mask_n, _dot_t(q4, kn), NEG)
        m = jnp.maximum(jnp.max(s1, axis=1, keepdims=True), jnp.max(s2, axis=1, keepdims=True))
        e1 = jnp.where(mask_w, jnp.exp(s1 - m), 0.0)
        e2 = jnp.where(mask_n, jnp.exp(s2 - m), 0.0)
        den = jnp.maximum(jnp.sum(e1, axis=1, keepdims=True) + jnp.sum(e2, axis=1, keepdims=True), 1e-30)
        ow_ref[k * rows:(k + 1) * rows, :] = (_dot(e1.astype(BF16), vst) + _dot(e2.astype(BF16), vn)) / den


def _nsa_sample_pre(q_s, ckv, win_state, layer, win_new, past_len):
    n_seq, t_new, _ = q_s.shape
    n_cmp = ckv.shape[2]
    wrows = win_state.shape[2]
    c4 = 2 * NSA_KV_HEADS * HEAD_DIM
    rows = N_HEADS * t_new
    n_sel = (past_len + t_new + SEL_BLOCK - 1) // SEL_BLOCK
    sel_lanes = -(-n_sel // LANES) * LANES
    return pl.pallas_call(
        functools.partial(_nsa_sample_pre_kernel, t_new=t_new, past_len=past_len),
        grid=(n_seq,),
        in_specs=[
            pl.BlockSpec((None, t_new, N_HEADS * HEAD_DIM), lambda s: (s, 0, 0)),
            pl.BlockSpec((None, 2 * NSA_KV_HEADS, n_cmp, HEAD_DIM), lambda s: (s, 0, 0, 0)),
            pl.BlockSpec((None, None, wrows, HEAD_DIM), lambda s: (layer, s, 0, 0)),
            pl.BlockSpec((None, t_new, c4), lambda s: (s, 0, 0)),
        ],
        out_specs=[
            pl.BlockSpec((None, rows, HEAD_DIM), lambda s: (s, 0, 0)),
            pl.BlockSpec((None, rows, HEAD_DIM), lambda s: (s, 0, 0)),
            pl.BlockSpec((None, NSA_KV_HEADS * t_new, sel_lanes), lambda s: (s, 0, 0)),
        ],
        out_shape=[
            jax.ShapeDtypeStruct((n_seq, rows, HEAD_DIM), F32),
            jax.ShapeDtypeStruct((n_seq, rows, HEAD_DIM), F32),
            jax.ShapeDtypeStruct((n_seq, NSA_KV_HEADS * t_new, sel_lanes), F32),
        ],
        compiler_params=_cparams(1),
        name="nsa_sample_pre",
    )(q_s, ckv, win_state, win_new)


def _nsa_sample_sel_kernel(pt_ref, q_ref, g_ref, snew_ref, selm_ref, oc_ref, ow_ref, *refs,
                           n_pages, n_step, t_new, past_len):
    del pt_ref
    page_refs = refs[:n_step]
    o_ref, qb_ref, selrows_ref, mx_ref, lx_ref, po_ref = refs[n_step:]
    p = pl.program_id(1)
    grp = GROUP_C
    kvh = NSA_KV_HEADS
    rows = grp * t_new
    sel_lanes = selm_ref.shape[1]
    stride = 2 * kvh
    assert PAGE_SIZE == 2 * SEL_BLOCK

    @pl.when(p == 0)
    def _():
        for k in range(kvh):
            qb_ref[k] = _group_rows(q_ref[...], k * grp, grp).astype(BF16)
            selrows_ref[k] = jnp.concatenate([selm_ref[k * t_new:(k + 1) * t_new, :]] * grp, axis=0)
        mx_ref[...] = jnp.zeros(mx_ref.shape, F32)
        lx_ref[...] = jnp.zeros(lx_ref.shape, F32)

    lane = lax.broadcasted_iota(jnp.int32, (rows, LANES), 1)
    lane_s = lax.broadcasted_iota(jnp.int32, (rows, sel_lanes), 1)

    def sel_col(selrows, block):
        return jnp.sum(jnp.where(lane_s == block, selrows, 0.0), axis=1, keepdims=True)

    n_keys = n_step * PAGE_SIZE
    ej = lax.broadcasted_iota(jnp.int32, (sel_lanes, n_keys), 0)
    ec = lax.broadcasted_iota(jnp.int32, (sel_lanes, n_keys), 1)
    expand = jnp.where(ej == p * (n_keys // SEL_BLOCK) + ec // SEL_BLOCK, 1.0, 0.0).astype(BF16)
    here = lane == p
    for k in range(kvh):
        k_all = jnp.concatenate([pr[pl.ds(k, PAGE_SIZE, stride=stride), :] for pr in page_refs], axis=0)
        v_all = jnp.concatenate([pr[pl.ds(kvh + k, PAGE_SIZE, stride=stride), :] for pr in page_refs], axis=0)
        mask = _dot(selrows_ref[k].astype(BF16), expand) > 0.5
        s = jnp.where(mask, _dot_t(qb_ref[k], k_all.astype(BF16)), NEG)
        m = jnp.max(s, axis=1, keepdims=True)
        e = jnp.where(mask, jnp.exp(s - m), 0.0)
        po_ref[p, k] = _dot(e.astype(BF16), v_all.astype(BF16))
        mx_ref[k] = jnp.where(here, m, mx_ref[k])
        lx_ref[k] = jnp.where(here, jnp.sum(e, axis=1, keepdims=True), lx_ref[k])

    @pl.when(p == n_pages // n_step - 1)
    def _():
        rowi = lax.broadcasted_iota(jnp.int32, (rows, LANES), 0)
        causal = lane <= rowi % t_new
        is_page = lane < n_pages // n_step
        gates = g_ref[...]
        for k in range(kvh):
            selrows = selrows_ref[k]
            mask_n = causal & (sel_col(selrows, past_len // SEL_BLOCK) > 0.5)
            kn = _pad_rows(snew_ref[:, k * HEAD_DIM:(k + 1) * HEAD_DIM], LANES).astype(BF16)
            vn = _pad_rows(snew_ref[:, (kvh + k) * HEAD_DIM:(kvh + k + 1) * HEAD_DIM], LANES).astype(BF16)
            sn = jnp.where(mask_n, _dot_t(qb_ref[k], kn), NEG)
            mx = mx_ref[k]
            m_tot = jnp.maximum(jnp.max(sn, axis=1, keepdims=True),
                                jnp.max(jnp.where(is_page, mx, NEG), axis=1, keepdims=True))
            pn = jnp.where(mask_n, jnp.exp(sn - m_tot), 0.0)
            w = jnp.where(is_page, jnp.exp(mx - m_tot), 0.0)
            l_tot = jnp.sum(pn, axis=1, keepdims=True) + jnp.sum(w * lx_ref[k], axis=1, keepdims=True)
            o_s = _dot(pn.astype(BF16), vn)
            for st in range(n_pages // n_step):
                o_s = o_s + w[:, st:st + 1] * po_ref[st, k]
            o_s = o_s / jnp.maximum(l_tot, 1e-30)
            cols = [jnp.concatenate([gates[:, k * LANES + 3 * g + br:k * LANES + 3 * g + br + 1] for g in range(grp)],
                                    axis=0) for br in range(3)]
            rs = slice(k * rows, (k + 1) * rows)
            o = cols[0] * oc_ref[rs, :] + cols[1] * o_s + cols[2] * ow_ref[rs, :]
            for g in range(grp):
                h = k * grp + g
                o_ref[:, h * HEAD_DIM:(h + 1) * HEAD_DIM] = o[g * t_new:(g + 1) * t_new, :]


def _nsa_sample_sel(q_s, g_s, sel_new, selm, o_c, o_w, cache, layer, page_table, past_len):
    n_seq, t_new, _ = q_s.shape
    n_pages = page_table.shape[1]
    kvh = NSA_KV_HEADS
    c4 = 2 * kvh * HEAD_DIM
    rows = GROUP_C * t_new
    sel_lanes = selm.shape[2]
    n_step = PAGES_PER_STEP
    assert past_len % SEL_BLOCK == 0 and t_new <= SEL_BLOCK and past_len == n_pages * PAGE_SIZE
    assert n_pages <= LANES and n_pages % n_step == 0
    per_seq = lambda shape: pl.BlockSpec((None,) + shape, lambda s, p, pt: (s, 0, 0))
    grid_spec = pltpu.PrefetchScalarGridSpec(
        num_scalar_prefetch=1,
        grid=(n_seq, n_pages // n_step),
        in_specs=[
            per_seq((t_new, N_HEADS * HEAD_DIM)),
            per_seq((t_new, kvh * LANES)),
            per_seq((t_new, c4)),
            per_seq((kvh * t_new, sel_lanes)),
            per_seq((kvh * rows, HEAD_DIM)),
            per_seq((kvh * rows, HEAD_DIM)),
        ] + _page_specs(n_step, PAGE_SIZE * 2 * kvh, layer),
        out_specs=per_seq((t_new, N_HEADS * HEAD_DIM)),
        scratch_shapes=[
            pltpu.VMEM((kvh, rows, HEAD_DIM), BF16),
            pltpu.VMEM((kvh, rows, sel_lanes), F32),
            pltpu.VMEM((kvh, rows, LANES), F32),
            pltpu.VMEM((kvh, rows, LANES), F32),
            pltpu.VMEM((n_pages // n_step, kvh, rows, HEAD_DIM), F32),
        ],
    )
    return pl.pallas_call(
        functools.partial(_nsa_sample_sel_kernel, n_pages=n_pages, n_step=n_step, t_new=t_new, past_len=past_len),
        grid_spec=grid_spec,
        out_shape=jax.ShapeDtypeStruct((n_seq, t_new, N_HEADS * HEAD_DIM), F32),
        compiler_params=_cparams(2),
        name="nsa_sample_sel",
    )(page_table, q_s, g_s, sel_new, selm, o_c, o_w, *([cache] * n_step))


def _rope_tables(seq, t_new, past_len, sample_rows):
    half = HEAD_DIM // 2
    freq = ROPE_THETA ** (-jnp.arange(half, dtype=F32) / half)

    def tab(pos):
        ang = pos.astype(F32)[:, None] * freq[None, :]
        c, s = jnp.cos(ang), jnp.sin(ang)
        return jnp.concatenate([c, c], axis=1), jnp.concatenate([-s, s], axis=1)

    cp, sp = tab(jnp.arange(seq, dtype=jnp.int32))
    cs, ss = tab(past_len + jnp.arange(t_new, dtype=jnp.int32))
    reps = sample_rows // t_new
    return (jnp.concatenate([cp, jnp.tile(cs, (reps, 1))], axis=0),
            jnp.concatenate([sp, jnp.tile(ss, (reps, 1))], axis=0))


def kernel(x_prompt, x_sample, p_prompt, p_sample, cache_moba_kv, cache_nsa_cmp_kv, cache_nsa_sel_kv, state_nsa_win_kv, state_lru_conv, state_lru_h, page_table, norm_mix, norm_ffn, norm_ple, norm_out, moba_w_qkv, moba_w_o, lru_w_in, lru_conv_w, lru_conv_b, lru_w_a, lru_b_a, lru_w_x, lru_b_x, lru_lambda, lru_w_o, nsa_w_in, nsa_w_o, nsa_cmp_pos_k, nsa_cmp_pos_v, nsa_cmp_k_w1, nsa_cmp_k_w2, nsa_cmp_v_w1, nsa_cmp_v_w2, ffn_w_in, ffn_w_out, ple_w_gate, ple_w_proj):
    batch, seq, d = x_prompt.shape
    n_seq, t_new, _ = x_sample.shape
    depth = norm_mix.shape[0]
    n_pages = page_table.shape[1]
    past_len = n_pages * PAGE_SIZE
    mp = batch * seq
    ms = n_seq * t_new
    assert d == D_MODEL and ms == ROW_TILE and seq % ROW_TILE == 0
    n_prompt_tiles = mp // ROW_TILE
    tiles_per_seq = seq // ROW_TILE
    hq = N_HEADS * HEAD_DIM

    h_parts = (x_prompt.reshape(mp, d), x_sample.reshape(ms, d), 0)
    p_parts = (p_prompt.reshape(depth, mp, -1), p_sample.reshape(depth, ms, -1))
    cos_t, sin_t = _rope_tables(seq, t_new, past_len, ms)
    proj = functools.partial(_norm_proj, cos_t=cos_t, sin_t=sin_t, n_prompt_tiles=n_prompt_tiles,
                             tiles_per_seq=tiles_per_seq)

    native = lambda c: c.reshape(c.shape[:2] + (-1, HEAD_DIM))
    moba_cache, cmp_cache, sel_cache, win_state = map(
        native, (cache_moba_kv, cache_nsa_cmp_kv, cache_nsa_sel_kv, state_nsa_win_kv))

    outs = {k: [] for k in ("moba_p", "moba_s", "cmp_p", "cmp_s", "sel_p", "sel_s", "win_p", "win_s",
                            "conv_p", "conv_s", "hh_p", "hh_s")}
    y = None
    for i in range(depth):
        kind, j = i % N_MIXERS, i // N_MIXERS
        if kind == 0:
            hk = MOBA_KV_HEADS * HEAD_DIM
            segs = [(0, c, c, 512, "rope_query") for c in range(0, hq, 512)]
            segs += [(1, 0, hq, hk, "rope"), (1, hk, hq + hk, hk, "none")]
            q, (kv_p, kv_s) = proj(h_parts, norm_mix[i], moba_w_qkv[j].astype(BF16), segs=segs,
                                   outs=[(hq, BF16, False), (2 * hk, F32, True)])
            a_p = _moba_prompt(q, kv_p, batch, seq)
            a_s = _moba_sample(q[mp:].astype(F32).reshape(n_seq, t_new, hq), kv_s.reshape(n_seq, t_new, 2 * hk),
                               moba_cache, j, page_table, past_len)
            w_o = moba_w_o[j]
            outs["moba_p"].append(kv_p.reshape(batch, seq, 2, MOBA_KV_HEADS, HEAD_DIM))
            outs["moba_s"].append(kv_s.reshape(n_seq, t_new, 2, MOBA_KV_HEADS, HEAD_DIM))
        elif kind == 1:
            width = lru_w_in.shape[2] // 2
            segs = [(0, c, c, 512, "gelu") for c in range(0, width, 512)]
            segs += [(1, c, width + c, 512, "none") for c in range(0, width, 512)]
            gact, u = proj(h_parts, norm_mix[i], lru_w_in[j].astype(BF16), segs=segs,
                           outs=[(width, F32, False), (width, F32, False)])
            lru_w = (lru_conv_w[j], lru_conv_b[j], lru_w_a[j].astype(BF16), lru_b_a[j], lru_w_x[j].astype(BF16),
                     lru_b_x[j], lru_lambda[j])
            zeros_c = jnp.zeros((batch, SUBLANES, width), F32)
            zeros_h = jnp.zeros((batch, 1, width), F32)
            a_p, hh_p = _lru(u, gact, 0, batch, seq, ROW_TILE, zeros_c, zeros_h, *lru_w, out_dtype=BF16)
            prev_s = jnp.concatenate([jnp.zeros((n_seq, SUBLANES - (CONV_WIDTH - 1), width), F32),
                                      state_lru_conv[j]], axis=1)
            a_s, hh_s = _lru(u, gact, mp, n_seq, t_new, t_new, prev_s, state_lru_h[j][:, None, :], *lru_w,
                             out_dtype=F32)
            a_s = a_s.reshape(n_seq, t_new, width)
            w_o = lru_w_o[j]
            u_s = jnp.concatenate([state_lru_conv[j], u[mp:].reshape(n_seq, t_new, width)], axis=1)
            outs["conv_p"].append(jnp.stack([u[(b + 1) * seq - (CONV_WIDTH - 1):(b + 1) * seq] for b in range(batch)]))
            outs["conv_s"].append(u_s[:, t_new:])
            outs["hh_p"].append(hh_p.reshape(batch, width))
            outs["hh_s"].append(hh_s.reshape(n_seq, width))
        else:
            hk = NSA_KV_HEADS * HEAD_DIM
            w_in = nsa_w_in[j]
            wg = w_in[:, hq + 6 * hk:].reshape(d, NSA_KV_HEADS, GROUP_C * 3)
            wg = jnp.pad(wg, ((0, 0), (0, 0), (0, LANES - GROUP_C * 3))).reshape(d, NSA_KV_HEADS * LANES)
            w_all = jnp.concatenate([w_in[:, :hq + 6 * hk], wg], axis=1).astype(BF16)
            segs = [(0, c, c, 512, "rope_query") for c in range(0, hq, 512)]
            for br in range(3):
                segs += [(1 + br, 0, hq + 2 * br * hk, hk, "rope"), (1 + br, hk, hq + (2 * br + 1) * hk, hk, "none")]
            segs += [(4, 0, hq + 6 * hk, NSA_KV_HEADS * LANES, "sigmoid")]
            q, (cmp_p, cmp_s), (sel_p, sel_s), (win_p, win_s), gates = proj(
                h_parts, norm_mix[i], w_all, segs=segs,
                outs=[(hq, BF16, False), (2 * hk, F32, True), (2 * hk, F32, True), (2 * hk, F32, True),
                      (NSA_KV_HEADS * LANES, F32, False)])
            w1k, w1v = nsa_cmp_k_w1[j].astype(BF16), nsa_cmp_v_w1[j].astype(BF16)

            def cat(w1):
                w3 = w1.reshape(CMP_BLOCK, HEAD_DIM, w1.shape[1])
                return jnp.concatenate([w3[:CMP_STRIDE], w3[CMP_STRIDE:]], axis=2)

            pe_k = jnp.broadcast_to(nsa_cmp_pos_k[j].reshape(1, -1), (SUBLANES, CMP_BLOCK * HEAD_DIM))
            pe_v = jnp.broadcast_to(nsa_cmp_pos_v[j].reshape(1, -1), (SUBLANES, CMP_BLOCK * HEAD_DIM))
            w1k_cat, w1v_cat = cat(w1k), cat(w1v)
            fin_w = (pe_k, pe_v, w1k, w1v, nsa_cmp_k_w2[j].astype(BF16), nsa_cmp_v_w2[j].astype(BF16))
            ckv_p = _cmp_finish(_cmp_half(cmp_p, 0, batch, seq, w1k_cat, w1v_cat), batch, seq, *fin_w)
            a_p = _nsa_prompt(q, gates, ckv_p, sel_p, win_p, batch, seq)
            ckv_s = _cmp_finish(_cmp_half_pages(cmp_cache, j, page_table, w1k_cat, w1v_cat), n_seq, past_len, *fin_w)
            q_s = q[mp:].astype(F32).reshape(n_seq, t_new, hq)
            o_c, o_w, selm = _nsa_sample_pre(q_s, ckv_s, win_state, j, win_s.reshape(n_seq, t_new, 2 * hk), past_len)
            a_s = _nsa_sample_sel(q_s, gates[mp:].reshape(n_seq, t_new, -1), sel_s.reshape(n_seq, t_new, 2 * hk),
                                  selm, o_c, o_w, sel_cache, j, page_table, past_len)
            w_o = nsa_w_o[j]
            shp_p = (batch, seq, 2, NSA_KV_HEADS, HEAD_DIM)
            shp_s = (n_seq, t_new, 2, NSA_KV_HEADS, HEAD_DIM)
            outs["cmp_p"].append(cmp_p.reshape(shp_p))
            outs["cmp_s"].append(cmp_s.reshape(shp_s))
            outs["sel_p"].append(sel_p.reshape(shp_p))
            outs["sel_s"].append(sel_s.reshape(shp_s))
            wb = min(WINDOW, seq)
            outs["win_p"].append(win_p.reshape(shp_p)[:, seq - wb:])
            outs["win_s"].append(jnp.concatenate([state_nsa_win_kv[j], win_s.reshape(shp_s)], axis=1)[:, t_new:])
        res = _tail((a_p, a_s.reshape(ms, -1)), h_parts, p_parts, i, w_o.astype(BF16), norm_ffn[i],
                    ffn_w_in[i].astype(BF16), ffn_w_out[i].astype(BF16), norm_ple[i], ple_w_gate[i].astype(BF16),
                    ple_w_proj[i].astype(BF16), norm_out, n_prompt_tiles, final=(i == depth - 1))
        h_parts = (res[0], res[0], n_prompt_tiles)
        if i == depth - 1:
            y = res[1:]
    return (y[0].reshape(batch, seq, d), y[1].reshape(n_seq, t_new, d),
            jnp.stack(outs["moba_p"]), jnp.stack(outs["moba_s"]),
            jnp.stack(outs["cmp_p"]), jnp.stack(outs["cmp_s"]),
            jnp.stack(outs["sel_p"]), jnp.stack(outs["sel_s"]),
            jnp.stack(outs["win_p"]), jnp.stack(outs["win_s"]),
            jnp.stack(outs["conv_p"]), jnp.stack(outs["conv_s"]),
            jnp.stack(outs["hh_p"]), jnp.stack(outs["hh_s"]))
```

```python
import functools

import jax
import jax.numpy as jnp
from jax import lax
from jax.experimental import pallas as pl
from jax.experimental.pallas import tpu as pltpu

F32 = jnp.float32
BF16 = jnp.bfloat16

D_MODEL = 1024
N_HEADS = 8
HEAD_DIM = 128
ROPE_THETA = 10000.0
RMS_EPS = 1e-6
N_MIXERS = 3
PAGE_SIZE = 128
MOBA_KV_HEADS = 4
MOBA_BLOCK = 256
MOBA_TOPK = 3
LRU_BLOCK_WIDTH = 256
CONV_WIDTH = 4
LRU_C = 8.0
NSA_KV_HEADS = 2
GROUP_C = N_HEADS // NSA_KV_HEADS
CMP_BLOCK = 32
CMP_STRIDE = 16
SEL_BLOCK = 64
SEL_TOPN = 16
WINDOW = 512

LANES = 128
SUBLANES = 8
ROW_TILE = 256
VMEM_LIMIT = 56 * 1024 * 1024

NEG = -1e30
SCALE = HEAD_DIM ** -0.5


def _cparams(n_axes):
    return pltpu.CompilerParams(dimension_semantics=("arbitrary",) * n_axes, vmem_limit_bytes=VMEM_LIMIT)


def _whole():
    return pl.BlockSpec(memory_space=pltpu.VMEM)


def _gelu(x):
    return 0.5 * x * (1.0 + jnp.tanh(0.7978845608028654 * (x + 0.044715 * (x * x * x))))


def _sigmoid(x):
    return 1.0 / (1.0 + jnp.exp(-x))


def _dot(a, b):
    return jnp.dot(a, b, preferred_element_type=F32)


def _dot_t(a, b):
    return lax.dot_general(a, b, (((1,), (1,)), ((), ())), preferred_element_type=F32)


def _rms(x, g):
    ms = jnp.mean(x * x, axis=-1, keepdims=True)
    return x * lax.rsqrt(ms + RMS_EPS) * g


def _rank_before(score, lane, n):
    rank = jnp.zeros(score.shape, jnp.int32)
    for jp in range(n):
        col = score[:, jp:jp + 1]
        beats = (col > score) | ((col == score) & (jp < lane))
        rank = rank + beats.astype(jnp.int32)
    return rank


def _rank_before_t(score, n):
    row = lax.broadcasted_iota(jnp.int32, score.shape, 0)
    rank = jnp.zeros(score.shape, jnp.int32)
    for jp in range(n):
        r = score[jp:jp + 1, :]
        rank = rank + ((r > score) | ((r == score) & (jp < row))).astype(jnp.int32)
    return rank


def _part_specs(block, n_prompt_tiles, sample_block, lead=()):
    return [pl.BlockSpec((None,) * len(lead) + block, lambda i: lead + (jnp.minimum(i, n_prompt_tiles - 1), 0)),
            pl.BlockSpec((None,) * len(lead) + block, lambda i: lead + (sample_block, 0))]


def _store_part(is_prompt, ref_p, ref_s, cols, value):
    @pl.when(is_prompt)
    def _():
        ref_p[:, cols] = value.astype(ref_p.dtype)

    @pl.when(jnp.logical_not(is_prompt))
    def _():
        ref_s[:, cols] = value.astype(ref_s.dtype)


def _proj_kernel(x_ref, g_ref, w_ref, cos_ref, sin_ref, *out_refs, segs, token_rows):
    xn = _rms(x_ref[...], g_ref[...]).astype(BF16)
    for (oi, ocol, wcol, width, kind) in segs:
        z = _dot(xn, w_ref[:, wcol:wcol + width])
        if kind in ("rope", "rope_query"):
            cos = cos_ref[...]
            sin = sin_ref[...]
            if kind == "rope_query":
                cos = cos * SCALE
                sin = sin * SCALE
            parts = []
            for c in range(width // HEAD_DIM):
                zh = z[:, c * HEAD_DIM:(c + 1) * HEAD_DIM]
                parts.append(zh * cos + pltpu.roll(zh, HEAD_DIM // 2, 1) * sin)
            z = parts[0] if len(parts) == 1 else jnp.concatenate(parts, axis=1)
        elif kind == "sigmoid":
            z = _sigmoid(z)
        elif kind == "gelu":
            z = _gelu(z)
        r = token_rows[oi]
        if r:
            for c in range(width // HEAD_DIM):
                out_refs[oi][pl.ds(ocol // HEAD_DIM + c, ROW_TILE, stride=r), :] = z[:, c * HEAD_DIM:(c + 1) * HEAD_DIM]
        else:
            out_refs[oi][:, ocol:ocol + width] = z.astype(out_refs[oi].dtype)


def _norm_proj_rows(x, block0, n_tiles, tab_map, g, w_bf16, cos_t, sin_t, segs, outs):
    n = w_bf16.shape[1]
    token_rows = tuple(nc // HEAD_DIM if native else 0 for nc, _, native in outs)
    blocks = [(ROW_TILE * r, HEAD_DIM) if r else (ROW_TILE, nc) for (nc, _, _), r in zip(outs, token_rows)]
    return pl.pallas_call(
        functools.partial(_proj_kernel, segs=tuple(segs), token_rows=token_rows),
        grid=(n_tiles,),
        in_specs=[
            pl.BlockSpec((ROW_TILE, D_MODEL), lambda i: (block0 + i, 0)),
            pl.BlockSpec((1, D_MODEL), lambda i: (0, 0)),
            pl.BlockSpec((D_MODEL, n), lambda i: (0, 0)),
            pl.BlockSpec((ROW_TILE, HEAD_DIM), tab_map),
            pl.BlockSpec((ROW_TILE, HEAD_DIM), tab_map),
        ],
        out_specs=[pl.BlockSpec(blk, lambda i: (i, 0)) for blk in blocks],
        out_shape=[jax.ShapeDtypeStruct((n_tiles * blk[0], blk[1]), dt) for blk, (_, dt, _) in zip(blocks, outs)],
        compiler_params=_cparams(1),
        name="norm_proj",
    )(x, g.reshape(1, D_MODEL), w_bf16, cos_t, sin_t)


def _norm_proj(h_parts, g, w_bf16, cos_t, sin_t, segs, outs, n_prompt_tiles, tiles_per_seq):
    hp, hs, hs_block = h_parts
    res_p = _norm_proj_rows(hp, 0, n_prompt_tiles, lambda i: (i % tiles_per_seq, 0), g, w_bf16, cos_t, sin_t, segs, outs)
    res_s = _norm_proj_rows(hs, hs_block, 1, lambda i: (tiles_per_seq, 0), g, w_bf16, cos_t, sin_t, segs, outs)
    return list(zip(res_p, res_s))


def _tail_kernel(ap_ref, as_ref, hp_ref, hs_ref, pp_ref, ps_ref, wo_ref, gf_ref, wi_ref, wout_ref, gp_ref, wg_ref,
                 wp_ref, gout_ref, h_out_ref, *y_out_refs, ffn, chunk, n_prompt_tiles):
    is_prompt = pl.program_id(0) < n_prompt_tiles
    a = jnp.where(is_prompt, ap_ref[...], as_ref[...].astype(BF16))
    h = jnp.where(is_prompt, hp_ref[...], hs_ref[...])
    p = jnp.where(is_prompt, pp_ref[...], ps_ref[...])
    h1 = h + _dot(a, wo_ref[...])
    xn = _rms(h1, gf_ref[...]).astype(BF16)
    acc = jnp.zeros(h1.shape, F32)
    for c in range(ffn // chunk):
        zg = _dot(xn, wi_ref[:, c * chunk:(c + 1) * chunk])
        zu = _dot(xn, wi_ref[:, ffn + c * chunk:ffn + (c + 1) * chunk])
        act = (zg * _sigmoid(zg) * zu).astype(BF16)
        acc = acc + _dot(act, wout_ref[c * chunk:(c + 1) * chunk, :])
    h2 = h1 + acc
    xn2 = _rms(h2, gp_ref[...]).astype(BF16)
    gate = _sigmoid(_dot(xn2, wg_ref[...]))
    h3 = h2 + gate * _dot(p.astype(BF16), wp_ref[...])
    h_out_ref[...] = h3
    if y_out_refs:
        _store_part(is_prompt, y_out_refs[0], y_out_refs[1], slice(None), _rms(h3, gout_ref[...]))


def _tail(a_parts, h_parts, p_parts, layer, w_o, g_ffn, w_in, w_out, g_ple, w_gate, w_proj, g_out, n_prompt_tiles,
          final):
    mp = n_prompt_tiles * ROW_TILE
    ffn = w_out.shape[0]
    ple = p_parts[0].shape[2]
    vec = pl.BlockSpec((1, D_MODEL), lambda i: (0, 0))
    out_specs = [pl.BlockSpec((ROW_TILE, D_MODEL), lambda i: (i, 0))]
    out_shape = [jax.ShapeDtypeStruct((mp + ROW_TILE, D_MODEL), F32)]
    if final:
        out_specs += _part_specs((ROW_TILE, D_MODEL), n_prompt_tiles, 0)
        out_shape += [jax.ShapeDtypeStruct((mp, D_MODEL), F32), jax.ShapeDtypeStruct((ROW_TILE, D_MODEL), F32)]
    return pl.pallas_call(
        functools.partial(_tail_kernel, ffn=ffn, chunk=256, n_prompt_tiles=n_prompt_tiles),
        grid=(n_prompt_tiles + 1,),
        in_specs=_part_specs((ROW_TILE, D_MODEL), n_prompt_tiles, 0)
        + _part_specs((ROW_TILE, D_MODEL), n_prompt_tiles, h_parts[2])
        + _part_specs((ROW_TILE, ple), n_prompt_tiles, 0, lead=(layer,))
        + [_whole(), vec, _whole(), _whole(), vec, _whole(), _whole(), vec],
        out_specs=out_specs,
        out_shape=out_shape,
        compiler_params=_cparams(1),
        name="tail",
    )(a_parts[0], a_parts[1], h_parts[0], h_parts[1], p_parts[0], p_parts[1], w_o, g_ffn.reshape(1, -1), w_in, w_out,
      g_ple.reshape(1, -1), w_gate, w_proj, g_out.reshape(1, -1))


def _moba_prompt_kernel(q_ref, kv_ref, o_ref, kmean_ref, kb_ref, vt_ref, sel_ref, *, nb):
    i = pl.program_id(2)
    blk = MOBA_BLOCK
    kvh = MOBA_KV_HEADS
    grp = N_HEADS // kvh
    nq = grp * blk
    nb_pad = kmean_ref.shape[0]

    @pl.when(i == 0)
    def _():
        j = pl.program_id(1)
        k = kv_ref[pl.ds(j, nb * blk, stride=2 * kvh), :]
        km = jnp.sum(k.reshape(nb, blk, HEAD_DIM), axis=1) * (1.0 / blk)
        kmean_ref[...] = jnp.concatenate([km, jnp.zeros((nb_pad - nb, HEAD_DIM), F32)], axis=0).astype(BF16)
        kb_ref[...] = k.astype(BF16)
        for c in range(nb // 2):
            v = kv_ref[pl.ds(2 * c * blk * 2 * kvh + kvh + j, 2 * blk, stride=2 * kvh), :]
            vt_ref[c] = v.T.astype(BF16)

    q = jnp.concatenate([q_ref[:, g * HEAD_DIM:(g + 1) * HEAD_DIM] for g in range(grp)], axis=0)
    blk_row = lax.broadcasted_iota(jnp.int32, (nb_pad, nq), 0)
    valid = blk_row < i
    gate = jnp.where(valid, _dot_t(kmean_ref[...], q), NEG)
    sel_ref[...] = jnp.where(valid & (_rank_before_t(gate, nb) < MOBA_TOPK), 1.0, 0.0)

    def past(c, carry):
        m, l, acc = carry
        s = _dot_t(kb_ref[pl.ds(pl.multiple_of(c * 2 * blk, 2 * blk), 2 * blk), :], q)
        halves = []
        m_new = m
        for hh in range(2):
            on = sel_ref[pl.ds(2 * c + hh, 1), :] > 0.5
            sh = s[hh * blk:(hh + 1) * blk, :]
            m_new = jnp.maximum(m_new, jnp.where(on, jnp.max(sh, axis=0, keepdims=True), NEG))
            halves.append((on, sh))
        p = jnp.concatenate([jnp.exp(sh - jnp.where(on, m_new, -NEG)) for on, sh in halves], axis=0)
        alpha = jnp.exp(m - m_new)
        l = alpha * l + jnp.sum(p, axis=0, keepdims=True)
        return m_new, l, alpha * acc + _dot(vt_ref[c], p.astype(BF16))

    init = (jnp.full((1, nq), NEG, F32), jnp.zeros((1, nq), F32), jnp.zeros((HEAD_DIM, nq), F32))
    m, l, acc = lax.fori_loop(0, (i + 1) // 2, past, init)
    key = lax.broadcasted_iota(jnp.int32, (blk, nq), 0)
    qi = lax.broadcasted_iota(jnp.int32, (blk, nq), 1) % blk
    s = jnp.where(key <= qi, _dot_t(kb_ref[pl.ds(pl.multiple_of(i * blk, blk), blk), :], q), NEG)
    m_new = jnp.maximum(m, jnp.max(s, axis=0, keepdims=True))
    p = jnp.exp(s - m_new)
    alpha = jnp.exp(m - m_new)
    l = alpha * l + jnp.sum(p, axis=0, keepdims=True)
    vt_pair = vt_ref[i // 2]
    vt_own = jnp.where(i % 2 == 0, vt_pair[:, :blk], vt_pair[:, blk:])
    acc = alpha * acc + _dot(vt_own, p.astype(BF16))
    o = acc / jnp.maximum(l, 1e-30)
    for g in range(grp):
        o_ref[:, g * HEAD_DIM:(g + 1) * HEAD_DIM] = o[:, g * blk:(g + 1) * blk].T.astype(o_ref.dtype)


def _moba_prompt(q, kv, batch, seq):
    nb = seq // MOBA_BLOCK
    kvh = MOBA_KV_HEADS
    assert nb <= 2 * SUBLANES and nb % 2 == 0
    return pl.pallas_call(
        functools.partial(_moba_prompt_kernel, nb=nb),
        grid=(batch, kvh, nb),
        in_specs=[
            pl.BlockSpec((MOBA_BLOCK, 2 * HEAD_DIM), lambda b, j, i: (b * nb + i, j)),
            pl.BlockSpec((seq * 2 * kvh, HEAD_DIM), lambda b, j, i: (b, 0)),
        ],
        out_specs=pl.BlockSpec((MOBA_BLOCK, 2 * HEAD_DIM), lambda b, j, i: (b * nb + i, j)),
        out_shape=jax.ShapeDtypeStruct((batch * seq, N_HEADS * HEAD_DIM), BF16),
        scratch_shapes=[
            pltpu.VMEM((2 * SUBLANES, HEAD_DIM), BF16),
            pltpu.VMEM((seq, HEAD_DIM), BF16),
            pltpu.VMEM((nb // 2, HEAD_DIM, 2 * MOBA_BLOCK), BF16),
            pltpu.VMEM((2 * SUBLANES, (N_HEADS // kvh) * MOBA_BLOCK), F32),
        ],
        compiler_params=_cparams(3),
        name="moba_prompt",
    )(q, kv)


PAGES_PER_STEP = 8


def _group_rows(q, first_head, group):
    return jnp.concatenate(
        [q[:, (first_head + g) * HEAD_DIM:(first_head + g + 1) * HEAD_DIM] for g in range(group)], axis=0)


def _pad_rows(x, rows):
    return jnp.concatenate([x, jnp.zeros((rows - x.shape[0], x.shape[1]), x.dtype)], axis=0)


def _page_specs(n, rows, layer):
    return [pl.BlockSpec((None, None, rows, HEAD_DIM),
                         functools.partial(lambda s, p, pt, g: (layer, pt[s, n * p + g], 0, 0), g=g))
            for g in range(n)]


def _moba_sample_kernel(pt_ref, q_ref, kvn_ref, *refs, n_pages, n_step, t_new):
    del pt_ref
    page_refs = refs[:n_step]
    o_ref, qf_ref, qb_ref, gx_ref, mx_ref, lx_ref, po_ref = refs[n_step:]
    p = pl.program_id(1)
    kvh = MOBA_KV_HEADS
    group = N_HEADS // kvh
    rows = group * t_new
    stride = 2 * kvh

    @pl.when(p == 0)
    def _():
        for j in range(kvh):
            qj = _group_rows(q_ref[...], j * group, group)
            qf_ref[j] = qj
            qb_ref[j] = qj.astype(BF16)
        gx_ref[...] = jnp.zeros(gx_ref.shape, F32)
        mx_ref[...] = jnp.zeros(mx_ref.shape, F32)
        lx_ref[...] = jnp.zeros(lx_ref.shape, F32)

    lane = lax.broadcasted_iota(jnp.int32, (rows, LANES), 1)
    zero = jnp.zeros((rows, PAGE_SIZE), BF16)
    ks = [[page_refs[g][pl.ds(j, PAGE_SIZE, stride=stride), :] for g in range(n_step)] for j in range(kvh)]
    scores = [_dot_t(qb_ref[j], jnp.concatenate(ks[j], axis=0).astype(BF16)) for j in range(kvh)]
    e_all = []
    for j in range(kvh):
        qf = qf_ref[j]
        gx, mx, lx = gx_ref[j], mx_ref[j], lx_ref[j]
        e_rows = []
        for g in range(n_step):
            gate = jnp.sum(qf * jnp.sum(ks[j][g], axis=0, keepdims=True), axis=1, keepdims=True)
            sg = scores[j][:, g * PAGE_SIZE:(g + 1) * PAGE_SIZE]
            m = jnp.max(sg, axis=1, keepdims=True)
            e = jnp.exp(sg - m)
            e_rows.append(jnp.concatenate([e.astype(BF16) if gg == g else zero for gg in range(n_step)], axis=1))
            here = lane == p * n_step + g
            gx = jnp.where(here, gate, gx)
            mx = jnp.where(here, m, mx)
            lx = jnp.where(here, jnp.sum(e, axis=1, keepdims=True), lx)
        e_all.append(jnp.concatenate(e_rows, axis=0))
        gx_ref[j] = gx
        mx_ref[j] = mx
        lx_ref[j] = lx
    for j in range(kvh):
        v_all = jnp.concatenate([page_refs[g][pl.ds(kvh + j, PAGE_SIZE, stride=stride), :] for g in range(n_step)],
                                axis=0)
        o_all = _dot(e_all[j], v_all.astype(BF16))
        for g in range(n_step):
            po_ref[p * n_step + g, j] = o_all[g * rows:(g + 1) * rows, :]

    @pl.when(p == n_pages // n_step - 1)
    def _():
        n_past = n_pages // 2
        rowi = lax.broadcasted_iota(jnp.int32, (rows, LANES), 0)
        mask_n = lane <= (rowi % t_new)
        first = (lane % 2 == 0) & (lane < n_pages)
        for j in range(kvh):
            g2 = gx_ref[j]
            gate = jnp.where(first, (g2 + pltpu.roll(g2, LANES - 1, 1)) * (1.0 / MOBA_BLOCK), NEG)
            rank = jnp.zeros(gate.shape, jnp.int32)
            for bp in range(n_past):
                col = gate[:, 2 * bp:2 * bp + 1]
                rank = rank + ((col > gate) | ((col == gate) & (2 * bp < lane))).astype(jnp.int32)
            sel_first = jnp.where(first & (rank < MOBA_TOPK), 1.0, 0.0)
            sel = (sel_first + pltpu.roll(sel_first, 1, 1)) > 0.5
            kn = _pad_rows(kvn_ref[pl.ds(j, t_new, stride=stride), :], LANES).astype(BF16)
            vn = _pad_rows(kvn_ref[pl.ds(kvh + j, t_new, stride=stride), :], LANES).astype(BF16)
            sn = jnp.where(mask_n, _dot_t(qb_ref[j], kn), NEG)
            mx = mx_ref[j]
            m_tot = jnp.maximum(jnp.max(sn, axis=1, keepdims=True),
                                jnp.max(jnp.where(sel, mx, NEG), axis=1, keepdims=True))
            pn = jnp.where(mask_n, jnp.exp(sn - m_tot), 0.0)
            w = jnp.where(sel, jnp.exp(mx - m_tot), 0.0)
            l_tot = jnp.sum(pn, axis=1, keepdims=True) + jnp.sum(w * lx_ref[j], axis=1, keepdims=True)
            o = _dot(pn.astype(BF16), vn)
            for pg in range(n_pages):
                o = o + w[:, pg:pg + 1] * po_ref[pg, j]
            o = o / jnp.maximum(l_tot, 1e-30)
            for g in range(group):
                h = j * group + g
                o_ref[:, h * HEAD_DIM:(h + 1) * HEAD_DIM] = o[g * t_new:(g + 1) * t_new, :]


def _moba_sample(q_s, kv_s, cache, layer, page_table, past_len):
    n_seq, t_new, _ = q_s.shape
    n_pages = page_table.shape[1]
    kvh = MOBA_KV_HEADS
    rows = (N_HEADS // kvh) * t_new
    n_step = PAGES_PER_STEP
    assert n_pages <= LANES and n_pages % n_step == 0 and past_len == n_pages * PAGE_SIZE
    assert MOBA_BLOCK == 2 * PAGE_SIZE and t_new <= MOBA_BLOCK
    per_seq = lambda r, w: pl.BlockSpec((None, r, w), lambda s, p, pt: (s, 0, 0))
    grid_spec = pltpu.PrefetchScalarGridSpec(
        num_scalar_prefetch=1,
        grid=(n_seq, n_pages // n_step),
        in_specs=[per_seq(t_new, N_HEADS * HEAD_DIM), per_seq(t_new * 2 * kvh, HEAD_DIM)]
        + _page_specs(n_step, PAGE_SIZE * 2 * kvh, layer),
        out_specs=per_seq(t_new, N_HEADS * HEAD_DIM),
        scratch_shapes=[
            pltpu.VMEM((kvh, rows, HEAD_DIM), F32),
            pltpu.VMEM((kvh, rows, HEAD_DIM), BF16),
            pltpu.VMEM((kvh, rows, LANES), F32),
            pltpu.VMEM((kvh, rows, LANES), F32),
            pltpu.VMEM((kvh, rows, LANES), F32),
            pltpu.VMEM((n_pages, kvh, rows, HEAD_DIM), F32),
        ],
    )
    return pl.pallas_call(
        functools.partial(_moba_sample_kernel, n_pages=n_pages, n_step=n_step, t_new=t_new),
        grid_spec=grid_spec,
        out_shape=jax.ShapeDtypeStruct((n_seq, t_new, N_HEADS * HEAD_DIM), F32),
        compiler_params=_cparams(2),
        name="moba_sample",
    )(page_table, q_s, kv_s, *([cache] * n_step))


def _lru_kernel(u_ref, gact_ref, prev0_ref, h0_ref, cw_ref, cb_ref, wa_ref, ba_ref, wx_ref, bx_ref, lam_ref,
                a_out_ref, hlast_ref, prev_s, h_s, a_s, b_s, y_s, *, tt):
    j = pl.program_id(1)
    width = u_ref.shape[1]

    @pl.when(j == 0)
    def _():
        if tt > SUBLANES:
            prev_s[0:tt - SUBLANES, :] = jnp.zeros((tt - SUBLANES, width), F32)
        prev_s[tt - SUBLANES:tt, :] = prev0_ref[...]
        h_s[...] = h0_ref[...]

    u = u_ref[...]
    prev = prev_s[...]
    rowi = lax.broadcasted_iota(jnp.int32, u.shape, 0)
    xc = cb_ref[...]
    for k in range(CONV_WIDTH):
        sh = CONV_WIDTH - 1 - k
        if sh == 0:
            us = u
        else:
            us = jnp.where(rowi < sh, pltpu.roll(prev, sh, 0), pltpu.roll(u, sh, 0))
        xc = xc + us * cw_ref[k:k + 1, :]
    prev_s[...] = u

    xb = xc.astype(BF16)
    nblk = width // LRU_BLOCK_WIDTH
    ra, rx = [], []
    for n in range(nblk):
        xs = xb[:, n * LRU_BLOCK_WIDTH:(n + 1) * LRU_BLOCK_WIDTH]
        ra.append(_dot(xs, wa_ref[n]))
        rx.append(_dot(xs, wx_ref[n]))
    r = _sigmoid(jnp.concatenate(ra, axis=1) + ba_ref[...])
    ig = _sigmoid(jnp.concatenate(rx, axis=1) + bx_ref[...])
    nl = -lam_ref[...]
    softplus = jnp.maximum(nl, 0.0) + jnp.log(1.0 + jnp.exp(-jnp.abs(nl)))
    log_a = -LRU_C * r * softplus
    a = jnp.exp(log_a)
    a_s[...] = a
    b_s[...] = jnp.sqrt(1.0 - a * a) * ig * xc

    def step(t, h):
        h = a_s[pl.ds(t, 1), :] * h + b_s[pl.ds(t, 1), :]
        y_s[pl.ds(t, 1), :] = h
        return h

    h = lax.fori_loop(0, tt, step, h_s[...], unroll=8)
    h_s[...] = h
    hlast_ref[...] = h
    a_out_ref[...] = (gact_ref[...] * y_s[...]).astype(a_out_ref.dtype)


def _lru(u, gact, row0, n_seq, seq_len, tt, prev0, h0, cw, cb, wa, ba, wx, bx, lam, out_dtype):
    width = u.shape[1]
    tps = seq_len // tt
    base = row0 // tt
    rowmap = lambda s, j: (base + s * tps + j, 0)
    vec = pl.BlockSpec((1, width), lambda s, j: (0, 0))
    return pl.pallas_call(
        functools.partial(_lru_kernel, tt=tt),
        grid=(n_seq, tps),
        in_specs=[
            pl.BlockSpec((tt, width), rowmap),
            pl.BlockSpec((tt, width), rowmap),
            pl.BlockSpec((None, SUBLANES, width), lambda s, j: (s, 0, 0)),
            pl.BlockSpec((None, 1, width), lambda s, j: (s, 0, 0)),
            pl.BlockSpec((CONV_WIDTH, width), lambda s, j: (0, 0)),
            vec, _whole(), vec, _whole(), vec, vec,
        ],
        out_specs=[
            pl.BlockSpec((tt, width), lambda s, j: (s * tps + j, 0)),
            pl.BlockSpec((None, 1, width), lambda s, j: (s, 0, 0)),
        ],
        out_shape=[
            jax.ShapeDtypeStruct((n_seq * seq_len, width), out_dtype),
            jax.ShapeDtypeStruct((n_seq, 1, width), F32),
        ],
        scratch_shapes=[
            pltpu.VMEM((tt, width), F32),
            pltpu.VMEM((1, width), F32),
            pltpu.VMEM((tt, width), F32),
            pltpu.VMEM((tt, width), F32),
            pltpu.VMEM((tt, width), F32),
        ],
        compiler_params=_cparams(2),
        name="rglru",
    )(u, gact, prev0, h0, cw, cb.reshape(1, -1), wa, ba.reshape(1, -1), wx, bx.reshape(1, -1), lam.reshape(1, -1))


CMP_PAGES_PER_STEP = 16


def _cmp_half_pages_kernel(pt_ref, *refs, n_step):
    del pt_ref
    page_refs, (wk_ref, wv_ref, o_ref) = refs[:n_step], refs[n_step:]
    c4 = 2 * NSA_KV_HEADS
    halves = PAGE_SIZE // CMP_STRIDE
    for c in range(c4):
        w_ref = wk_ref if c < NSA_KV_HEADS else wv_ref
        acc = jnp.zeros((n_step * halves, 2 * HEAD_DIM), F32)
        for l in range(CMP_STRIDE):
            x = jnp.concatenate([pr[pl.ds(l * c4 + c, halves, stride=CMP_STRIDE * c4), :] for pr in page_refs], axis=0)
            acc = acc + _dot(x.astype(BF16), w_ref[l])
        o_ref[:, c * 2 * HEAD_DIM:(c + 1) * 2 * HEAD_DIM] = acc


def _cmp_half_pages(cache, layer, page_table, w1k_cat, w1v_cat):
    n_seq, n_pages = page_table.shape
    n_step = CMP_PAGES_PER_STEP
    c4 = 2 * NSA_KV_HEADS
    halves = PAGE_SIZE // CMP_STRIDE
    assert n_pages % n_step == 0 and PAGE_SIZE % CMP_STRIDE == 0 and halves == SUBLANES
    steps = n_pages // n_step
    grid_spec = pltpu.PrefetchScalarGridSpec(
        num_scalar_prefetch=1,
        grid=(n_seq, steps),
        in_specs=_page_specs(n_step, PAGE_SIZE * c4, layer) + [_whole(), _whole()],
        out_specs=pl.BlockSpec((n_step * halves, c4 * 2 * HEAD_DIM), lambda s, p, pt: (s * steps + p, 0)),
    )
    return pl.pallas_call(
        functools.partial(_cmp_half_pages_kernel, n_step=n_step),
        grid_spec=grid_spec,
        out_shape=jax.ShapeDtypeStruct((n_seq * n_pages * halves, c4 * 2 * HEAD_DIM), F32),
        compiler_params=_cparams(2),
        name="nsa_cmp_half_pages",
    )(page_table, *([cache] * n_step), w1k_cat, w1v_cat)


def _cmp_half_kernel(x_ref, wk_ref, wv_ref, o_ref, *, n_half):
    c4 = 2 * NSA_KV_HEADS
    for c in range(c4):
        w_ref = wk_ref if c < NSA_KV_HEADS else wv_ref
        acc = jnp.zeros((n_half, 2 * HEAD_DIM), F32)
        for l in range(CMP_STRIDE):
            x = x_ref[pl.ds(l * c4 + c, n_half, stride=CMP_STRIDE * c4), :]
            acc = acc + _dot(x.astype(BF16), w_ref[l])
        o_ref[:, c * 2 * HEAD_DIM:(c + 1) * 2 * HEAD_DIM] = acc


def _cmp_finish_kernel(ab_ref, pek_ref, pev_ref, w1k_ref, w1v_ref, w2k_ref, w2v_ref, o_ref, *, n_half):
    nc = n_half - (CMP_BLOCK // CMP_STRIDE - 1)
    rowi = lax.broadcasted_iota(jnp.int32, (n_half, HEAD_DIM), 0)
    for c in range(2 * NSA_KV_HEADS):
        is_k = c < NSA_KV_HEADS
        pe = (pek_ref if is_k else pev_ref)[...]
        w1 = (w1k_ref if is_k else w1v_ref)[...]
        w2 = (w2k_ref if is_k else w2v_ref)[...]
        bias = _dot(pe.astype(BF16), w1)[0:1, :]
        a = ab_ref[:, c * 2 * HEAD_DIM:c * 2 * HEAD_DIM + HEAD_DIM]
        b = ab_ref[:, c * 2 * HEAD_DIM + HEAD_DIM:(c + 1) * 2 * HEAD_DIM]
        hid = a + pltpu.roll(b, n_half - 1, 0) + bias
        tok = _dot(_gelu(hid).astype(BF16), w2)
        o_ref[c] = jnp.where(rowi < nc, tok, 0.0)


def _cmp_half(x, row0, n_seq, seq_len, w1k_cat, w1v_cat):
    rows_per_step = 2048
    assert CMP_BLOCK == 2 * CMP_STRIDE and seq_len % rows_per_step == 0 and row0 % rows_per_step == 0
    n_half_step = rows_per_step // CMP_STRIDE
    n_steps = n_seq * seq_len // rows_per_step
    c4 = 2 * NSA_KV_HEADS
    return pl.pallas_call(
        functools.partial(_cmp_half_kernel, n_half=n_half_step),
        grid=(n_steps,),
        in_specs=[pl.BlockSpec((rows_per_step * c4, HEAD_DIM), lambda i: (row0 // rows_per_step + i, 0)),
                  _whole(), _whole()],
        out_specs=pl.BlockSpec((n_half_step, c4 * 2 * HEAD_DIM), lambda i: (i, 0)),
        out_shape=jax.ShapeDtypeStruct((n_steps * n_half_step, c4 * 2 * HEAD_DIM), F32),
        compiler_params=_cparams(1),
        name="nsa_cmp_half",
    )(x, w1k_cat, w1v_cat)


def _cmp_finish(ab, n_seq, seq_len, pe_k, pe_v, w1k, w1v, w2k, w2v):
    n_half = seq_len // CMP_STRIDE
    c4 = 2 * NSA_KV_HEADS
    return pl.pallas_call(
        functools.partial(_cmp_finish_kernel, n_half=n_half),
        grid=(n_seq,),
        in_specs=[pl.BlockSpec((n_half, c4 * 2 * HEAD_DIM), lambda s: (s, 0)),
                  _whole(), _whole(), _whole(), _whole(), _whole(), _whole()],
        out_specs=pl.BlockSpec((None, c4, n_half, HEAD_DIM), lambda s: (s, 0, 0, 0)),
        out_shape=jax.ShapeDtypeStruct((n_seq, c4, n_half, HEAD_DIM), F32),
        compiler_params=_cparams(1),
        name="nsa_cmp_finish",
    )(ab, pe_k, pe_v, w1k, w1v, w2k, w2v)


def _sel_map(n_rows, n_cols, transposed=False):
    r = lax.broadcasted_iota(jnp.int32, (n_rows, n_cols), 0)
    k = lax.broadcasted_iota(jnp.int32, (n_rows, n_cols), 1)
    c, j = (k, r) if transposed else (r, k)
    d = c - j * (SEL_BLOCK // CMP_STRIDE)
    w = jnp.zeros((n_rows, n_cols), F32)
    for m in range(SEL_BLOCK // CMP_STRIDE):
        for n in range(CMP_BLOCK // CMP_STRIDE):
            w = w + jnp.where(d == m - n, 1.0, 0.0)
    return w


def _split3(x):
    hi = x.astype(BF16)
    r1 = x - hi.astype(F32)
    mid = r1.astype(BF16)
    return hi, mid, (r1 - mid.astype(F32)).astype(BF16)


def _importance(p_sum, sel_map_bf16):
    hi, mid, lo = _split3(p_sum)
    return _dot(hi, sel_map_bf16) + _dot(mid, sel_map_bf16) + _dot(lo, sel_map_bf16)


def _importance_t(sel_map_t_bf16, p_sum_t):
    hi, mid, lo = _split3(p_sum_t)
    return _dot(sel_map_t_bf16, hi) + _dot(sel_map_t_bf16, mid) + _dot(sel_map_t_bf16, lo)


def _select_blocks(imp, cur, lane, n_sel):
    forced = (lane == 0) | (lane == cur) | (lane == cur - 1)
    allowed = lane <= cur
    score = jnp.where(allowed, jnp.where(forced, 1e30, imp), NEG)
    return allowed & (_rank_before(score, lane, n_sel) < SEL_TOPN)


def _masked_softmax_rows(s, mask):
    s = jnp.where(mask, s, NEG)
    m = jnp.max(s, axis=1, keepdims=True)
    e = jnp.where(mask, jnp.exp(s - m), 0.0)
    return e / jnp.maximum(jnp.sum(e, axis=1, keepdims=True), 1e-30)


def _nsa_prompt_kernel(q_ref, g_ref, ck_ref, cv_ref, sel_ref, win_ref, o_ref,
                       ckb_ref, cvt_ref, ksb_ref, vst_ref, kwb_ref, vwt_ref, *, seq, tq):
    i = pl.program_id(2)
    grp = GROUP_C
    kvh = NSA_KV_HEADS
    nq = grp * tq
    n_cmp = ck_ref.shape[0]
    nc = n_cmp - (CMP_BLOCK // CMP_STRIDE - 1)
    n_sel = seq // SEL_BLOCK

    @pl.when(i == 0)
    def _():
        k = pl.program_id(1)
        stride = 2 * kvh
        ckb_ref[...] = ck_ref[...].astype(BF16)
        cvt_ref[...] = cv_ref[...].T.astype(BF16)
        ksb_ref[...] = sel_ref[pl.ds(k, seq, stride=stride), :].astype(BF16)
        kwb_ref[...] = win_ref[pl.ds(k, seq, stride=stride), :].astype(BF16)
        for c in range(seq // (2 * tq)):
            vst_ref[c] = sel_ref[pl.ds(2 * c * tq * stride + kvh + k, 2 * tq, stride=stride), :].T.astype(BF16)
        for b in range(seq // tq):
            vwt_ref[b] = win_ref[pl.ds(b * tq * stride + kvh + k, tq, stride=stride), :].T.astype(BF16)

    q4 = jnp.concatenate([q_ref[:, g * HEAD_DIM:(g + 1) * HEAD_DIM] for g in range(grp)], axis=0)
    pos = i * tq + lax.broadcasted_iota(jnp.int32, (1, tq), 1)
    pos4 = jnp.concatenate([pos] * grp, axis=1)

    c_row = lax.broadcasted_iota(jnp.int32, (n_cmp, nq), 0)
    avail = (c_row * CMP_STRIDE + (CMP_BLOCK - 1) <= pos4) & (c_row < nc)
    s_c = jnp.where(avail, _dot_t(ckb_ref[...], q4), NEG)
    m_c = jnp.max(s_c, axis=0, keepdims=True)
    e_c = jnp.where(avail, jnp.exp(s_c - m_c), 0.0)
    p_c = e_c / jnp.maximum(jnp.sum(e_c, axis=0, keepdims=True), 1e-30)
    o_c = _dot(cvt_ref[...], p_c.astype(BF16))

    p_sum = p_c[:, 0:tq]
    for g in range(1, grp):
        p_sum = p_sum + p_c[:, g * tq:(g + 1) * tq]
    imp = _importance_t(_sel_map(LANES, n_cmp, transposed=True).astype(BF16), p_sum)[0:n_sel, :]
    j_row = lax.broadcasted_iota(jnp.int32, (n_sel, tq), 0)
    cur = pos // SEL_BLOCK
    forced = (j_row == 0) | (j_row == cur) | (j_row == cur - 1)
    allowed = j_row <= cur
    score = jnp.where(allowed, jnp.where(forced, 1e30, imp), NEG)
    selected = allowed & (_rank_before_t(score, n_sel) < SEL_TOPN)
    sel_bf = jnp.concatenate([jnp.where(selected, 1.0, 0.0), jnp.zeros((LANES - n_sel, tq), F32)],
                             axis=0).astype(BF16)

    tk = 2 * tq
    key = lax.broadcasted_iota(jnp.int32, (tk, tq), 0)
    ek = lax.broadcasted_iota(jnp.int32, (tk, LANES), 0)
    ej = lax.broadcasted_iota(jnp.int32, (tk, LANES), 1)

    def sel_trip(c, carry):
        m, l, acc = carry
        expand = jnp.where(ej == c * (tk // SEL_BLOCK) + ek // SEL_BLOCK, 1.0, 0.0).astype(BF16)
        chosen = _dot(expand, sel_bf) > 0.5
        bias = jnp.where(chosen & (c * tk + key <= pos), 0.0, NEG)
        s = _dot_t(ksb_ref[pl.ds(pl.multiple_of(c * tk, tk), tk), :], q4) + jnp.concatenate([bias] * grp, axis=1)
        m_new = jnp.maximum(m, jnp.max(s, axis=0, keepdims=True))
        p = jnp.exp(s - m_new)
        alpha = jnp.exp(m - m_new)
        l = alpha * l + jnp.sum(p, axis=0, keepdims=True)
        return m_new, l, alpha * acc + _dot(vst_ref[c], p.astype(BF16))

    init = (jnp.full((1, nq), NEG, F32), jnp.zeros((1, nq), F32), jnp.zeros((HEAD_DIM, nq), F32))
    m_s, l_s, acc_s = lax.fori_loop(0, i // 2 + 1, sel_trip, init)
    o_s = acc_s / jnp.maximum(l_s, 1e-30)

    n_wb = WINDOW // tq + 1
    start = jnp.minimum(jnp.maximum(i - (n_wb - 1), 0), seq // tq - n_wb)
    wkey = start * tq + lax.broadcasted_iota(jnp.int32, (n_wb * tq, tq), 0)
    wbias = jnp.where((wkey <= pos) & (wkey > pos - WINDOW), 0.0, NEG)
    s_w = _dot_t(kwb_ref[pl.ds(pl.multiple_of(start * tq, tq), n_wb * tq), :], q4)
    s_w = s_w + jnp.concatenate([wbias] * grp, axis=1)
    p_w = jnp.exp(s_w - jnp.max(s_w, axis=0, keepdims=True))
    vwt = jnp.concatenate([vwt_ref[start + b] for b in range(n_wb)], axis=1)
    o_w = _dot(vwt, p_w.astype(BF16)) / jnp.maximum(jnp.sum(p_w, axis=0, keepdims=True), 1e-30)

    gates = g_ref[...].T
    for g in range(grp):
        cs = slice(g * tq, (g + 1) * tq)
        o = (gates[3 * g:3 * g + 1, :] * o_c[:, cs] + gates[3 * g + 1:3 * g + 2, :] * o_s[:, cs]
             + gates[3 * g + 2:3 * g + 3, :] * o_w[:, cs])
        o_ref[:, g * HEAD_DIM:(g + 1) * HEAD_DIM] = o.T.astype(o_ref.dtype)


def _nsa_prompt(q, gates, ckv, sel, win, batch, seq):
    tq = 256
    nq = seq // tq
    kvh = NSA_KV_HEADS
    gw = GROUP_C * HEAD_DIM
    n_cmp = ckv.shape[2]
    assert seq // SEL_BLOCK <= LANES and n_cmp <= LANES and WINDOW % tq == 0
    assert nq % 2 == 0 and nq >= WINDOW // tq + 1
    full = pl.BlockSpec((seq * 2 * kvh, HEAD_DIM), lambda b, k, i: (b, 0))
    return pl.pallas_call(
        functools.partial(_nsa_prompt_kernel, seq=seq, tq=tq),
        grid=(batch, kvh, nq),
        in_specs=[
            pl.BlockSpec((tq, gw), lambda b, k, i: (b * nq + i, k)),
            pl.BlockSpec((tq, LANES), lambda b, k, i: (b * nq + i, k)),
            pl.BlockSpec((None, None, n_cmp, HEAD_DIM), lambda b, k, i: (b, k, 0, 0)),
            pl.BlockSpec((None, None, n_cmp, HEAD_DIM), lambda b, k, i: (b, kvh + k, 0, 0)),
            full, full,
        ],
        out_specs=pl.BlockSpec((tq, gw), lambda b, k, i: (b * nq + i, k)),
        out_shape=jax.ShapeDtypeStruct((batch * seq, N_HEADS * HEAD_DIM), BF16),
        scratch_shapes=[
            pltpu.VMEM((n_cmp, HEAD_DIM), BF16),
            pltpu.VMEM((HEAD_DIM, n_cmp), BF16),
            pltpu.VMEM((seq, HEAD_DIM), BF16),
            pltpu.VMEM((nq // 2, HEAD_DIM, 2 * tq), BF16),
            pltpu.VMEM((seq, HEAD_DIM), BF16),
            pltpu.VMEM((nq, HEAD_DIM, tq), BF16),
        ],
        compiler_params=_cparams(3),
        name="nsa_prompt",
    )(q, gates, ckv, ckv, sel, win)


def _nsa_sample_pre_kernel(q_ref, ckv_ref, wst_ref, wnew_ref, oc_ref, ow_ref, selm_ref, *, t_new, past_len):
    grp = GROUP_C
    kvh = NSA_KV_HEADS
    n_cmp = ckv_ref.shape[1]
    nc = n_cmp - (CMP_BLOCK // CMP_STRIDE - 1)
    wb = wst_ref.shape[0] // (2 * kvh)
    n_sel = (past_len + t_new + SEL_BLOCK - 1) // SEL_BLOCK
    sel_lanes = selm_ref.shape[1]
    rows = grp * t_new
    rowi = lax.broadcasted_iota(jnp.int32, (rows, 1), 0)
    pos = past_len + rowi % t_new
    pos_t = past_len + lax.broadcasted_iota(jnp.int32, (t_new, 1), 0)
    lane_c = lax.broadcasted_iota(jnp.int32, (rows, n_cmp), 1)
    avail = (lane_c * CMP_STRIDE + (CMP_BLOCK - 1) <= pos) & (lane_c < nc)
    lane_s = lax.broadcasted_iota(jnp.int32, (t_new, sel_lanes), 1)
    smap = _sel_map(n_cmp, sel_lanes).astype(BF16)
    lane_w = lax.broadcasted_iota(jnp.int32, (rows, wb), 1)
    wpos = past_len - wb + lane_w
    mask_w = (wpos <= pos) & (wpos > pos - WINDOW)
    lane_n = lax.broadcasted_iota(jnp.int32, (rows, LANES), 1)
    npos = past_len + lane_n
    mask_n = (npos <= pos) & (npos > pos - WINDOW) & (lane_n < t_new)
    for k in range(kvh):
        q4 = jnp.concatenate(
            [q_ref[:, (k * grp + g) * HEAD_DIM:(k * grp + g + 1) * HEAD_DIM] for g in range(grp)], axis=0).astype(BF16)
        p_c = _masked_softmax_rows(_dot_t(q4, ckv_ref[k].astype(BF16)), avail)
        oc_ref[k * rows:(k + 1) * rows, :] = _dot(p_c.astype(BF16), ckv_ref[kvh + k].astype(BF16))
        p_sum = p_c[0:t_new]
        for g in range(1, grp):
            p_sum = p_sum + p_c[g * t_new:(g + 1) * t_new]
        imp = _importance(p_sum, smap)
        selected = _select_blocks(imp, pos_t // SEL_BLOCK, lane_s, n_sel)
        selm_ref[k * t_new:(k + 1) * t_new, :] = jnp.where(selected, 1.0, 0.0)
        kst = wst_ref[pl.ds(k, wb, stride=2 * kvh), :].astype(BF16)
        vst = wst_ref[pl.ds(kvh + k, wb, stride=2 * kvh), :].astype(BF16)
        kn = _pad_rows(wnew_ref[pl.ds(k, t_new, stride=2 * kvh), :], LANES).astype(BF16)
        vn = _pad_rows(wnew_ref[pl.ds(kvh + k, t_new, stride=2 * kvh), :], LANES).astype(BF16)
        s1 = jnp.where(mask_w, _dot_t(q4, kst), NEG)
        s2 = jnp.where(mask_n, _dot_t(q4, kn), NEG)
        m = jnp.maximum(jnp.max(s1, axis=1, keepdims=True), jnp.max(s2, axis=1, keepdims=True))
        e1 = jnp.where(mask_w, jnp.exp(s1 - m), 0.0)
        e2 = jnp.where(mask_n, jnp.exp(s2 - m), 0.0)
        den = jnp.maximum(jnp.sum(e1, axis=1, keepdims=True) + jnp.sum(e2, axis=1, keepdims=True), 1e-30)
        ow_ref[k * rows:(k + 1) * rows, :] = (_dot(e1.astype(BF16), vst) + _dot(e2.astype(BF16), vn)) / den


def _nsa_sample_pre(q_s, ckv, win_state, layer, win_new, past_len):
    n_seq, t_new, _ = q_s.shape
    n_cmp = ckv.shape[2]
    wrows = win_state.shape[2]
    c4 = 2 * NSA_KV_HEADS * HEAD_DIM
    rows = N_HEADS * t_new
    n_sel = (past_len + t_new + SEL_BLOCK - 1) // SEL_BLOCK
    sel_lanes = -(-n_sel // LANES) * LANES
    return pl.pallas_call(
        functools.partial(_nsa_sample_pre_kernel, t_new=t_new, past_len=past_len),
        grid=(n_seq,),
        in_specs=[
            pl.BlockSpec((None, t_new, N_HEADS * HEAD_DIM), lambda s: (s, 0, 0)),
            pl.BlockSpec((None, 2 * NSA_KV_HEADS, n_cmp, HEAD_DIM), lambda s: (s, 0, 0, 0)),
            pl.BlockSpec((None, None, wrows, HEAD_DIM), lambda s: (layer, s, 0, 0)),
            pl.BlockSpec((None, t_new * c4 // HEAD_DIM, HEAD_DIM), lambda s: (s, 0, 0)),
        ],
        out_specs=[
            pl.BlockSpec((None, rows, HEAD_DIM), lambda s: (s, 0, 0)),
            pl.BlockSpec((None, rows, HEAD_DIM), lambda s: (s, 0, 0)),
            pl.BlockSpec((None, NSA_KV_HEADS * t_new, sel_lanes), lambda s: (s, 0, 0)),
        ],
        out_shape=[
            jax.ShapeDtypeStruct((n_seq, rows, HEAD_DIM), F32),
            jax.ShapeDtypeStruct((n_seq, rows, HEAD_DIM), F32),
            jax.ShapeDtypeStruct((n_seq, NSA_KV_HEADS * t_new, sel_lanes), F32),
        ],
        compiler_params=_cparams(1),
        name="nsa_sample_pre",
    )(q_s, ckv, win_state, win_new)


def _nsa_sample_sel_kernel(pt_ref, q_ref, g_ref, snew_ref, selm_ref, oc_ref, ow_ref, *refs,
                           n_pages, n_step, t_new, past_len):
    del pt_ref
    page_refs = refs[:n_step]
    o_ref, qb_ref, selrows_ref, mx_ref, lx_ref, po_ref = refs[n_step:]
    p = pl.program_id(1)
    grp = GROUP_C
    kvh = NSA_KV_HEADS
    rows = grp * t_new
    sel_lanes = selm_ref.shape[1]
    stride = 2 * kvh
    assert PAGE_SIZE == 2 * SEL_BLOCK

    @pl.when(p == 0)
    def _():
        for k in range(kvh):
            qb_ref[k] = _group_rows(q_ref[...], k * grp, grp).astype(BF16)
            selrows_ref[k] = jnp.concatenate([selm_ref[k * t_new:(k + 1) * t_new, :]] * grp, axis=0)
        mx_ref[...] = jnp.zeros(mx_ref.shape, F32)
        lx_ref[...] = jnp.zeros(lx_ref.shape, F32)

    lane = lax.broadcasted_iota(jnp.int32, (rows, LANES), 1)
    lane_s = lax.broadcasted_iota(jnp.int32, (rows, sel_lanes), 1)

    def sel_col(selrows, block):
        return jnp.sum(jnp.where(lane_s == block, selrows, 0.0), axis=1, keepdims=True)

    n_keys = n_step * PAGE_SIZE
    ej = lax.broadcasted_iota(jnp.int32, (sel_lanes, n_keys), 0)
    ec = lax.broadcasted_iota(jnp.int32, (sel_lanes, n_keys), 1)
    expand = jnp.where(ej == p * (n_keys // SEL_BLOCK) + ec // SEL_BLOCK, 1.0, 0.0).astype(BF16)
    here = lane == p
    for k in range(kvh):
        k_all = jnp.concatenate([pr[pl.ds(k, PAGE_SIZE, stride=stride), :] for pr in page_refs], axis=0)
        v_all = jnp.concatenate([pr[pl.ds(kvh + k, PAGE_SIZE, stride=stride), :] for pr in page_refs], axis=0)
        mask = _dot(selrows_ref[k].astype(BF16), expand) > 0.5
        s = jnp.where(mask, _dot_t(qb_ref[k], k_all.astype(BF16)), NEG)
        m = jnp.max(s, axis=1, keepdims=True)
        e = jnp.where(mask, jnp.exp(s - m), 0.0)
        po_ref[p, k] = _dot(e.astype(BF16), v_all.astype(BF16))
        mx_ref[k] = jnp.where(here, m, mx_ref[k])
        lx_ref[k] = jnp.where(here, jnp.sum(e, axis=1, keepdims=True), lx_ref[k])

    @pl.when(p == n_pages // n_step - 1)
    def _():
        rowi = lax.broadcasted_iota(jnp.int32, (rows, LANES), 0)
        causal = lane <= rowi % t_new
        is_page = lane < n_pages // n_step
        gates = g_ref[...]
        for k in range(kvh):
            selrows = selrows_ref[k]
            mask_n = causal & (sel_col(selrows, past_len // SEL_BLOCK) > 0.5)
            kn = _pad_rows(snew_ref[pl.ds(k, t_new, stride=stride), :], LANES).astype(BF16)
            vn = _pad_rows(snew_ref[pl.ds(kvh + k, t_new, stride=stride), :], LANES).astype(BF16)
            sn = jnp.where(mask_n, _dot_t(qb_ref[k], kn), NEG)
            mx = mx_ref[k]
            m_tot = jnp.maximum(jnp.max(sn, axis=1, keepdims=True),
                                jnp.max(jnp.where(is_page, mx, NEG), axis=1, keepdims=True))
            pn = jnp.where(mask_n, jnp.exp(sn - m_tot), 0.0)
            w = jnp.where(is_page, jnp.exp(mx - m_tot), 0.0)
            l_tot = jnp.sum(pn, axis=1, keepdims=True) + jnp.sum(w * lx_ref[k], axis=1, keepdims=True)
            o_s = _dot(pn.astype(BF16), vn)
            for st in range(n_pages // n_step):
                o_s = o_s + w[:, st:st + 1] * po_ref[st, k]
            o_s = o_s / jnp.maximum(l_tot, 1e-30)
            cols = [jnp.concatenate([gates[:, k * LANES + 3 * g + br:k * LANES + 3 * g + br + 1] for g in range(grp)],
                                    axis=0) for br in range(3)]
            rs = slice(k * rows, (k + 1) * rows)
            o = cols[0] * oc_ref[rs, :] + cols[1] * o_s + cols[2] * ow_ref[rs, :]
            for g in range(grp):
                h = k * grp + g
                o_ref[:, h * HEAD_DIM:(h + 1) * HEAD_DIM] = o[g * t_new:(g + 1) * t_new, :]


def _nsa_sample_sel(q_s, g_s, sel_new, selm, o_c, o_w, cache, layer, page_table, past_len):
    n_seq, t_new, _ = q_s.shape
    n_pages = page_table.shape[1]
    kvh = NSA_KV_HEADS
    c4 = 2 * kvh * HEAD_DIM
    rows = GROUP_C * t_new
    sel_lanes = selm.shape[2]
    n_step = PAGES_PER_STEP
    assert past_len % SEL_BLOCK == 0 and t_new <= SEL_BLOCK and past_len == n_pages * PAGE_SIZE
    assert n_pages <= LANES and n_pages % n_step == 0
    per_seq = lambda shape: pl.BlockSpec((None,) + shape, lambda s, p, pt: (s, 0, 0))
    grid_spec = pltpu.PrefetchScalarGridSpec(
        num_scalar_prefetch=1,
        grid=(n_seq, n_pages // n_step),
        in_specs=[
            per_seq((t_new, N_HEADS * HEAD_DIM)),
            per_seq((t_new, kvh * LANES)),
            per_seq((t_new * c4 // HEAD_DIM, HEAD_DIM)),
            per_seq((kvh * t_new, sel_lanes)),
            per_seq((kvh * rows, HEAD_DIM)),
            per_seq((kvh * rows, HEAD_DIM)),
        ] + _page_specs(n_step, PAGE_SIZE * 2 * kvh, layer),
        out_specs=per_seq((t_new, N_HEADS * HEAD_DIM)),
        scratch_shapes=[
            pltpu.VMEM((kvh, rows, HEAD_DIM), BF16),
            pltpu.VMEM((kvh, rows, sel_lanes), F32),
            pltpu.VMEM((kvh, rows, LANES), F32),
            pltpu.VMEM((kvh, rows, LANES), F32),
            pltpu.VMEM((n_pages // n_step, kvh, rows, HEAD_DIM), F32),
        ],
    )
    return pl.pallas_call(
        functools.partial(_nsa_sample_sel_kernel, n_pages=n_pages, n_step=n_step, t_new=t_new, past_len=past_len),
        grid_spec=grid_spec,
        out_shape=jax.ShapeDtypeStruct((n_seq, t_new, N_HEADS * HEAD_DIM), F32),
        compiler_params=_cparams(2),
        name="nsa_sample_sel",
    )(page_table, q_s, g_s, sel_new, selm, o_c, o_w, *([cache] * n_step))


def _rope_tables(seq, t_new, past_len, sample_rows):
    half = HEAD_DIM // 2
    freq = ROPE_THETA ** (-jnp.arange(half, dtype=F32) / half)

    def tab(pos):
        ang = pos.astype(F32)[:, None] * freq[None, :]
        c, s = jnp.cos(ang), jnp.sin(ang)
        return jnp.concatenate([c, c], axis=1), jnp.concatenate([-s, s], axis=1)

    cp, sp = tab(jnp.arange(seq, dtype=jnp.int32))
    cs, ss = tab(past_len + jnp.arange(t_new, dtype=jnp.int32))
    reps = sample_rows // t_new
    return (jnp.concatenate([cp, jnp.tile(cs, (reps, 1))], axis=0),
            jnp.concatenate([sp, jnp.tile(ss, (reps, 1))], axis=0))


def kernel(x_prompt, x_sample, p_prompt, p_sample, cache_moba_kv, cache_nsa_cmp_kv, cache_nsa_sel_kv, state_nsa_win_kv, state_lru_conv, state_lru_h, page_table, norm_mix, norm_ffn, norm_ple, norm_out, moba_w_qkv, moba_w_o, lru_w_in, lru_conv_w, lru_conv_b, lru_w_a, lru_b_a, lru_w_x, lru_b_x, lru_lambda, lru_w_o, nsa_w_in, nsa_w_o, nsa_cmp_pos_k, nsa_cmp_pos_v, nsa_cmp_k_w1, nsa_cmp_k_w2, nsa_cmp_v_w1, nsa_cmp_v_w2, ffn_w_in, ffn_w_out, ple_w_gate, ple_w_proj):
    batch, seq, d = x_prompt.shape
    n_seq, t_new, _ = x_sample.shape
    depth = norm_mix.shape[0]
    n_pages = page_table.shape[1]
    past_len = n_pages * PAGE_SIZE
    mp = batch * seq
    ms = n_seq * t_new
    assert d == D_MODEL and ms == ROW_TILE and seq % ROW_TILE == 0
    n_prompt_tiles = mp // ROW_TILE
    tiles_per_seq = seq // ROW_TILE
    hq = N_HEADS * HEAD_DIM

    h_parts = (x_prompt.reshape(mp, d), x_sample.reshape(ms, d), 0)
    p_parts = (p_prompt.reshape(depth, mp, -1), p_sample.reshape(depth, ms, -1))
    cos_t, sin_t = _rope_tables(seq, t_new, past_len, ms)
    proj = functools.partial(_norm_proj, cos_t=cos_t, sin_t=sin_t, n_prompt_tiles=n_prompt_tiles,
                             tiles_per_seq=tiles_per_seq)

    native = lambda c: c.reshape(c.shape[:2] + (-1, HEAD_DIM))
    moba_cache, cmp_cache, sel_cache, win_state = map(
        native, (cache_moba_kv, cache_nsa_cmp_kv, cache_nsa_sel_kv, state_nsa_win_kv))

    outs = {k: [] for k in ("moba_p", "moba_s", "cmp_p", "cmp_s", "sel_p", "sel_s", "win_p", "win_s",
                            "conv_p", "conv_s", "hh_p", "hh_s")}
    y = None
    for i in range(depth):
        kind, j = i % N_MIXERS, i // N_MIXERS
        if kind == 0:
            hk = MOBA_KV_HEADS * HEAD_DIM
            segs = [(0, c, c, 512, "rope_query") for c in range(0, hq, 512)]
            segs += [(1, 0, hq, hk, "rope"), (1, hk, hq + hk, hk, "none")]
            (q_p, q_s), (kv_p, kv_s) = proj(h_parts, norm_mix[i], moba_w_qkv[j].astype(BF16), segs=segs,
                                            outs=[(hq, BF16, False), (2 * hk, F32, True)])
            a_p = _moba_prompt(q_p, kv_p, batch, seq)
            a_s = _moba_sample(q_s.astype(F32).reshape(n_seq, t_new, hq), kv_s.reshape(n_seq, -1, HEAD_DIM),
                               moba_cache, j, page_table, past_len)
            w_o = moba_w_o[j]
            outs["moba_p"].append(kv_p.reshape(batch, seq, 2, MOBA_KV_HEADS, HEAD_DIM))
            outs["moba_s"].append(kv_s.reshape(n_seq, t_new, 2, MOBA_KV_HEADS, HEAD_DIM))
        elif kind == 1:
            width = lru_w_in.shape[2] // 2
            segs = [(0, c, c, 512, "gelu") for c in range(0, width, 512)]
            segs += [(1, c, width + c, 512, "none") for c in range(0, width, 512)]
            (gact_p, gact_s), (u_p, u_s) = proj(h_parts, norm_mix[i], lru_w_in[j].astype(BF16), segs=segs,
                                                outs=[(width, F32, False), (width, F32, False)])
            lru_w = (lru_conv_w[j], lru_conv_b[j], lru_w_a[j].astype(BF16), lru_b_a[j], lru_w_x[j].astype(BF16),
                     lru_b_x[j], lru_lambda[j])
            zeros_c = jnp.zeros((batch, SUBLANES, width), F32)
            zeros_h = jnp.zeros((batch, 1, width), F32)
            a_p, hh_p = _lru(u_p, gact_p, 0, batch, seq, ROW_TILE, zeros_c, zeros_h, *lru_w, out_dtype=BF16)
            prev_s = jnp.concatenate([jnp.zeros((n_seq, SUBLANES - (CONV_WIDTH - 1), width), F32),
                                      state_lru_conv[j]], axis=1)
            a_s, hh_s = _lru(u_s, gact_s, 0, n_seq, t_new, t_new, prev_s, state_lru_h[j][:, None, :], *lru_w,
                             out_dtype=F32)
            a_s = a_s.reshape(n_seq, t_new, width)
            w_o = lru_w_o[j]
            up_s = jnp.concatenate([state_lru_conv[j], u_s.reshape(n_seq, t_new, width)], axis=1)
            outs["conv_p"].append(
                jnp.stack([u_p[(b + 1) * seq - (CONV_WIDTH - 1):(b + 1) * seq] for b in range(batch)]))
            outs["conv_s"].append(up_s[:, t_new:])
            outs["hh_p"].append(hh_p.reshape(batch, width))
            outs["hh_s"].append(hh_s.reshape(n_seq, width))
        else:
            hk = NSA_KV_HEADS * HEAD_DIM
            w_in = nsa_w_in[j]
            wg = w_in[:, hq + 6 * hk:].reshape(d, NSA_KV_HEADS, GROUP_C * 3)
            wg = jnp.pad(wg, ((0, 0), (0, 0), (0, LANES - GROUP_C * 3))).reshape(d, NSA_KV_HEADS * LANES)
            w_all = jnp.concatenate([w_in[:, :hq + 6 * hk], wg], axis=1).astype(BF16)
            segs = [(0, c, c, 512, "rope_query") for c in range(0, hq, 512)]
            for br in range(3):
                segs += [(1 + br, 0, hq + 2 * br * hk, hk, "rope"), (1 + br, hk, hq + (2 * br + 1) * hk, hk, "none")]
            segs += [(4, 0, hq + 6 * hk, NSA_KV_HEADS * LANES, "sigmoid")]
            (q_p, q_s), (cmp_p, cmp_s), (sel_p, sel_s), (win_p, win_s), (gates_p, gates_s) = proj(
                h_parts, norm_mix[i], w_all, segs=segs,
                outs=[(hq, BF16, False), (2 * hk, F32, True), (2 * hk, F32, True), (2 * hk, F32, True),
                      (NSA_KV_HEADS * LANES, F32, False)])
            w1k, w1v = nsa_cmp_k_w1[j].astype(BF16), nsa_cmp_v_w1[j].astype(BF16)

            def cat(w1):
                w3 = w1.reshape(CMP_BLOCK, HEAD_DIM, w1.shape[1])
                return jnp.concatenate([w3[:CMP_STRIDE], w3[CMP_STRIDE:]], axis=2)

            pe_k = jnp.broadcast_to(nsa_cmp_pos_k[j].reshape(1, -1), (SUBLANES, CMP_BLOCK * HEAD_DIM))
            pe_v = jnp.broadcast_to(nsa_cmp_pos_v[j].reshape(1, -1), (SUBLANES, CMP_BLOCK * HEAD_DIM))
            w1k_cat, w1v_cat = cat(w1k), cat(w1v)
            fin_w = (pe_k, pe_v, w1k, w1v, nsa_cmp_k_w2[j].astype(BF16), nsa_cmp_v_w2[j].astype(BF16))
            ckv_p = _cmp_finish(_cmp_half(cmp_p, 0, batch, seq, w1k_cat, w1v_cat), batch, seq, *fin_w)
            a_p = _nsa_prompt(q_p, gates_p, ckv_p, sel_p, win_p, batch, seq)
            ckv_s = _cmp_finish(_cmp_half_pages(cmp_cache, j, page_table, w1k_cat, w1v_cat), n_seq, past_len, *fin_w)
            q_s = q_s.astype(F32).reshape(n_seq, t_new, hq)
            o_c, o_w, selm = _nsa_sample_pre(q_s, ckv_s, win_state, j, win_s.reshape(n_seq, -1, HEAD_DIM), past_len)
            a_s = _nsa_sample_sel(q_s, gates_s.reshape(n_seq, t_new, -1), sel_s.reshape(n_seq, -1, HEAD_DIM),
                                  selm, o_c, o_w, sel_cache, j, page_table, past_len)
            w_o = nsa_w_o[j]
            shp_p = (batch, seq, 2, NSA_KV_HEADS, HEAD_DIM)
            shp_s = (n_seq, t_new, 2, NSA_KV_HEADS, HEAD_DIM)
            outs["cmp_p"].append(cmp_p.reshape(shp_p))
            outs["cmp_s"].append(cmp_s.reshape(shp_s))
            outs["sel_p"].append(sel_p.reshape(shp_p))
            outs["sel_s"].append(sel_s.reshape(shp_s))
            wb = min(WINDOW, seq)
            outs["win_p"].append(win_p.reshape(shp_p)[:, seq - wb:])
            outs["win_s"].append(jnp.concatenate([state_nsa_win_kv[j], win_s.reshape(shp_s)], axis=1)[:, t_new:])
        res = _tail((a_p, a_s.reshape(ms, -1)), h_parts, p_parts, i, w_o.astype(BF16), norm_ffn[i],
                    ffn_w_in[i].astype(BF16), ffn_w_out[i].astype(BF16), norm_ple[i], ple_w_gate[i].astype(BF16),
                    ple_w_proj[i].astype(BF16), norm_out, n_prompt_tiles, final=(i == depth - 1))
        h_parts = (res[0], res[0], n_prompt_tiles)
        if i == depth - 1:
            y = res[1:]
    return (y[0].reshape(batch, seq, d), y[1].reshape(n_seq, t_new, d),
            jnp.stack(outs["moba_p"]), jnp.stack(outs["moba_s"]),
            jnp.stack(outs["cmp_p"]), jnp.stack(outs["cmp_s"]),
            jnp.stack(outs["sel_p"]), jnp.stack(outs["sel_s"]),
            jnp.stack(outs["win_p"]), jnp.stack(outs["win_s"]),
            jnp.stack(outs["conv_p"]), jnp.stack(outs["conv_s"]),
            jnp.stack(outs["hh_p"]), jnp.stack(outs["hh_s"]))
```

```python
import functools

import jax
import jax.numpy as jnp
from jax import lax
from jax.experimental import pallas as pl
from jax.experimental.pallas import tpu as pltpu

F32 = jnp.float32
BF16 = jnp.bfloat16

D_MODEL = 1024
N_HEADS = 8
HEAD_DIM = 128
ROPE_THETA = 10000.0
RMS_EPS = 1e-6
N_MIXERS = 3
PAGE_SIZE = 128
MOBA_KV_HEADS = 4
MOBA_BLOCK = 256
MOBA_TOPK = 3
LRU_BLOCK_WIDTH = 256
CONV_WIDTH = 4
LRU_C = 8.0
NSA_KV_HEADS = 2
GROUP_C = N_HEADS // NSA_KV_HEADS
CMP_BLOCK = 32
CMP_STRIDE = 16
SEL_BLOCK = 64
SEL_TOPN = 16
WINDOW = 512

LANES = 128
SUBLANES = 8
ROW_TILE = 256
PROMPT_TILE = 512
VMEM_LIMIT = 56 * 1024 * 1024

NEG = -1e30
SCALE = HEAD_DIM ** -0.5


def _cparams(n_axes):
    return pltpu.CompilerParams(dimension_semantics=("arbitrary",) * n_axes, vmem_limit_bytes=VMEM_LIMIT)


def _whole():
    return pl.BlockSpec(memory_space=pltpu.VMEM)


def _gelu(x):
    return 0.5 * x * (1.0 + jnp.tanh(0.7978845608028654 * (x + 0.044715 * (x * x * x))))


def _sigmoid(x):
    return 1.0 / (1.0 + jnp.exp(-x))


def _dot(a, b):
    return jnp.dot(a, b, preferred_element_type=F32)


def _dot_t(a, b):
    return lax.dot_general(a, b, (((1,), (1,)), ((), ())), preferred_element_type=F32)


def _rms(x, g):
    ms = jnp.mean(x * x, axis=-1, keepdims=True)
    return x * lax.rsqrt(ms + RMS_EPS) * g


def _rank_before(score, lane, n):
    rank = jnp.zeros(score.shape, jnp.int32)
    for jp in range(n):
        col = score[:, jp:jp + 1]
        beats = (col > score) | ((col == score) & (jp < lane))
        rank = rank + beats.astype(jnp.int32)
    return rank


def _rank_before_t(score, n):
    row = lax.broadcasted_iota(jnp.int32, score.shape, 0)
    rank = jnp.zeros(score.shape, jnp.int32)
    for jp in range(n):
        r = score[jp:jp + 1, :]
        rank = rank + ((r > score) | ((r == score) & (jp < row))).astype(jnp.int32)
    return rank


def _proj_kernel(x_ref, g_ref, w_ref, cos_ref, sin_ref, *out_refs, segs, token_rows):
    xn = _rms(x_ref[...], g_ref[...]).astype(BF16)
    for (oi, ocol, wcol, width, kind) in segs:
        z = _dot(xn, w_ref[:, wcol:wcol + width])
        if kind in ("rope", "rope_query"):
            cos = cos_ref[...]
            sin = sin_ref[...]
            if kind == "rope_query":
                cos = cos * SCALE
                sin = sin * SCALE
            parts = []
            for c in range(width // HEAD_DIM):
                zh = z[:, c * HEAD_DIM:(c + 1) * HEAD_DIM]
                parts.append(zh * cos + pltpu.roll(zh, HEAD_DIM // 2, 1) * sin)
            z = parts[0] if len(parts) == 1 else jnp.concatenate(parts, axis=1)
        elif kind == "sigmoid":
            z = _sigmoid(z)
        elif kind == "gelu":
            z = _gelu(z)
        r = token_rows[oi]
        if r:
            for c in range(width // HEAD_DIM):
                out_refs[oi][pl.ds(ocol // HEAD_DIM + c, x_ref.shape[0], stride=r), :] = (
                    z[:, c * HEAD_DIM:(c + 1) * HEAD_DIM])
        else:
            out_refs[oi][:, ocol:ocol + width] = z.astype(out_refs[oi].dtype)


def _norm_proj_rows(x, tile, tab_map, g, w_bf16, cos_t, sin_t, segs, outs):
    n = w_bf16.shape[1]
    n_tiles = x.shape[0] // tile
    token_rows = tuple(nc // HEAD_DIM if native else 0 for nc, _, native in outs)
    blocks = [(tile * r, HEAD_DIM) if r else (tile, nc) for (nc, _, _), r in zip(outs, token_rows)]
    return pl.pallas_call(
        functools.partial(_proj_kernel, segs=tuple(segs), token_rows=token_rows),
        grid=(n_tiles,),
        in_specs=[
            pl.BlockSpec((tile, D_MODEL), lambda i: (i, 0)),
            pl.BlockSpec((1, D_MODEL), lambda i: (0, 0)),
            pl.BlockSpec((D_MODEL, n), lambda i: (0, 0)),
            pl.BlockSpec((tile, HEAD_DIM), tab_map),
            pl.BlockSpec((tile, HEAD_DIM), tab_map),
        ],
        out_specs=[pl.BlockSpec(blk, lambda i: (i, 0)) for blk in blocks],
        out_shape=[jax.ShapeDtypeStruct((n_tiles * blk[0], blk[1]), dt) for blk, (_, dt, _) in zip(blocks, outs)],
        compiler_params=_cparams(1),
        name="norm_proj",
    )(x, g.reshape(1, D_MODEL), w_bf16, cos_t, sin_t)


def _norm_proj(h_parts, g, w_bf16, cos_t, sin_t, segs, outs, seq):
    hp, hs = h_parts
    ms = hs.shape[0]
    assert seq % PROMPT_TILE == 0 and seq % ms == 0
    res_p = _norm_proj_rows(hp, PROMPT_TILE, lambda i: (i % (seq // PROMPT_TILE), 0), g, w_bf16, cos_t, sin_t, segs, outs)
    res_s = _norm_proj_rows(hs, ms, lambda i: (seq // ms, 0), g, w_bf16, cos_t, sin_t, segs, outs)
    return list(zip(res_p, res_s))


def _tail_kernel(a_ref, h_ref, p_ref, wo_ref, gf_ref, wi_ref, wout_ref, gp_ref, wg_ref, wp_ref, gout_ref,
                 h_out_ref, *y_out_ref, ffn, chunk):
    h1 = h_ref[...] + _dot(a_ref[...].astype(BF16), wo_ref[...])
    xn = _rms(h1, gf_ref[...]).astype(BF16)
    acc = jnp.zeros(h1.shape, F32)
    for c in range(ffn // chunk):
        zg = _dot(xn, wi_ref[:, c * chunk:(c + 1) * chunk])
        zu = _dot(xn, wi_ref[:, ffn + c * chunk:ffn + (c + 1) * chunk])
        act = (zg * _sigmoid(zg) * zu).astype(BF16)
        acc = acc + _dot(act, wout_ref[c * chunk:(c + 1) * chunk, :])
    h2 = h1 + acc
    xn2 = _rms(h2, gp_ref[...]).astype(BF16)
    gate = _sigmoid(_dot(xn2, wg_ref[...]))
    h3 = h2 + gate * _dot(p_ref[...].astype(BF16), wp_ref[...])
    h_out_ref[...] = h3
    if y_out_ref:
        y_out_ref[0][...] = _rms(h3, gout_ref[...])


def _tail_rows(a, h, p, layer, tile, w_o, g_ffn, w_in, w_out, g_ple, w_gate, w_proj, g_out, final):
    rows = h.shape[0]
    ffn = w_out.shape[0]
    ple = p.shape[2]
    row = lambda i: (i, 0)
    vec = pl.BlockSpec((1, D_MODEL), lambda i: (0, 0))
    n_out = 2 if final else 1
    return pl.pallas_call(
        functools.partial(_tail_kernel, ffn=ffn, chunk=256),
        grid=(rows // tile,),
        in_specs=[
            pl.BlockSpec((tile, D_MODEL), row),
            pl.BlockSpec((tile, D_MODEL), row),
            pl.BlockSpec((None, tile, ple), lambda i: (layer, i, 0)),
            _whole(), vec, _whole(), _whole(), vec, _whole(), _whole(), vec,
        ],
        out_specs=[pl.BlockSpec((tile, D_MODEL), row)] * n_out,
        out_shape=[jax.ShapeDtypeStruct((rows, D_MODEL), F32)] * n_out,
        compiler_params=_cparams(1),
        name="tail",
    )(a, h, p, w_o, g_ffn.reshape(1, -1), w_in, w_out, g_ple.reshape(1, -1), w_gate, w_proj, g_out.reshape(1, -1))


def _moba_prompt_kernel(q_ref, kv_ref, o_ref, kmean_ref, kb_ref, vt_ref, sel_ref, *, nb):
    i = pl.program_id(2)
    blk = MOBA_BLOCK
    kvh = MOBA_KV_HEADS
    grp = N_HEADS // kvh
    nq = grp * blk
    nb_pad = kmean_ref.shape[0]

    @pl.when(i == 0)
    def _():
        j = pl.program_id(1)
        k = kv_ref[pl.ds(j, nb * blk, stride=2 * kvh), :]
        km = jnp.sum(k.reshape(nb, blk, HEAD_DIM), axis=1) * (1.0 / blk)
        kmean_ref[...] = jnp.concatenate([km, jnp.zeros((nb_pad - nb, HEAD_DIM), F32)], axis=0).astype(BF16)
        kb_ref[...] = k.astype(BF16)
        for c in range(nb // 2):
            v = kv_ref[pl.ds(2 * c * blk * 2 * kvh + kvh + j, 2 * blk, stride=2 * kvh), :]
            vt_ref[c] = v.T.astype(BF16)

    q = jnp.concatenate([q_ref[:, g * HEAD_DIM:(g + 1) * HEAD_DIM] for g in range(grp)], axis=0)
    blk_row = lax.broadcasted_iota(jnp.int32, (nb_pad, nq), 0)
    valid = blk_row < i
    gate = jnp.where(valid, _dot_t(kmean_ref[...], q), NEG)
    sel_ref[...] = jnp.where(valid & (_rank_before_t(gate, nb) < MOBA_TOPK), 1.0, 0.0)

    def past(c, carry):
        m, l, acc = carry
        s = _dot_t(kb_ref[pl.ds(pl.multiple_of(c * 2 * blk, 2 * blk), 2 * blk), :], q)
        halves = []
        m_new = m
        for hh in range(2):
            on = sel_ref[pl.ds(2 * c + hh, 1), :] > 0.5
            sh = s[hh * blk:(hh + 1) * blk, :]
            m_new = jnp.maximum(m_new, jnp.where(on, jnp.max(sh, axis=0, keepdims=True), NEG))
            halves.append((on, sh))
        p = jnp.concatenate([jnp.exp(sh - jnp.where(on, m_new, -NEG)) for on, sh in halves], axis=0)
        alpha = jnp.exp(m - m_new)
        l = alpha * l + jnp.sum(p, axis=0, keepdims=True)
        return m_new, l, alpha * acc + _dot(vt_ref[c], p.astype(BF16))

    init = (jnp.full((1, nq), NEG, F32), jnp.zeros((1, nq), F32), jnp.zeros((HEAD_DIM, nq), F32))
    m, l, acc = lax.fori_loop(0, (i + 1) // 2, past, init)
    key = lax.broadcasted_iota(jnp.int32, (blk, nq), 0)
    qi = lax.broadcasted_iota(jnp.int32, (blk, nq), 1) % blk
    s = jnp.where(key <= qi, _dot_t(kb_ref[pl.ds(pl.multiple_of(i * blk, blk), blk), :], q), NEG)
    m_new = jnp.maximum(m, jnp.max(s, axis=0, keepdims=True))
    p = jnp.exp(s - m_new)
    alpha = jnp.exp(m - m_new)
    l = alpha * l + jnp.sum(p, axis=0, keepdims=True)
    vt_pair = vt_ref[i // 2]
    vt_own = jnp.where(i % 2 == 0, vt_pair[:, :blk], vt_pair[:, blk:])
    acc = alpha * acc + _dot(vt_own, p.astype(BF16))
    o = acc / jnp.maximum(l, 1e-30)
    for g in range(grp):
        o_ref[:, g * HEAD_DIM:(g + 1) * HEAD_DIM] = o[:, g * blk:(g + 1) * blk].T.astype(o_ref.dtype)


def _moba_prompt(q, kv, batch, seq):
    nb = seq // MOBA_BLOCK
    kvh = MOBA_KV_HEADS
    assert nb <= 2 * SUBLANES and nb % 2 == 0
    return pl.pallas_call(
        functools.partial(_moba_prompt_kernel, nb=nb),
        grid=(batch, kvh, nb),
        in_specs=[
            pl.BlockSpec((MOBA_BLOCK, 2 * HEAD_DIM), lambda b, j, i: (b * nb + i, j)),
            pl.BlockSpec((seq * 2 * kvh, HEAD_DIM), lambda b, j, i: (b, 0)),
        ],
        out_specs=pl.BlockSpec((MOBA_BLOCK, 2 * HEAD_DIM), lambda b, j, i: (b * nb + i, j)),
        out_shape=jax.ShapeDtypeStruct((batch * seq, N_HEADS * HEAD_DIM), BF16),
        scratch_shapes=[
            pltpu.VMEM((2 * SUBLANES, HEAD_DIM), BF16),
            pltpu.VMEM((seq, HEAD_DIM), BF16),
            pltpu.VMEM((nb // 2, HEAD_DIM, 2 * MOBA_BLOCK), BF16),
            pltpu.VMEM((2 * SUBLANES, (N_HEADS // kvh) * MOBA_BLOCK), F32),
        ],
        compiler_params=_cparams(3),
        name="moba_prompt",
    )(q, kv)


PAGES_PER_STEP = 16


def _group_rows(q, first_head, group):
    return jnp.concatenate(
        [q[:, (first_head + g) * HEAD_DIM:(first_head + g + 1) * HEAD_DIM] for g in range(group)], axis=0)


def _pad_rows(x, rows):
    return jnp.concatenate([x, jnp.zeros((rows - x.shape[0], x.shape[1]), x.dtype)], axis=0)


def _page_specs(n, rows, layer):
    return [pl.BlockSpec((None, None, rows, HEAD_DIM),
                         functools.partial(lambda s, p, pt, g: (layer, pt[s, n * p + g], 0, 0), g=g))
            for g in range(n)]


def _moba_sample_kernel(pt_ref, q_ref, kvn_ref, *refs, n_pages, n_step, t_new):
    del pt_ref
    page_refs = refs[:n_step]
    o_ref, qf_ref, qb_ref, gx_ref, mx_ref, lx_ref, po_ref = refs[n_step:]
    p = pl.program_id(1)
    kvh = MOBA_KV_HEADS
    group = N_HEADS // kvh
    rows = group * t_new
    stride = 2 * kvh

    @pl.when(p == 0)
    def _():
        for j in range(kvh):
            qj = _group_rows(q_ref[...], j * group, group)
            qf_ref[j] = qj
            qb_ref[j] = qj.astype(BF16)
        gx_ref[...] = jnp.zeros(gx_ref.shape, F32)
        mx_ref[...] = jnp.zeros(mx_ref.shape, F32)
        lx_ref[...] = jnp.zeros(lx_ref.shape, F32)

    lane = lax.broadcasted_iota(jnp.int32, (rows, LANES), 1)
    zero = jnp.zeros((rows, PAGE_SIZE), BF16)
    ks = [[page_refs[g][pl.ds(j, PAGE_SIZE, stride=stride), :] for g in range(n_step)] for j in range(kvh)]
    scores = [_dot_t(qb_ref[j], jnp.concatenate(ks[j], axis=0).astype(BF16)) for j in range(kvh)]
    e_all = []
    for j in range(kvh):
        qf = qf_ref[j]
        gx, mx, lx = gx_ref[j], mx_ref[j], lx_ref[j]
        e_rows = []
        for g in range(n_step):
            gate = jnp.sum(qf * jnp.sum(ks[j][g], axis=0, keepdims=True), axis=1, keepdims=True)
            sg = scores[j][:, g * PAGE_SIZE:(g + 1) * PAGE_SIZE]
            m = jnp.max(sg, axis=1, keepdims=True)
            e = jnp.exp(sg - m)
            e_rows.append(jnp.concatenate([e.astype(BF16) if gg == g else zero for gg in range(n_step)], axis=1))
            here = lane == p * n_step + g
            gx = jnp.where(here, gate, gx)
            mx = jnp.where(here, m, mx)
            lx = jnp.where(here, jnp.sum(e, axis=1, keepdims=True), lx)
        e_all.append(jnp.concatenate(e_rows, axis=0))
        gx_ref[j] = gx
        mx_ref[j] = mx
        lx_ref[j] = lx
    for j in range(kvh):
        v_all = jnp.concatenate([page_refs[g][pl.ds(kvh + j, PAGE_SIZE, stride=stride), :] for g in range(n_step)],
                                axis=0)
        o_all = _dot(e_all[j], v_all.astype(BF16))
        for g in range(n_step):
            po_ref[p * n_step + g, j] = o_all[g * rows:(g + 1) * rows, :]

    @pl.when(p == n_pages // n_step - 1)
    def _():
        n_past = n_pages // 2
        rowi = lax.broadcasted_iota(jnp.int32, (rows, LANES), 0)
        mask_n = lane <= (rowi % t_new)
        first = (lane % 2 == 0) & (lane < n_pages)
        for j in range(kvh):
            g2 = gx_ref[j]
            gate = jnp.where(first, (g2 + pltpu.roll(g2, LANES - 1, 1)) * (1.0 / MOBA_BLOCK), NEG)
            rank = jnp.zeros(gate.shape, jnp.int32)
            for bp in range(n_past):
                col = gate[:, 2 * bp:2 * bp + 1]
                rank = rank + ((col > gate) | ((col == gate) & (2 * bp < lane))).astype(jnp.int32)
            sel_first = jnp.where(first & (rank < MOBA_TOPK), 1.0, 0.0)
            sel = (sel_first + pltpu.roll(sel_first, 1, 1)) > 0.5
            kn = _pad_rows(kvn_ref[pl.ds(j, t_new, stride=stride), :], LANES).astype(BF16)
            vn = _pad_rows(kvn_ref[pl.ds(kvh + j, t_new, stride=stride), :], LANES).astype(BF16)
            sn = jnp.where(mask_n, _dot_t(qb_ref[j], kn), NEG)
            mx = mx_ref[j]
            m_tot = jnp.maximum(jnp.max(sn, axis=1, keepdims=True),
                                jnp.max(jnp.where(sel, mx, NEG), axis=1, keepdims=True))
            pn = jnp.where(mask_n, jnp.exp(sn - m_tot), 0.0)
            w = jnp.where(sel, jnp.exp(mx - m_tot), 0.0)
            l_tot = jnp.sum(pn, axis=1, keepdims=True) + jnp.sum(w * lx_ref[j], axis=1, keepdims=True)
            o = _dot(pn.astype(BF16), vn)
            for pg in range(n_pages):
                o = o + w[:, pg:pg + 1] * po_ref[pg, j]
            o = o / jnp.maximum(l_tot, 1e-30)
            for g in range(group):
                h = j * group + g
                o_ref[:, h * HEAD_DIM:(h + 1) * HEAD_DIM] = o[g * t_new:(g + 1) * t_new, :]


def _moba_sample(q_s, kv_s, cache, layer, page_table, past_len):
    n_seq, t_new, _ = q_s.shape
    n_pages = page_table.shape[1]
    kvh = MOBA_KV_HEADS
    rows = (N_HEADS // kvh) * t_new
    n_step = PAGES_PER_STEP
    assert n_pages <= LANES and n_pages % n_step == 0 and past_len == n_pages * PAGE_SIZE
    assert MOBA_BLOCK == 2 * PAGE_SIZE and t_new <= MOBA_BLOCK
    per_seq = lambda r, w: pl.BlockSpec((None, r, w), lambda s, p, pt: (s, 0, 0))
    grid_spec = pltpu.PrefetchScalarGridSpec(
        num_scalar_prefetch=1,
        grid=(n_seq, n_pages // n_step),
        in_specs=[per_seq(t_new, N_HEADS * HEAD_DIM), per_seq(t_new * 2 * kvh, HEAD_DIM)]
        + _page_specs(n_step, PAGE_SIZE * 2 * kvh, layer),
        out_specs=per_seq(t_new, N_HEADS * HEAD_DIM),
        scratch_shapes=[
            pltpu.VMEM((kvh, rows, HEAD_DIM), F32),
            pltpu.VMEM((kvh, rows, HEAD_DIM), BF16),
            pltpu.VMEM((kvh, rows, LANES), F32),
            pltpu.VMEM((kvh, rows, LANES), F32),
            pltpu.VMEM((kvh, rows, LANES), F32),
            pltpu.VMEM((n_pages, kvh, rows, HEAD_DIM), F32),
        ],
    )
    return pl.pallas_call(
        functools.partial(_moba_sample_kernel, n_pages=n_pages, n_step=n_step, t_new=t_new),
        grid_spec=grid_spec,
        out_shape=jax.ShapeDtypeStruct((n_seq, t_new, N_HEADS * HEAD_DIM), F32),
        compiler_params=_cparams(2),
        name="moba_sample",
    )(page_table, q_s, kv_s, *([cache] * n_step))


def _lru_kernel(u_ref, gact_ref, prev0_ref, h0_ref, cw_ref, cb_ref, wa_ref, ba_ref, wx_ref, bx_ref, lam_ref,
                a_out_ref, hlast_ref, prev_s, h_s, a_s, b_s, y_s, *, tt):
    j = pl.program_id(1)
    width = u_ref.shape[1]

    @pl.when(j == 0)
    def _():
        if tt > SUBLANES:
            prev_s[0:tt - SUBLANES, :] = jnp.zeros((tt - SUBLANES, width), F32)
        prev_s[tt - SUBLANES:tt, :] = prev0_ref[...]
        h_s[...] = h0_ref[...]

    u = u_ref[...]
    prev = prev_s[...]
    rowi = lax.broadcasted_iota(jnp.int32, u.shape, 0)
    xc = cb_ref[...]
    for k in range(CONV_WIDTH):
        sh = CONV_WIDTH - 1 - k
        if sh == 0:
            us = u
        else:
            us = jnp.where(rowi < sh, pltpu.roll(prev, sh, 0), pltpu.roll(u, sh, 0))
        xc = xc + us * cw_ref[k:k + 1, :]
    prev_s[...] = u

    xb = xc.astype(BF16)
    nblk = width // LRU_BLOCK_WIDTH
    ra, rx = [], []
    for n in range(nblk):
        xs = xb[:, n * LRU_BLOCK_WIDTH:(n + 1) * LRU_BLOCK_WIDTH]
        ra.append(_dot(xs, wa_ref[n]))
        rx.append(_dot(xs, wx_ref[n]))
    r = _sigmoid(jnp.concatenate(ra, axis=1) + ba_ref[...])
    ig = _sigmoid(jnp.concatenate(rx, axis=1) + bx_ref[...])
    nl = -lam_ref[...]
    softplus = jnp.maximum(nl, 0.0) + jnp.log(1.0 + jnp.exp(-jnp.abs(nl)))
    log_a = -LRU_C * r * softplus
    a = jnp.exp(log_a)
    a_s[...] = a
    b_s[...] = jnp.sqrt(1.0 - a * a) * ig * xc

    def step(t, h):
        h = a_s[pl.ds(t, 1), :] * h + b_s[pl.ds(t, 1), :]
        y_s[pl.ds(t, 1), :] = h
        return h

    h = lax.fori_loop(0, tt, step, h_s[...], unroll=8)
    h_s[...] = h
    hlast_ref[...] = h
    a_out_ref[...] = (gact_ref[...] * y_s[...]).astype(a_out_ref.dtype)


def _lru(u, gact, row0, n_seq, seq_len, tt, prev0, h0, cw, cb, wa, ba, wx, bx, lam, out_dtype):
    width = u.shape[1]
    tps = seq_len // tt
    base = row0 // tt
    rowmap = lambda s, j: (base + s * tps + j, 0)
    vec = pl.BlockSpec((1, width), lambda s, j: (0, 0))
    return pl.pallas_call(
        functools.partial(_lru_kernel, tt=tt),
        grid=(n_seq, tps),
        in_specs=[
            pl.BlockSpec((tt, width), rowmap),
            pl.BlockSpec((tt, width), rowmap),
            pl.BlockSpec((None, SUBLANES, width), lambda s, j: (s, 0, 0)),
            pl.BlockSpec((None, 1, width), lambda s, j: (s, 0, 0)),
            pl.BlockSpec((CONV_WIDTH, width), lambda s, j: (0, 0)),
            vec, _whole(), vec, _whole(), vec, vec,
        ],
        out_specs=[
            pl.BlockSpec((tt, width), lambda s, j: (s * tps + j, 0)),
            pl.BlockSpec((None, 1, width), lambda s, j: (s, 0, 0)),
        ],
        out_shape=[
            jax.ShapeDtypeStruct((n_seq * seq_len, width), out_dtype),
            jax.ShapeDtypeStruct((n_seq, 1, width), F32),
        ],
        scratch_shapes=[
            pltpu.VMEM((tt, width), F32),
            pltpu.VMEM((1, width), F32),
            pltpu.VMEM((tt, width), F32),
            pltpu.VMEM((tt, width), F32),
            pltpu.VMEM((tt, width), F32),
        ],
        compiler_params=_cparams(2),
        name="rglru",
    )(u, gact, prev0, h0, cw, cb.reshape(1, -1), wa, ba.reshape(1, -1), wx, bx.reshape(1, -1), lam.reshape(1, -1))


CMP_PAGES_PER_STEP = 16


def _cmp_half_pages_kernel(pt_ref, *refs, n_step):
    del pt_ref
    page_refs, (wk_ref, wv_ref, o_ref) = refs[:n_step], refs[n_step:]
    c4 = 2 * NSA_KV_HEADS
    halves = PAGE_SIZE // CMP_STRIDE
    for c in range(c4):
        w_ref = wk_ref if c < NSA_KV_HEADS else wv_ref
        acc = jnp.zeros((n_step * halves, 2 * HEAD_DIM), F32)
        for l in range(CMP_STRIDE):
            x = jnp.concatenate([pr[pl.ds(l * c4 + c, halves, stride=CMP_STRIDE * c4), :] for pr in page_refs], axis=0)
            acc = acc + _dot(x.astype(BF16), w_ref[l])
        o_ref[:, c * 2 * HEAD_DIM:(c + 1) * 2 * HEAD_DIM] = acc


def _cmp_half_pages(cache, layer, page_table, w1k_cat, w1v_cat):
    n_seq, n_pages = page_table.shape
    n_step = CMP_PAGES_PER_STEP
    c4 = 2 * NSA_KV_HEADS
    halves = PAGE_SIZE // CMP_STRIDE
    assert n_pages % n_step == 0 and PAGE_SIZE % CMP_STRIDE == 0 and halves == SUBLANES
    steps = n_pages // n_step
    grid_spec = pltpu.PrefetchScalarGridSpec(
        num_scalar_prefetch=1,
        grid=(n_seq, steps),
        in_specs=_page_specs(n_step, PAGE_SIZE * c4, layer) + [_whole(), _whole()],
        out_specs=pl.BlockSpec((n_step * halves, c4 * 2 * HEAD_DIM), lambda s, p, pt: (s * steps + p, 0)),
    )
    return pl.pallas_call(
        functools.partial(_cmp_half_pages_kernel, n_step=n_step),
        grid_spec=grid_spec,
        out_shape=jax.ShapeDtypeStruct((n_seq * n_pages * halves, c4 * 2 * HEAD_DIM), F32),
        compiler_params=_cparams(2),
        name="nsa_cmp_half_pages",
    )(page_table, *([cache] * n_step), w1k_cat, w1v_cat)


def _cmp_half_kernel(x_ref, wk_ref, wv_ref, o_ref, *, n_half):
    c4 = 2 * NSA_KV_HEADS
    for c in range(c4):
        w_ref = wk_ref if c < NSA_KV_HEADS else wv_ref
        acc = jnp.zeros((n_half, 2 * HEAD_DIM), F32)
        for l in range(CMP_STRIDE):
            x = x_ref[pl.ds(l * c4 + c, n_half, stride=CMP_STRIDE * c4), :]
            acc = acc + _dot(x.astype(BF16), w_ref[l])
        o_ref[:, c * 2 * HEAD_DIM:(c + 1) * 2 * HEAD_DIM] = acc


def _cmp_finish_kernel(ab_ref, pek_ref, pev_ref, w1k_ref, w1v_ref, w2k_ref, w2v_ref, o_ref, *, n_half):
    nc = n_half - (CMP_BLOCK // CMP_STRIDE - 1)
    rowi = lax.broadcasted_iota(jnp.int32, (n_half, HEAD_DIM), 0)
    for c in range(2 * NSA_KV_HEADS):
        is_k = c < NSA_KV_HEADS
        pe = (pek_ref if is_k else pev_ref)[...]
        w1 = (w1k_ref if is_k else w1v_ref)[...]
        w2 = (w2k_ref if is_k else w2v_ref)[...]
        bias = _dot(pe.astype(BF16), w1)[0:1, :]
        a = ab_ref[:, c * 2 * HEAD_DIM:c * 2 * HEAD_DIM + HEAD_DIM]
        b = ab_ref[:, c * 2 * HEAD_DIM + HEAD_DIM:(c + 1) * 2 * HEAD_DIM]
        hid = a + pltpu.roll(b, n_half - 1, 0) + bias
        tok = _dot(_gelu(hid).astype(BF16), w2)
        o_ref[c] = jnp.where(rowi < nc, tok, 0.0)


def _cmp_half(x, row0, n_seq, seq_len, w1k_cat, w1v_cat):
    rows_per_step = 2048
    assert CMP_BLOCK == 2 * CMP_STRIDE and seq_len % rows_per_step == 0 and row0 % rows_per_step == 0
    n_half_step = rows_per_step // CMP_STRIDE
    n_steps = n_seq * seq_len // rows_per_step
    c4 = 2 * NSA_KV_HEADS
    return pl.pallas_call(
        functools.partial(_cmp_half_kernel, n_half=n_half_step),
        grid=(n_steps,),
        in_specs=[pl.BlockSpec((rows_per_step * c4, HEAD_DIM), lambda i: (row0 // rows_per_step + i, 0)),
                  _whole(), _whole()],
        out_specs=pl.BlockSpec((n_half_step, c4 * 2 * HEAD_DIM), lambda i: (i, 0)),
        out_shape=jax.ShapeDtypeStruct((n_steps * n_half_step, c4 * 2 * HEAD_DIM), F32),
        compiler_params=_cparams(1),
        name="nsa_cmp_half",
    )(x, w1k_cat, w1v_cat)


def _cmp_finish(ab, n_seq, seq_len, pe_k, pe_v, w1k, w1v, w2k, w2v):
    n_half = seq_len // CMP_STRIDE
    c4 = 2 * NSA_KV_HEADS
    return pl.pallas_call(
        functools.partial(_cmp_finish_kernel, n_half=n_half),
        grid=(n_seq,),
        in_specs=[pl.BlockSpec((n_half, c4 * 2 * HEAD_DIM), lambda s: (s, 0)),
                  _whole(), _whole(), _whole(), _whole(), _whole(), _whole()],
        out_specs=pl.BlockSpec((None, c4, n_half, HEAD_DIM), lambda s: (s, 0, 0, 0)),
        out_shape=jax.ShapeDtypeStruct((n_seq, c4, n_half, HEAD_DIM), F32),
        compiler_params=_cparams(1),
        name="nsa_cmp_finish",
    )(ab, pe_k, pe_v, w1k, w1v, w2k, w2v)


def _sel_map(n_rows, n_cols, transposed=False):
    r = lax.broadcasted_iota(jnp.int32, (n_rows, n_cols), 0)
    k = lax.broadcasted_iota(jnp.int32, (n_rows, n_cols), 1)
    c, j = (k, r) if transposed else (r, k)
    d = c - j * (SEL_BLOCK // CMP_STRIDE)
    w = jnp.zeros((n_rows, n_cols), F32)
    for m in range(SEL_BLOCK // CMP_STRIDE):
        for n in range(CMP_BLOCK // CMP_STRIDE):
            w = w + jnp.where(d == m - n, 1.0, 0.0)
    return w


def _split3(x):
    hi = x.astype(BF16)
    r1 = x - hi.astype(F32)
    mid = r1.astype(BF16)
    return hi, mid, (r1 - mid.astype(F32)).astype(BF16)


def _importance(p_sum, sel_map_bf16):
    hi, mid, lo = _split3(p_sum)
    return _dot(hi, sel_map_bf16) + _dot(mid, sel_map_bf16) + _dot(lo, sel_map_bf16)


def _importance_t(sel_map_t_bf16, p_sum_t):
    hi, mid, lo = _split3(p_sum_t)
    return _dot(sel_map_t_bf16, hi) + _dot(sel_map_t_bf16, mid) + _dot(sel_map_t_bf16, lo)


def _select_blocks(imp, cur, lane, n_sel):
    forced = (lane == 0) | (lane == cur) | (lane == cur - 1)
    allowed = lane <= cur
    score = jnp.where(allowed, jnp.where(forced, 1e30, imp), NEG)
    return allowed & (_rank_before(score, lane, n_sel) < SEL_TOPN)


def _masked_softmax_rows(s, mask):
    s = jnp.where(mask, s, NEG)
    m = jnp.max(s, axis=1, keepdims=True)
    e = jnp.where(mask, jnp.exp(s - m), 0.0)
    return e / jnp.maximum(jnp.sum(e, axis=1, keepdims=True), 1e-30)


def _nsa_prompt_kernel(q_ref, g_ref, ck_ref, cv_ref, sel_ref, win_ref, o_ref,
                       ckb_ref, cvt_ref, ksb_ref, vst_ref, kwb_ref, vwt_ref, *, seq, tq):
    i = pl.program_id(2)
    grp = GROUP_C
    kvh = NSA_KV_HEADS
    nq = grp * tq
    n_cmp = ck_ref.shape[0]
    nc = n_cmp - (CMP_BLOCK // CMP_STRIDE - 1)
    n_sel = seq // SEL_BLOCK

    @pl.when(i == 0)
    def _():
        k = pl.program_id(1)
        stride = 2 * kvh
        ckb_ref[...] = ck_ref[...].astype(BF16)
        cvt_ref[...] = cv_ref[...].T.astype(BF16)
        ksb_ref[...] = sel_ref[pl.ds(k, seq, stride=stride), :].astype(BF16)
        kwb_ref[...] = win_ref[pl.ds(k, seq, stride=stride), :].astype(BF16)
        for c in range(seq // (2 * tq)):
            vst_ref[c] = sel_ref[pl.ds(2 * c * tq * stride + kvh + k, 2 * tq, stride=stride), :].T.astype(BF16)
        for b in range(seq // tq):
            vwt_ref[b] = win_ref[pl.ds(b * tq * stride + kvh + k, tq, stride=stride), :].T.astype(BF16)

    q4 = jnp.concatenate([q_ref[:, g * HEAD_DIM:(g + 1) * HEAD_DIM] for g in range(grp)], axis=0)
    pos = i * tq + lax.broadcasted_iota(jnp.int32, (1, tq), 1)
    pos4 = jnp.concatenate([pos] * grp, axis=1)

    c_row = lax.broadcasted_iota(jnp.int32, (n_cmp, nq), 0)
    avail = (c_row * CMP_STRIDE + (CMP_BLOCK - 1) <= pos4) & (c_row < nc)
    s_c = jnp.where(avail, _dot_t(ckb_ref[...], q4), NEG)
    m_c = jnp.max(s_c, axis=0, keepdims=True)
    e_c = jnp.where(avail, jnp.exp(s_c - m_c), 0.0)
    p_c = e_c / jnp.maximum(jnp.sum(e_c, axis=0, keepdims=True), 1e-30)
    o_c = _dot(cvt_ref[...], p_c.astype(BF16))

    p_sum = p_c[:, 0:tq]
    for g in range(1, grp):
        p_sum = p_sum + p_c[:, g * tq:(g + 1) * tq]
    imp = _importance_t(_sel_map(LANES, n_cmp, transposed=True).astype(BF16), p_sum)[0:n_sel, :]
    j_row = lax.broadcasted_iota(jnp.int32, (n_sel, tq), 0)
    cur = pos // SEL_BLOCK
    forced = (j_row == 0) | (j_row == cur) | (j_row == cur - 1)
    allowed = j_row <= cur
    score = jnp.where(allowed, jnp.where(forced, 1e30, imp), NEG)
    selected = allowed & (_rank_before_t(score, n_sel) < SEL_TOPN)
    sel_bf = jnp.concatenate([jnp.where(selected, 1.0, 0.0), jnp.zeros((LANES - n_sel, tq), F32)],
                             axis=0).astype(BF16)

    tk = 2 * tq
    key = lax.broadcasted_iota(jnp.int32, (tk, tq), 0)
    ek = lax.broadcasted_iota(jnp.int32, (tk, LANES), 0)
    ej = lax.broadcasted_iota(jnp.int32, (tk, LANES), 1)

    def sel_trip(c, carry):
        m, l, acc = carry
        expand = jnp.where(ej == c * (tk // SEL_BLOCK) + ek // SEL_BLOCK, 1.0, 0.0).astype(BF16)
        chosen = _dot(expand, sel_bf) > 0.5
        bias = jnp.where(chosen & (c * tk + key <= pos), 0.0, NEG)
        s = _dot_t(ksb_ref[pl.ds(pl.multiple_of(c * tk, tk), tk), :], q4) + jnp.concatenate([bias] * grp, axis=1)
        m_new = jnp.maximum(m, jnp.max(s, axis=0, keepdims=True))
        p = jnp.exp(s - m_new)
        alpha = jnp.exp(m - m_new)
        l = alpha * l + jnp.sum(p, axis=0, keepdims=True)
        return m_new, l, alpha * acc + _dot(vst_ref[c], p.astype(BF16))

    init = (jnp.full((1, nq), NEG, F32), jnp.zeros((1, nq), F32), jnp.zeros((HEAD_DIM, nq), F32))
    m_s, l_s, acc_s = lax.fori_loop(0, i // 2 + 1, sel_trip, init)
    o_s = acc_s / jnp.maximum(l_s, 1e-30)

    n_wb = WINDOW // tq + 1
    start = jnp.minimum(jnp.maximum(i - (n_wb - 1), 0), seq // tq - n_wb)
    wkey = start * tq + lax.broadcasted_iota(jnp.int32, (n_wb * tq, tq), 0)
    wbias = jnp.where((wkey <= pos) & (wkey > pos - WINDOW), 0.0, NEG)
    s_w = _dot_t(kwb_ref[pl.ds(pl.multiple_of(start * tq, tq), n_wb * tq), :], q4)
    s_w = s_w + jnp.concatenate([wbias] * grp, axis=1)
    p_w = jnp.exp(s_w - jnp.max(s_w, axis=0, keepdims=True))
    vwt = jnp.concatenate([vwt_ref[start + b] for b in range(n_wb)], axis=1)
    o_w = _dot(vwt, p_w.astype(BF16)) / jnp.maximum(jnp.sum(p_w, axis=0, keepdims=True), 1e-30)

    gates = g_ref[...].T
    for g in range(grp):
        cs = slice(g * tq, (g + 1) * tq)
        o = (gates[3 * g:3 * g + 1, :] * o_c[:, cs] + gates[3 * g + 1:3 * g + 2, :] * o_s[:, cs]
             + gates[3 * g + 2:3 * g + 3, :] * o_w[:, cs])
        o_ref[:, g * HEAD_DIM:(g + 1) * HEAD_DIM] = o.T.astype(o_ref.dtype)


def _nsa_prompt(q, gates, ckv, sel, win, batch, seq):
    tq = 256
    nq = seq // tq
    kvh = NSA_KV_HEADS
    gw = GROUP_C * HEAD_DIM
    n_cmp = ckv.shape[2]
    assert seq // SEL_BLOCK <= LANES and n_cmp <= LANES and WINDOW % tq == 0
    assert nq % 2 == 0 and nq >= WINDOW // tq + 1
    full = pl.BlockSpec((seq * 2 * kvh, HEAD_DIM), lambda b, k, i: (b, 0))
    return pl.pallas_call(
        functools.partial(_nsa_prompt_kernel, seq=seq, tq=tq),
        grid=(batch, kvh, nq),
        in_specs=[
            pl.BlockSpec((tq, gw), lambda b, k, i: (b * nq + i, k)),
            pl.BlockSpec((tq, LANES), lambda b, k, i: (b * nq + i, k)),
            pl.BlockSpec((None, None, n_cmp, HEAD_DIM), lambda b, k, i: (b, k, 0, 0)),
            pl.BlockSpec((None, None, n_cmp, HEAD_DIM), lambda b, k, i: (b, kvh + k, 0, 0)),
            full, full,
        ],
        out_specs=pl.BlockSpec((tq, gw), lambda b, k, i: (b * nq + i, k)),
        out_shape=jax.ShapeDtypeStruct((batch * seq, N_HEADS * HEAD_DIM), BF16),
        scratch_shapes=[
            pltpu.VMEM((n_cmp, HEAD_DIM), BF16),
            pltpu.VMEM((HEAD_DIM, n_cmp), BF16),
            pltpu.VMEM((seq, HEAD_DIM), BF16),
            pltpu.VMEM((nq // 2, HEAD_DIM, 2 * tq), BF16),
            pltpu.VMEM((seq, HEAD_DIM), BF16),
            pltpu.VMEM((nq, HEAD_DIM, tq), BF16),
        ],
        compiler_params=_cparams(3),
        name="nsa_prompt",
    )(q, gates, ckv, ckv, sel, win)


def _nsa_sample_pre_kernel(q_ref, ckv_ref, wst_ref, wnew_ref, oc_ref, ow_ref, selm_ref, *, t_new, past_len):
    grp = GROUP_C
    kvh = NSA_KV_HEADS
    n_cmp = ckv_ref.shape[1]
    nc = n_cmp - (CMP_BLOCK // CMP_STRIDE - 1)
    wb = wst_ref.shape[0] // (2 * kvh)
    n_sel = (past_len + t_new + SEL_BLOCK - 1) // SEL_BLOCK
    sel_lanes = selm_ref.shape[1]
    rows = grp * t_new
    rowi = lax.broadcasted_iota(jnp.int32, (rows, 1), 0)
    pos = past_len + rowi % t_new
    pos_t = past_len + lax.broadcasted_iota(jnp.int32, (t_new, 1), 0)
    lane_c = lax.broadcasted_iota(jnp.int32, (rows, n_cmp), 1)
    avail = (lane_c * CMP_STRIDE + (CMP_BLOCK - 1) <= pos) & (lane_c < nc)
    lane_s = lax.broadcasted_iota(jnp.int32, (t_new, sel_lanes), 1)
    smap = _sel_map(n_cmp, sel_lanes).astype(BF16)
    lane_w = lax.broadcasted_iota(jnp.int32, (rows, wb), 1)
    wpos = past_len - wb + lane_w
    mask_w = (wpos <= pos) & (wpos > pos - WINDOW)
    lane_n = lax.broadcasted_iota(jnp.int32, (rows, LANES), 1)
    npos = past_len + lane_n
    mask_n = (npos <= pos) & (npos > pos - WINDOW) & (lane_n < t_new)
    for k in range(kvh):
        q4 = jnp.concatenate(
            [q_ref[:, (k * grp + g) * HEAD_DIM:(k * grp + g + 1) * HEAD_DIM] for g in range(grp)], axis=0).astype(BF16)
        p_c = _masked_softmax_rows(_dot_t(q4, ckv_ref[k].astype(BF16)), avail)
        oc_ref[k * rows:(k + 1) * rows, :] = _dot(p_c.astype(BF16), ckv_ref[kvh + k].astype(BF16))
        p_sum = p_c[0:t_new]
        for g in range(1, grp):
            p_sum = p_sum + p_c[g * t_new:(g + 1) * t_new]
        imp = _importance(p_sum, smap)
        selected = _select_blocks(imp, pos_t // SEL_BLOCK, lane_s, n_sel)
        selm_ref[k * t_new:(k + 1) * t_new, :] = jnp.where(selected, 1.0, 0.0)
        kst = wst_ref[pl.ds(k, wb, stride=2 * kvh), :].astype(BF16)
        vst = wst_ref[pl.ds(kvh + k, wb, stride=2 * kvh), :].astype(BF16)
        kn = _pad_rows(wnew_ref[pl.ds(k, t_new, stride=2 * kvh), :], LANES).astype(BF16)
        vn = _pad_rows(wnew_ref[pl.ds(kvh + k, t_new, stride=2 * kvh), :], LANES).astype(BF16)
        s1 = jnp.where(mask_w, _dot_t(q4, kst), NEG)
        s2 = jnp.where(mask_n, _dot_t(q4, kn), NEG)
        m = jnp.maximum(jnp.max(s1, axis=1, keepdims=True), jnp.max(s2, axis=1, keepdims=True))
        e1 = jnp.where(mask_w, jnp.exp(s1 - m), 0.0)
        e2 = jnp.where(mask_n, jnp.exp(s2 - m), 0.0)
        den = jnp.maximum(jnp.sum(e1, axis=1, keepdims=True) + jnp.sum(e2, axis=1, keepdims=True), 1e-30)
        ow_ref[k * rows:(k + 1) * rows, :] = (_dot(e1.astype(BF16), vst) + _dot(e2.astype(BF16), vn)) / den


def _nsa_sample_pre(q_s, ckv, win_state, layer, win_new, past_len):
    n_seq, t_new, _ = q_s.shape
    n_cmp = ckv.shape[2]
    wrows = win_state.shape[2]
    c4 = 2 * NSA_KV_HEADS * HEAD_DIM
    rows = N_HEADS * t_new
    n_sel = (past_len + t_new + SEL_BLOCK - 1) // SEL_BLOCK
    sel_lanes = -(-n_sel // LANES) * LANES
    return pl.pallas_call(
        functools.partial(_nsa_sample_pre_kernel, t_new=t_new, past_len=past_len),
        grid=(n_seq,),
        in_specs=[
            pl.BlockSpec((None, t_new, N_HEADS * HEAD_DIM), lambda s: (s, 0, 0)),
            pl.BlockSpec((None, 2 * NSA_KV_HEADS, n_cmp, HEAD_DIM), lambda s: (s, 0, 0, 0)),
            pl.BlockSpec((None, None, wrows, HEAD_DIM), lambda s: (layer, s, 0, 0)),
            pl.BlockSpec((None, t_new * c4 // HEAD_DIM, HEAD_DIM), lambda s: (s, 0, 0)),
        ],
        out_specs=[
            pl.BlockSpec((None, rows, HEAD_DIM), lambda s: (s, 0, 0)),
            pl.BlockSpec((None, rows, HEAD_DIM), lambda s: (s, 0, 0)),
            pl.BlockSpec((None, NSA_KV_HEADS * t_new, sel_lanes), lambda s: (s, 0, 0)),
        ],
        out_shape=[
            jax.ShapeDtypeStruct((n_seq, rows, HEAD_DIM), F32),
            jax.ShapeDtypeStruct((n_seq, rows, HEAD_DIM), F32),
            jax.ShapeDtypeStruct((n_seq, NSA_KV_HEADS * t_new, sel_lanes), F32),
        ],
        compiler_params=_cparams(1),
        name="nsa_sample_pre",
    )(q_s, ckv, win_state, win_new)


def _nsa_sample_sel_kernel(pt_ref, q_ref, g_ref, snew_ref, selm_ref, oc_ref, ow_ref, *refs,
                           n_pages, n_step, t_new, past_len):
    del pt_ref
    page_refs = refs[:n_step]
    o_ref, qb_ref, selrows_ref, mx_ref, lx_ref, po_ref = refs[n_step:]
    p = pl.program_id(1)
    grp = GROUP_C
    kvh = NSA_KV_HEADS
    rows = grp * t_new
    sel_lanes = selm_ref.shape[1]
    stride = 2 * kvh
    assert PAGE_SIZE == 2 * SEL_BLOCK

    @pl.when(p == 0)
    def _():
        for k in range(kvh):
            qb_ref[k] = _group_rows(q_ref[...], k * grp, grp).astype(BF16)
            selrows_ref[k] = jnp.concatenate([selm_ref[k * t_new:(k + 1) * t_new, :]] * grp, axis=0)
        mx_ref[...] = jnp.zeros(mx_ref.shape, F32)
        lx_ref[...] = jnp.zeros(lx_ref.shape, F32)

    lane = lax.broadcasted_iota(jnp.int32, (rows, LANES), 1)
    lane_s = lax.broadcasted_iota(jnp.int32, (rows, sel_lanes), 1)

    def sel_col(selrows, block):
        return jnp.sum(jnp.where(lane_s == block, selrows, 0.0), axis=1, keepdims=True)

    n_keys = n_step * PAGE_SIZE
    ej = lax.broadcasted_iota(jnp.int32, (sel_lanes, n_keys), 0)
    ec = lax.broadcasted_iota(jnp.int32, (sel_lanes, n_keys), 1)
    expand = jnp.where(ej == p * (n_keys // SEL_BLOCK) + ec // SEL_BLOCK, 1.0, 0.0).astype(BF16)
    here = lane == p
    for k in range(kvh):
        k_all = jnp.concatenate([pr[pl.ds(k, PAGE_SIZE, stride=stride), :] for pr in page_refs], axis=0)
        v_all = jnp.concatenate([pr[pl.ds(kvh + k, PAGE_SIZE, stride=stride), :] for pr in page_refs], axis=0)
        mask = _dot(selrows_ref[k].astype(BF16), expand) > 0.5
        s = jnp.where(mask, _dot_t(qb_ref[k], k_all.astype(BF16)), NEG)
        m = jnp.max(s, axis=1, keepdims=True)
        e = jnp.where(mask, jnp.exp(s - m), 0.0)
        po_ref[p, k] = _dot(e.astype(BF16), v_all.astype(BF16))
        mx_ref[k] = jnp.where(here, m, mx_ref[k])
        lx_ref[k] = jnp.where(here, jnp.sum(e, axis=1, keepdims=True), lx_ref[k])

    @pl.when(p == n_pages // n_step - 1)
    def _():
        rowi = lax.broadcasted_iota(jnp.int32, (rows, LANES), 0)
        causal = lane <= rowi % t_new
        is_page = lane < n_pages // n_step
        gates = g_ref[...]
        for k in range(kvh):
            selrows = selrows_ref[k]
            mask_n = causal & (sel_col(selrows, past_len // SEL_BLOCK) > 0.5)
            kn = _pad_rows(snew_ref[pl.ds(k, t_new, stride=stride), :], LANES).astype(BF16)
            vn = _pad_rows(snew_ref[pl.ds(kvh + k, t_new, stride=stride), :], LANES).astype(BF16)
            sn = jnp.where(mask_n, _dot_t(qb_ref[k], kn), NEG)
            mx = mx_ref[k]
            m_tot = jnp.maximum(jnp.max(sn, axis=1, keepdims=True),
                                jnp.max(jnp.where(is_page, mx, NEG), axis=1, keepdims=True))
            pn = jnp.where(mask_n, jnp.exp(sn - m_tot), 0.0)
            w = jnp.where(is_page, jnp.exp(mx - m_tot), 0.0)
            l_tot = jnp.sum(pn, axis=1, keepdims=True) + jnp.sum(w * lx_ref[k], axis=1, keepdims=True)
            o_s = _dot(pn.astype(BF16), vn)
            for st in range(n_pages // n_step):
                o_s = o_s + w[:, st:st + 1] * po_ref[st, k]
            o_s = o_s / jnp.maximum(l_tot, 1e-30)
            cols = [jnp.concatenate([gates[:, k * LANES + 3 * g + br:k * LANES + 3 * g + br + 1] for g in range(grp)],
                                    axis=0) for br in range(3)]
            rs = slice(k * rows, (k + 1) * rows)
            o = cols[0] * oc_ref[rs, :] + cols[1] * o_s + cols[2] * ow_ref[rs, :]
            for g in range(grp):
                h = k * grp + g
                o_ref[:, h * HEAD_DIM:(h + 1) * HEAD_DIM] = o[g * t_new:(g + 1) * t_new, :]


def _nsa_sample_sel(q_s, g_s, sel_new, selm, o_c, o_w, cache, layer, page_table, past_len):
    n_seq, t_new, _ = q_s.shape
    n_pages = page_table.shape[1]
    kvh = NSA_KV_HEADS
    c4 = 2 * kvh * HEAD_DIM
    rows = GROUP_C * t_new
    sel_lanes = selm.shape[2]
    n_step = PAGES_PER_STEP
    assert past_len % SEL_BLOCK == 0 and t_new <= SEL_BLOCK and past_len == n_pages * PAGE_SIZE
    assert n_pages <= LANES and n_pages % n_step == 0
    per_seq = lambda shape: pl.BlockSpec((None,) + shape, lambda s, p, pt: (s, 0, 0))
    grid_spec = pltpu.PrefetchScalarGridSpec(
        num_scalar_prefetch=1,
        grid=(n_seq, n_pages // n_step),
        in_specs=[
            per_seq((t_new, N_HEADS * HEAD_DIM)),
            per_seq((t_new, kvh * LANES)),
            per_seq((t_new * c4 // HEAD_DIM, HEAD_DIM)),
            per_seq((kvh * t_new, sel_lanes)),
            per_seq((kvh * rows, HEAD_DIM)),
            per_seq((kvh * rows, HEAD_DIM)),
        ] + _page_specs(n_step, PAGE_SIZE * 2 * kvh, layer),
        out_specs=per_seq((t_new, N_HEADS * HEAD_DIM)),
        scratch_shapes=[
            pltpu.VMEM((kvh, rows, HEAD_DIM), BF16),
            pltpu.VMEM((kvh, rows, sel_lanes), F32),
            pltpu.VMEM((kvh, rows, LANES), F32),
            pltpu.VMEM((kvh, rows, LANES), F32),
            pltpu.VMEM((n_pages // n_step, kvh, rows, HEAD_DIM), F32),
        ],
    )
    return pl.pallas_call(
        functools.partial(_nsa_sample_sel_kernel, n_pages=n_pages, n_step=n_step, t_new=t_new, past_len=past_len),
        grid_spec=grid_spec,
        out_shape=jax.ShapeDtypeStruct((n_seq, t_new, N_HEADS * HEAD_DIM), F32),
        compiler_params=_cparams(2),
        name="nsa_sample_sel",
    )(page_table, q_s, g_s, sel_new, selm, o_c, o_w, *([cache] * n_step))


def _rope_tables(seq, t_new, past_len, sample_rows):
    half = HEAD_DIM // 2
    freq = ROPE_THETA ** (-jnp.arange(half, dtype=F32) / half)

    def tab(pos):
        ang = pos.astype(F32)[:, None] * freq[None, :]
        c, s = jnp.cos(ang), jnp.sin(ang)
        return jnp.concatenate([c, c], axis=1), jnp.concatenate([-s, s], axis=1)

    cp, sp = tab(jnp.arange(seq, dtype=jnp.int32))
    cs, ss = tab(past_len + jnp.arange(t_new, dtype=jnp.int32))
    reps = sample_rows // t_new
    return (jnp.concatenate([cp, jnp.tile(cs, (reps, 1))], axis=0),
            jnp.concatenate([sp, jnp.tile(ss, (reps, 1))], axis=0))


def kernel(x_prompt, x_sample, p_prompt, p_sample, cache_moba_kv, cache_nsa_cmp_kv, cache_nsa_sel_kv, state_nsa_win_kv, state_lru_conv, state_lru_h, page_table, norm_mix, norm_ffn, norm_ple, norm_out, moba_w_qkv, moba_w_o, lru_w_in, lru_conv_w, lru_conv_b, lru_w_a, lru_b_a, lru_w_x, lru_b_x, lru_lambda, lru_w_o, nsa_w_in, nsa_w_o, nsa_cmp_pos_k, nsa_cmp_pos_v, nsa_cmp_k_w1, nsa_cmp_k_w2, nsa_cmp_v_w1, nsa_cmp_v_w2, ffn_w_in, ffn_w_out, ple_w_gate, ple_w_proj):
    batch, seq, d = x_prompt.shape
    n_seq, t_new, _ = x_sample.shape
    depth = norm_mix.shape[0]
    n_pages = page_table.shape[1]
    past_len = n_pages * PAGE_SIZE
    mp = batch * seq
    ms = n_seq * t_new
    assert d == D_MODEL and ms == ROW_TILE and seq % ROW_TILE == 0
    hq = N_HEADS * HEAD_DIM

    h_parts = (x_prompt.reshape(mp, d), x_sample.reshape(ms, d))
    p_parts = (p_prompt.reshape(depth, mp, -1), p_sample.reshape(depth, ms, -1))
    cos_t, sin_t = _rope_tables(seq, t_new, past_len, ms)
    proj = functools.partial(_norm_proj, cos_t=cos_t, sin_t=sin_t, seq=seq)

    native = lambda c: c.reshape(c.shape[:2] + (-1, HEAD_DIM))
    moba_cache, cmp_cache, sel_cache, win_state = map(
        native, (cache_moba_kv, cache_nsa_cmp_kv, cache_nsa_sel_kv, state_nsa_win_kv))

    outs = {k: [] for k in ("moba_p", "moba_s", "cmp_p", "cmp_s", "sel_p", "sel_s", "win_p", "win_s",
                            "conv_p", "conv_s", "hh_p", "hh_s")}
    y = None
    for i in range(depth):
        kind, j = i % N_MIXERS, i // N_MIXERS
        if kind == 0:
            hk = MOBA_KV_HEADS * HEAD_DIM
            segs = [(0, c, c, 512, "rope_query") for c in range(0, hq, 512)]
            segs += [(1, 0, hq, hk, "rope"), (1, hk, hq + hk, hk, "none")]
            (q_p, q_s), (kv_p, kv_s) = proj(h_parts, norm_mix[i], moba_w_qkv[j].astype(BF16), segs=segs,
                                            outs=[(hq, BF16, False), (2 * hk, F32, True)])
            a_p = _moba_prompt(q_p, kv_p, batch, seq)
            a_s = _moba_sample(q_s.astype(F32).reshape(n_seq, t_new, hq), kv_s.reshape(n_seq, -1, HEAD_DIM),
                               moba_cache, j, page_table, past_len)
            w_o = moba_w_o[j]
            outs["moba_p"].append(kv_p.reshape(batch, seq, 2, MOBA_KV_HEADS, HEAD_DIM))
            outs["moba_s"].append(kv_s.reshape(n_seq, t_new, 2, MOBA_KV_HEADS, HEAD_DIM))
        elif kind == 1:
            width = lru_w_in.shape[2] // 2
            segs = [(0, c, c, 512, "gelu") for c in range(0, width, 512)]
            segs += [(1, c, width + c, 512, "none") for c in range(0, width, 512)]
            (gact_p, gact_s), (u_p, u_s) = proj(h_parts, norm_mix[i], lru_w_in[j].astype(BF16), segs=segs,
                                                outs=[(width, F32, False), (width, F32, False)])
            lru_w = (lru_conv_w[j], lru_conv_b[j], lru_w_a[j].astype(BF16), lru_b_a[j], lru_w_x[j].astype(BF16),
                     lru_b_x[j], lru_lambda[j])
            zeros_c = jnp.zeros((batch, SUBLANES, width), F32)
            zeros_h = jnp.zeros((batch, 1, width), F32)
            a_p, hh_p = _lru(u_p, gact_p, 0, batch, seq, ROW_TILE, zeros_c, zeros_h, *lru_w, out_dtype=BF16)
            prev_s = jnp.concatenate([jnp.zeros((n_seq, SUBLANES - (CONV_WIDTH - 1), width), F32),
                                      state_lru_conv[j]], axis=1)
            a_s, hh_s = _lru(u_s, gact_s, 0, n_seq, t_new, t_new, prev_s, state_lru_h[j][:, None, :], *lru_w,
                             out_dtype=F32)
            a_s = a_s.reshape(n_seq, t_new, width)
            w_o = lru_w_o[j]
            up_s = jnp.concatenate([state_lru_conv[j], u_s.reshape(n_seq, t_new, width)], axis=1)
            outs["conv_p"].append(
                jnp.stack([u_p[(b + 1) * seq - (CONV_WIDTH - 1):(b + 1) * seq] for b in range(batch)]))
            outs["conv_s"].append(up_s[:, t_new:])
            outs["hh_p"].append(hh_p.reshape(batch, width))
            outs["hh_s"].append(hh_s.reshape(n_seq, width))
        else:
            hk = NSA_KV_HEADS * HEAD_DIM
            w_in = nsa_w_in[j]
            wg = w_in[:, hq + 6 * hk:].reshape(d, NSA_KV_HEADS, GROUP_C * 3)
            wg = jnp.pad(wg, ((0, 0), (0, 0), (0, LANES - GROUP_C * 3))).reshape(d, NSA_KV_HEADS * LANES)
            w_all = jnp.concatenate([w_in[:, :hq + 6 * hk], wg], axis=1).astype(BF16)
            segs = [(0, c, c, 512, "rope_query") for c in range(0, hq, 512)]
            for br in range(3):
                segs += [(1 + br, 0, hq + 2 * br * hk, hk, "rope"), (1 + br, hk, hq + (2 * br + 1) * hk, hk, "none")]
            segs += [(4, 0, hq + 6 * hk, NSA_KV_HEADS * LANES, "sigmoid")]
            (q_p, q_s), (cmp_p, cmp_s), (sel_p, sel_s), (win_p, win_s), (gates_p, gates_s) = proj(
                h_parts, norm_mix[i], w_all, segs=segs,
                outs=[(hq, BF16, False), (2 * hk, F32, True), (2 * hk, F32, True), (2 * hk, F32, True),
                      (NSA_KV_HEADS * LANES, F32, False)])
            w1k, w1v = nsa_cmp_k_w1[j].astype(BF16), nsa_cmp_v_w1[j].astype(BF16)

            def cat(w1):
                w3 = w1.reshape(CMP_BLOCK, HEAD_DIM, w1.shape[1])
                return jnp.concatenate([w3[:CMP_STRIDE], w3[CMP_STRIDE:]], axis=2)

            pe_k = jnp.broadcast_to(nsa_cmp_pos_k[j].reshape(1, -1), (SUBLANES, CMP_BLOCK * HEAD_DIM))
            pe_v = jnp.broadcast_to(nsa_cmp_pos_v[j].reshape(1, -1), (SUBLANES, CMP_BLOCK * HEAD_DIM))
            w1k_cat, w1v_cat = cat(w1k), cat(w1v)
            fin_w = (pe_k, pe_v, w1k, w1v, nsa_cmp_k_w2[j].astype(BF16), nsa_cmp_v_w2[j].astype(BF16))
            ckv_p = _cmp_finish(_cmp_half(cmp_p, 0, batch, seq, w1k_cat, w1v_cat), batch, seq, *fin_w)
            a_p = _nsa_prompt(q_p, gates_p, ckv_p, sel_p, win_p, batch, seq)
            ckv_s = _cmp_finish(_cmp_half_pages(cmp_cache, j, page_table, w1k_cat, w1v_cat), n_seq, past_len, *fin_w)
            q_s = q_s.astype(F32).reshape(n_seq, t_new, hq)
            o_c, o_w, selm = _nsa_sample_pre(q_s, ckv_s, win_state, j, win_s.reshape(n_seq, -1, HEAD_DIM), past_len)
            a_s = _nsa_sample_sel(q_s, gates_s.reshape(n_seq, t_new, -1), sel_s.reshape(n_seq, -1, HEAD_DIM),
                                  selm, o_c, o_w, sel_cache, j, page_table, past_len)
            w_o = nsa_w_o[j]
            shp_p = (batch, seq, 2, NSA_KV_HEADS, HEAD_DIM)
            shp_s = (n_seq, t_new, 2, NSA_KV_HEADS, HEAD_DIM)
            outs["cmp_p"].append(cmp_p.reshape(shp_p))
            outs["cmp_s"].append(cmp_s.reshape(shp_s))
            outs["sel_p"].append(sel_p.reshape(shp_p))
            outs["sel_s"].append(sel_s.reshape(shp_s))
            wb = min(WINDOW, seq)
            outs["win_p"].append(win_p.reshape(shp_p)[:, seq - wb:])
            outs["win_s"].append(jnp.concatenate([state_nsa_win_kv[j], win_s.reshape(shp_s)], axis=1)[:, t_new:])
        tail_w = (w_o.astype(BF16), norm_ffn[i], ffn_w_in[i].astype(BF16), ffn_w_out[i].astype(BF16), norm_ple[i],
                  ple_w_gate[i].astype(BF16), ple_w_proj[i].astype(BF16), norm_out)
        final = i == depth - 1
        res_p = _tail_rows(a_p, h_parts[0], p_parts[0], i, PROMPT_TILE, *tail_w, final=final)
        res_s = _tail_rows(a_s.reshape(ms, -1), h_parts[1], p_parts[1], i, ROW_TILE, *tail_w, final=final)
        h_parts = (res_p[0], res_s[0])
        if final:
            y = (res_p[1], res_s[1])
    return (y[0].reshape(batch, seq, d), y[1].reshape(n_seq, t_new, d),
            jnp.stack(outs["moba_p"]), jnp.stack(outs["moba_s"]),
            jnp.stack(outs["cmp_p"]), jnp.stack(outs["cmp_s"]),
            jnp.stack(outs["sel_p"]), jnp.stack(outs["sel_s"]),
            jnp.stack(outs["win_p"]), jnp.stack(outs["win_s"]),
            jnp.stack(outs["conv_p"]), jnp.stack(outs["conv_s"]),
            jnp.stack(outs["hh_p"]), jnp.stack(outs["hh_s"]))
```

```python
import functools

import jax
import jax.numpy as jnp
from jax import lax
from jax.experimental import pallas as pl
from jax.experimental.pallas import tpu as pltpu

F32 = jnp.float32
BF16 = jnp.bfloat16

D_MODEL = 1024
N_HEADS = 8
HEAD_DIM = 128
ROPE_THETA = 10000.0
RMS_EPS = 1e-6
N_MIXERS = 3
PAGE_SIZE = 128
MOBA_KV_HEADS = 4
MOBA_BLOCK = 256
MOBA_TOPK = 3
LRU_BLOCK_WIDTH = 256
CONV_WIDTH = 4
LRU_C = 8.0
NSA_KV_HEADS = 2
GROUP_C = N_HEADS // NSA_KV_HEADS
CMP_BLOCK = 32
CMP_STRIDE = 16
SEL_BLOCK = 64
SEL_TOPN = 16
WINDOW = 512

LANES = 128
SUBLANES = 8
ROW_TILE = 256
PROMPT_TILE = 512
VMEM_LIMIT = 56 * 1024 * 1024

NEG = -1e30
SCALE = HEAD_DIM ** -0.5


def _cparams(n_axes):
    return pltpu.CompilerParams(dimension_semantics=("arbitrary",) * n_axes, vmem_limit_bytes=VMEM_LIMIT)


def _whole():
    return pl.BlockSpec(memory_space=pltpu.VMEM)


def _gelu(x):
    return 0.5 * x * (1.0 + jnp.tanh(0.7978845608028654 * (x + 0.044715 * (x * x * x))))


def _sigmoid(x):
    return 1.0 / (1.0 + jnp.exp(-x))


def _dot(a, b):
    return jnp.dot(a, b, preferred_element_type=F32)


def _dot_t(a, b):
    return lax.dot_general(a, b, (((1,), (1,)), ((), ())), preferred_element_type=F32)


def _rms(x, g):
    ms = jnp.mean(x * x, axis=-1, keepdims=True)
    return x * lax.rsqrt(ms + RMS_EPS) * g


def _rank_before(score, lane, n):
    rank = jnp.zeros(score.shape, jnp.int32)
    for jp in range(n):
        col = score[:, jp:jp + 1]
        beats = (col > score) | ((col == score) & (jp < lane))
        rank = rank + beats.astype(jnp.int32)
    return rank


def _rank_before_t(score, n):
    row = lax.broadcasted_iota(jnp.int32, score.shape, 0)
    rank = jnp.zeros(score.shape, jnp.int32)
    for jp in range(n):
        r = score[jp:jp + 1, :]
        rank = rank + ((r > score) | ((r == score) & (jp < row))).astype(jnp.int32)
    return rank


def _proj_kernel(x_ref, g_ref, w_ref, cos_ref, sin_ref, *out_refs, segs, token_rows, n_aliased):
    out_refs = out_refs[n_aliased:]
    xn = _rms(x_ref[...], g_ref[...]).astype(BF16)
    for (oi, ocol, wcol, width, kind) in segs:
        z = _dot(xn, w_ref[:, wcol:wcol + width])
        if kind in ("rope", "rope_query"):
            cos = cos_ref[...]
            sin = sin_ref[...]
            if kind == "rope_query":
                cos = cos * SCALE
                sin = sin * SCALE
            parts = []
            for c in range(width // HEAD_DIM):
                zh = z[:, c * HEAD_DIM:(c + 1) * HEAD_DIM]
                parts.append(zh * cos + pltpu.roll(zh, HEAD_DIM // 2, 1) * sin)
            z = parts[0] if len(parts) == 1 else jnp.concatenate(parts, axis=1)
        elif kind == "sigmoid":
            z = _sigmoid(z)
        elif kind == "gelu":
            z = _gelu(z)
        r = token_rows[oi]
        if r:
            for c in range(width // HEAD_DIM):
                out_refs[oi][pl.ds(ocol // HEAD_DIM + c, x_ref.shape[0], stride=r), :] = (
                    z[:, c * HEAD_DIM:(c + 1) * HEAD_DIM])
        else:
            out_refs[oi][:, ocol:ocol + width] = z.astype(out_refs[oi].dtype)


def _norm_proj_rows(x, tile, tab_map, g, w_bf16, cos_t, sin_t, segs, outs, stacked=None):
    n = w_bf16.shape[1]
    n_tiles = x.shape[0] // tile
    token_rows = tuple(nc // HEAD_DIM if native else 0 for nc, _, native in outs)
    blocks = [(tile * r, HEAD_DIM) if r else (tile, nc) for (nc, _, _), r in zip(outs, token_rows)]
    out_specs = [pl.BlockSpec(blk, lambda i: (i, 0)) for blk in blocks]
    out_shape = [jax.ShapeDtypeStruct((n_tiles * blk[0], blk[1]), dt) for blk, (_, dt, _) in zip(blocks, outs)]
    extra_in, extra_specs, aliases = [], [], {}
    if stacked is not None:
        so, slot, n_slots, buf = stacked
        out_specs[so] = pl.BlockSpec(blocks[so], lambda i: (slot * n_tiles + i, 0))
        out_shape[so] = jax.ShapeDtypeStruct((n_slots * n_tiles * blocks[so][0], blocks[so][1]), outs[so][1])
        if buf is not None:
            extra_in, extra_specs, aliases = [buf], [pl.BlockSpec(memory_space=pl.ANY)], {5: so}
    return pl.pallas_call(
        functools.partial(_proj_kernel, segs=tuple(segs), token_rows=token_rows, n_aliased=len(extra_in)),
        grid=(n_tiles,),
        in_specs=[
            pl.BlockSpec((tile, D_MODEL), lambda i: (i, 0)),
            pl.BlockSpec((1, D_MODEL), lambda i: (0, 0)),
            pl.BlockSpec((D_MODEL, n), lambda i: (0, 0)),
            pl.BlockSpec((tile, HEAD_DIM), tab_map),
            pl.BlockSpec((tile, HEAD_DIM), tab_map),
        ] + extra_specs,
        out_specs=out_specs,
        out_shape=out_shape,
        input_output_aliases=aliases,
        compiler_params=_cparams(1),
        name="norm_proj",
    )(x, g.reshape(1, D_MODEL), w_bf16, cos_t, sin_t, *extra_in)


def _norm_proj(h_parts, g, w_bf16, cos_t, sin_t, segs, outs, seq, stacked=None):
    hp, hs = h_parts
    ms = hs.shape[0]
    assert seq % PROMPT_TILE == 0 and seq % ms == 0
    res_p = _norm_proj_rows(hp, PROMPT_TILE, lambda i: (i % (seq // PROMPT_TILE), 0), g, w_bf16, cos_t, sin_t, segs, outs,
                            stacked=stacked)
    res_s = _norm_proj_rows(hs, ms, lambda i: (seq // ms, 0), g, w_bf16, cos_t, sin_t, segs, outs)
    return list(zip(res_p, res_s))


def _tail_kernel(a_ref, h_ref, p_ref, wo_ref, gf_ref, wi_ref, wout_ref, gp_ref, wg_ref, wp_ref, gout_ref,
                 h_out_ref, *y_out_ref, ffn, chunk):
    h1 = h_ref[...] + _dot(a_ref[...].astype(BF16), wo_ref[...])
    xn = _rms(h1, gf_ref[...]).astype(BF16)
    acc = jnp.zeros(h1.shape, F32)
    for c in range(ffn // chunk):
        zg = _dot(xn, wi_ref[:, c * chunk:(c + 1) * chunk])
        zu = _dot(xn, wi_ref[:, ffn + c * chunk:ffn + (c + 1) * chunk])
        act = (zg * _sigmoid(zg) * zu).astype(BF16)
        acc = acc + _dot(act, wout_ref[c * chunk:(c + 1) * chunk, :])
    h2 = h1 + acc
    xn2 = _rms(h2, gp_ref[...]).astype(BF16)
    gate = _sigmoid(_dot(xn2, wg_ref[...]))
    h3 = h2 + gate * _dot(p_ref[...].astype(BF16), wp_ref[...])
    h_out_ref[...] = h3
    if y_out_ref:
        y_out_ref[0][...] = _rms(h3, gout_ref[...])


def _tail_rows(a, h, p, layer, tile, w_o, g_ffn, w_in, w_out, g_ple, w_gate, w_proj, g_out, final):
    rows = h.shape[0]
    ffn = w_out.shape[0]
    ple = p.shape[2]
    row = lambda i: (i, 0)
    vec = pl.BlockSpec((1, D_MODEL), lambda i: (0, 0))
    n_out = 2 if final else 1
    return pl.pallas_call(
        functools.partial(_tail_kernel, ffn=ffn, chunk=256),
        grid=(rows // tile,),
        in_specs=[
            pl.BlockSpec((tile, D_MODEL), row),
            pl.BlockSpec((tile, D_MODEL), row),
            pl.BlockSpec((None, tile, ple), lambda i: (layer, i, 0)),
            _whole(), vec, _whole(), _whole(), vec, _whole(), _whole(), vec,
        ],
        out_specs=[pl.BlockSpec((tile, D_MODEL), row)] * n_out,
        out_shape=[jax.ShapeDtypeStruct((rows, D_MODEL), F32)] * n_out,
        compiler_params=_cparams(1),
        name="tail",
    )(a, h, p, w_o, g_ffn.reshape(1, -1), w_in, w_out, g_ple.reshape(1, -1), w_gate, w_proj, g_out.reshape(1, -1))


def _moba_prompt_kernel(q_ref, kv_ref, o_ref, kmean_ref, kb_ref, vt_ref, sel_ref, *, nb):
    i = pl.program_id(2)
    blk = MOBA_BLOCK
    kvh = MOBA_KV_HEADS
    grp = N_HEADS // kvh
    nq = grp * blk
    nb_pad = kmean_ref.shape[0]

    @pl.when(i == 0)
    def _():
        j = pl.program_id(1)
        k = kv_ref[pl.ds(j, nb * blk, stride=2 * kvh), :]
        km = jnp.sum(k.reshape(nb, blk, HEAD_DIM), axis=1) * (1.0 / blk)
        kmean_ref[...] = jnp.concatenate([km, jnp.zeros((nb_pad - nb, HEAD_DIM), F32)], axis=0).astype(BF16)
        kb_ref[...] = k.astype(BF16)
        for c in range(nb // 2):
            v = kv_ref[pl.ds(2 * c * blk * 2 * kvh + kvh + j, 2 * blk, stride=2 * kvh), :]
            vt_ref[c] = v.T.astype(BF16)

    q = jnp.concatenate([q_ref[:, g * HEAD_DIM:(g + 1) * HEAD_DIM] for g in range(grp)], axis=0)
    blk_row = lax.broadcasted_iota(jnp.int32, (nb_pad, nq), 0)
    valid = blk_row < i
    gate = jnp.where(valid, _dot_t(kmean_ref[...], q), NEG)
    sel_ref[...] = jnp.where(valid & (_rank_before_t(gate, nb) < MOBA_TOPK), 1.0, 0.0)

    def past(c, carry):
        m, l, acc = carry
        s = _dot_t(kb_ref[pl.ds(pl.multiple_of(c * 2 * blk, 2 * blk), 2 * blk), :], q)
        halves = []
        m_new = m
        for hh in range(2):
            on = sel_ref[pl.ds(2 * c + hh, 1), :] > 0.5
            sh = s[hh * blk:(hh + 1) * blk, :]
            m_new = jnp.maximum(m_new, jnp.where(on, jnp.max(sh, axis=0, keepdims=True), NEG))
            halves.append((on, sh))
        p = jnp.concatenate([jnp.exp(sh - jnp.where(on, m_new, -NEG)) for on, sh in halves], axis=0)
        alpha = jnp.exp(m - m_new)
        l = alpha * l + jnp.sum(p, axis=0, keepdims=True)
        return m_new, l, alpha * acc + _dot(vt_ref[c], p.astype(BF16))

    init = (jnp.full((1, nq), NEG, F32), jnp.zeros((1, nq), F32), jnp.zeros((HEAD_DIM, nq), F32))
    m, l, acc = lax.fori_loop(0, (i + 1) // 2, past, init)
    key = lax.broadcasted_iota(jnp.int32, (blk, nq), 0)
    qi = lax.broadcasted_iota(jnp.int32, (blk, nq), 1) % blk
    s = jnp.where(key <= qi, _dot_t(kb_ref[pl.ds(pl.multiple_of(i * blk, blk), blk), :], q), NEG)
    m_new = jnp.maximum(m, jnp.max(s, axis=0, keepdims=True))
    p = jnp.exp(s - m_new)
    alpha = jnp.exp(m - m_new)
    l = alpha * l + jnp.sum(p, axis=0, keepdims=True)
    vt_pair = vt_ref[i // 2]
    vt_own = jnp.where(i % 2 == 0, vt_pair[:, :blk], vt_pair[:, blk:])
    acc = alpha * acc + _dot(vt_own, p.astype(BF16))
    o = acc / jnp.maximum(l, 1e-30)
    for g in range(grp):
        o_ref[:, g * HEAD_DIM:(g + 1) * HEAD_DIM] = o[:, g * blk:(g + 1) * blk].T.astype(o_ref.dtype)


def _moba_prompt(q, kv, layer, batch, seq):
    nb = seq // MOBA_BLOCK
    kvh = MOBA_KV_HEADS
    assert nb <= 2 * SUBLANES and nb % 2 == 0
    return pl.pallas_call(
        functools.partial(_moba_prompt_kernel, nb=nb),
        grid=(batch, kvh, nb),
        in_specs=[
            pl.BlockSpec((MOBA_BLOCK, 2 * HEAD_DIM), lambda b, j, i: (b * nb + i, j)),
            pl.BlockSpec((seq * 2 * kvh, HEAD_DIM), lambda b, j, i: (layer * batch + b, 0)),
        ],
        out_specs=pl.BlockSpec((MOBA_BLOCK, 2 * HEAD_DIM), lambda b, j, i: (b * nb + i, j)),
        out_shape=jax.ShapeDtypeStruct((batch * seq, N_HEADS * HEAD_DIM), BF16),
        scratch_shapes=[
            pltpu.VMEM((2 * SUBLANES, HEAD_DIM), BF16),
            pltpu.VMEM((seq, HEAD_DIM), BF16),
            pltpu.VMEM((nb // 2, HEAD_DIM, 2 * MOBA_BLOCK), BF16),
            pltpu.VMEM((2 * SUBLANES, (N_HEADS // kvh) * MOBA_BLOCK), F32),
        ],
        compiler_params=_cparams(3),
        name="moba_prompt",
    )(q, kv)


PAGES_PER_STEP = 16


def _group_rows(q, first_head, group):
    return jnp.concatenate(
        [q[:, (first_head + g) * HEAD_DIM:(first_head + g + 1) * HEAD_DIM] for g in range(group)], axis=0)


def _pad_rows(x, rows):
    return jnp.concatenate([x, jnp.zeros((rows - x.shape[0], x.shape[1]), x.dtype)], axis=0)


def _page_specs(n, rows, layer):
    return [pl.BlockSpec((None, None, rows, HEAD_DIM),
                         functools.partial(lambda s, p, pt, g: (layer, pt[s, n * p + g], 0, 0), g=g))
            for g in range(n)]


def _moba_sample_kernel(pt_ref, q_ref, kvn_ref, *refs, n_pages, n_step, t_new):
    del pt_ref
    page_refs = refs[:n_step]
    o_ref, qf_ref, qb_ref, gx_ref, mx_ref, lx_ref, po_ref = refs[n_step:]
    p = pl.program_id(1)
    kvh = MOBA_KV_HEADS
    group = N_HEADS // kvh
    rows = group * t_new
    stride = 2 * kvh

    @pl.when(p == 0)
    def _():
        for j in range(kvh):
            qj = _group_rows(q_ref[...], j * group, group)
            qf_ref[j] = qj
            qb_ref[j] = qj.astype(BF16)
        gx_ref[...] = jnp.zeros(gx_ref.shape, F32)
        mx_ref[...] = jnp.zeros(mx_ref.shape, F32)
        lx_ref[...] = jnp.zeros(lx_ref.shape, F32)

    lane = lax.broadcasted_iota(jnp.int32, (rows, LANES), 1)
    zero = jnp.zeros((rows, PAGE_SIZE), BF16)
    ks = [[page_refs[g][pl.ds(j, PAGE_SIZE, stride=stride), :] for g in range(n_step)] for j in range(kvh)]
    scores = [_dot_t(qb_ref[j], jnp.concatenate(ks[j], axis=0).astype(BF16)) for j in range(kvh)]
    e_all = []
    for j in range(kvh):
        qf = qf_ref[j]
        gx, mx, lx = gx_ref[j], mx_ref[j], lx_ref[j]
        e_rows = []
        for g in range(n_step):
            gate = jnp.sum(qf * jnp.sum(ks[j][g], axis=0, keepdims=True), axis=1, keepdims=True)
            sg = scores[j][:, g * PAGE_SIZE:(g + 1) * PAGE_SIZE]
            m = jnp.max(sg, axis=1, keepdims=True)
            e = jnp.exp(sg - m)
            e_rows.append(jnp.concatenate([e.astype(BF16) if gg == g else zero for gg in range(n_step)], axis=1))
            here = lane == p * n_step + g
            gx = jnp.where(here, gate, gx)
            mx = jnp.where(here, m, mx)
            lx = jnp.where(here, jnp.sum(e, axis=1, keepdims=True), lx)
        e_all.append(jnp.concatenate(e_rows, axis=0))
        gx_ref[j] = gx
        mx_ref[j] = mx
        lx_ref[j] = lx
    for j in range(kvh):
        v_all = jnp.concatenate([page_refs[g][pl.ds(kvh + j, PAGE_SIZE, stride=stride), :] for g in range(n_step)],
                                axis=0)
        o_all = _dot(e_all[j], v_all.astype(BF16))
        for g in range(n_step):
            po_ref[p * n_step + g, j] = o_all[g * rows:(g + 1) * rows, :]

    @pl.when(p == n_pages // n_step - 1)
    def _():
        n_past = n_pages // 2
        rowi = lax.broadcasted_iota(jnp.int32, (rows, LANES), 0)
        mask_n = lane <= (rowi % t_new)
        first = (lane % 2 == 0) & (lane < n_pages)
        for j in range(kvh):
            g2 = gx_ref[j]
            gate = jnp.where(first, (g2 + pltpu.roll(g2, LANES - 1, 1)) * (1.0 / MOBA_BLOCK), NEG)
            rank = jnp.zeros(gate.shape, jnp.int32)
            for bp in range(n_past):
                col = gate[:, 2 * bp:2 * bp + 1]
                rank = rank + ((col > gate) | ((col == gate) & (2 * bp < lane))).astype(jnp.int32)
            sel_first = jnp.where(first & (rank < MOBA_TOPK), 1.0, 0.0)
            sel = (sel_first + pltpu.roll(sel_first, 1, 1)) > 0.5
            kn = _pad_rows(kvn_ref[pl.ds(j, t_new, stride=stride), :], LANES).astype(BF16)
            vn = _pad_rows(kvn_ref[pl.ds(kvh + j, t_new, stride=stride), :], LANES).astype(BF16)
            sn = jnp.where(mask_n, _dot_t(qb_ref[j], kn), NEG)
            mx = mx_ref[j]
            m_tot = jnp.maximum(jnp.max(sn, axis=1, keepdims=True),
                                jnp.max(jnp.where(sel, mx, NEG), axis=1, keepdims=True))
            pn = jnp.where(mask_n, jnp.exp(sn - m_tot), 0.0)
            w = jnp.where(sel, jnp.exp(mx - m_tot), 0.0)
            l_tot = jnp.sum(pn, axis=1, keepdims=True) + jnp.sum(w * lx_ref[j], axis=1, keepdims=True)
            o = _dot(pn.astype(BF16), vn)
            for pg in range(n_pages):
                o = o + w[:, pg:pg + 1] * po_ref[pg, j]
            o = o / jnp.maximum(l_tot, 1e-30)
            for g in range(group):
                h = j * group + g
                o_ref[:, h * HEAD_DIM:(h + 1) * HEAD_DIM] = o[g * t_new:(g + 1) * t_new, :]


def _moba_sample(q_s, kv_s, cache, layer, page_table, past_len):
    n_seq, t_new, _ = q_s.shape
    n_pages = page_table.shape[1]
    kvh = MOBA_KV_HEADS
    rows = (N_HEADS // kvh) * t_new
    n_step = PAGES_PER_STEP
    assert n_pages <= LANES and n_pages % n_step == 0 and past_len == n_pages * PAGE_SIZE
    assert MOBA_BLOCK == 2 * PAGE_SIZE and t_new <= MOBA_BLOCK
    per_seq = lambda r, w: pl.BlockSpec((None, r, w), lambda s, p, pt: (s, 0, 0))
    grid_spec = pltpu.PrefetchScalarGridSpec(
        num_scalar_prefetch=1,
        grid=(n_seq, n_pages // n_step),
        in_specs=[per_seq(t_new, N_HEADS * HEAD_DIM), per_seq(t_new * 2 * kvh, HEAD_DIM)]
        + _page_specs(n_step, PAGE_SIZE * 2 * kvh, layer),
        out_specs=per_seq(t_new, N_HEADS * HEAD_DIM),
        scratch_shapes=[
            pltpu.VMEM((kvh, rows, HEAD_DIM), F32),
            pltpu.VMEM((kvh, rows, HEAD_DIM), BF16),
            pltpu.VMEM((kvh, rows, LANES), F32),
            pltpu.VMEM((kvh, rows, LANES), F32),
            pltpu.VMEM((kvh, rows, LANES), F32),
            pltpu.VMEM((n_pages, kvh, rows, HEAD_DIM), F32),
        ],
    )
    return pl.pallas_call(
        functools.partial(_moba_sample_kernel, n_pages=n_pages, n_step=n_step, t_new=t_new),
        grid_spec=grid_spec,
        out_shape=jax.ShapeDtypeStruct((n_seq, t_new, N_HEADS * HEAD_DIM), F32),
        compiler_params=_cparams(2),
        name="moba_sample",
    )(page_table, q_s, kv_s, *([cache] * n_step))


def _lru_kernel(u_ref, gact_ref, prev0_ref, h0_ref, cw_ref, cb_ref, wa_ref, ba_ref, wx_ref, bx_ref, lam_ref,
                a_out_ref, hlast_ref, prev_s, h_s, a_s, b_s, y_s, *, tt):
    j = pl.program_id(1)
    width = u_ref.shape[1]

    @pl.when(j == 0)
    def _():
        if tt > SUBLANES:
            prev_s[0:tt - SUBLANES, :] = jnp.zeros((tt - SUBLANES, width), F32)
        prev_s[tt - SUBLANES:tt, :] = prev0_ref[...]
        h_s[...] = h0_ref[...]

    u = u_ref[...]
    prev = prev_s[...]
    rowi = lax.broadcasted_iota(jnp.int32, u.shape, 0)
    xc = cb_ref[...]
    for k in range(CONV_WIDTH):
        sh = CONV_WIDTH - 1 - k
        if sh == 0:
            us = u
        else:
            us = jnp.where(rowi < sh, pltpu.roll(prev, sh, 0), pltpu.roll(u, sh, 0))
        xc = xc + us * cw_ref[k:k + 1, :]
    prev_s[...] = u

    xb = xc.astype(BF16)
    nblk = width // LRU_BLOCK_WIDTH
    ra, rx = [], []
    for n in range(nblk):
        xs = xb[:, n * LRU_BLOCK_WIDTH:(n + 1) * LRU_BLOCK_WIDTH]
        ra.append(_dot(xs, wa_ref[n]))
        rx.append(_dot(xs, wx_ref[n]))
    r = _sigmoid(jnp.concatenate(ra, axis=1) + ba_ref[...])
    ig = _sigmoid(jnp.concatenate(rx, axis=1) + bx_ref[...])
    nl = -lam_ref[...]
    softplus = jnp.maximum(nl, 0.0) + jnp.log(1.0 + jnp.exp(-jnp.abs(nl)))
    log_a = -LRU_C * r * softplus
    a = jnp.exp(log_a)
    a_s[...] = a
    b_s[...] = jnp.sqrt(1.0 - a * a) * ig * xc

    def step(t, h):
        h = a_s[pl.ds(t, 1), :] * h + b_s[pl.ds(t, 1), :]
        y_s[pl.ds(t, 1), :] = h
        return h

    h = lax.fori_loop(0, tt, step, h_s[...], unroll=8)
    h_s[...] = h
    hlast_ref[...] = h
    a_out_ref[...] = (gact_ref[...] * y_s[...]).astype(a_out_ref.dtype)


def _lru(u, gact, row0, n_seq, seq_len, tt, prev0, h0, cw, cb, wa, ba, wx, bx, lam, out_dtype):
    width = u.shape[1]
    tps = seq_len // tt
    base = row0 // tt
    rowmap = lambda s, j: (base + s * tps + j, 0)
    vec = pl.BlockSpec((1, width), lambda s, j: (0, 0))
    return pl.pallas_call(
        functools.partial(_lru_kernel, tt=tt),
        grid=(n_seq, tps),
        in_specs=[
            pl.BlockSpec((tt, width), rowmap),
            pl.BlockSpec((tt, width), rowmap),
            pl.BlockSpec((None, SUBLANES, width), lambda s, j: (s, 0, 0)),
            pl.BlockSpec((None, 1, width), lambda s, j: (s, 0, 0)),
            pl.BlockSpec((CONV_WIDTH, width), lambda s, j: (0, 0)),
            vec, _whole(), vec, _whole(), vec, vec,
        ],
        out_specs=[
            pl.BlockSpec((tt, width), lambda s, j: (s * tps + j, 0)),
            pl.BlockSpec((None, 1, width), lambda s, j: (s, 0, 0)),
        ],
        out_shape=[
            jax.ShapeDtypeStruct((n_seq * seq_len, width), out_dtype),
            jax.ShapeDtypeStruct((n_seq, 1, width), F32),
        ],
        scratch_shapes=[
            pltpu.VMEM((tt, width), F32),
            pltpu.VMEM((1, width), F32),
            pltpu.VMEM((tt, width), F32),
            pltpu.VMEM((tt, width), F32),
            pltpu.VMEM((tt, width), F32),
        ],
        compiler_params=_cparams(2),
        name="rglru",
    )(u, gact, prev0, h0, cw, cb.reshape(1, -1), wa, ba.reshape(1, -1), wx, bx.reshape(1, -1), lam.reshape(1, -1))


CMP_PAGES_PER_STEP = 16


def _cmp_half_pages_kernel(pt_ref, *refs, n_step):
    del pt_ref
    page_refs, (wk_ref, wv_ref, o_ref) = refs[:n_step], refs[n_step:]
    c4 = 2 * NSA_KV_HEADS
    halves = PAGE_SIZE // CMP_STRIDE
    for c in range(c4):
        w_ref = wk_ref if c < NSA_KV_HEADS else wv_ref
        acc = jnp.zeros((n_step * halves, 2 * HEAD_DIM), F32)
        for l in range(CMP_STRIDE):
            x = jnp.concatenate([pr[pl.ds(l * c4 + c, halves, stride=CMP_STRIDE * c4), :] for pr in page_refs], axis=0)
            acc = acc + _dot(x.astype(BF16), w_ref[l])
        o_ref[:, c * 2 * HEAD_DIM:(c + 1) * 2 * HEAD_DIM] = acc


def _cmp_half_pages(cache, layer, page_table, w1k_cat, w1v_cat):
    n_seq, n_pages = page_table.shape
    n_step = CMP_PAGES_PER_STEP
    c4 = 2 * NSA_KV_HEADS
    halves = PAGE_SIZE // CMP_STRIDE
    assert n_pages % n_step == 0 and PAGE_SIZE % CMP_STRIDE == 0 and halves == SUBLANES
    steps = n_pages // n_step
    grid_spec = pltpu.PrefetchScalarGridSpec(
        num_scalar_prefetch=1,
        grid=(n_seq, steps),
        in_specs=_page_specs(n_step, PAGE_SIZE * c4, layer) + [_whole(), _whole()],
        out_specs=pl.BlockSpec((n_step * halves, c4 * 2 * HEAD_DIM), lambda s, p, pt: (s * steps + p, 0)),
    )
    return pl.pallas_call(
        functools.partial(_cmp_half_pages_kernel, n_step=n_step),
        grid_spec=grid_spec,
        out_shape=jax.ShapeDtypeStruct((n_seq * n_pages * halves, c4 * 2 * HEAD_DIM), F32),
        compiler_params=_cparams(2),
        name="nsa_cmp_half_pages",
    )(page_table, *([cache] * n_step), w1k_cat, w1v_cat)


def _cmp_half_kernel(x_ref, wk_ref, wv_ref, o_ref, *, n_half):
    c4 = 2 * NSA_KV_HEADS
    for c in range(c4):
        w_ref = wk_ref if c < NSA_KV_HEADS else wv_ref
        acc = jnp.zeros((n_half, 2 * HEAD_DIM), F32)
        for l in range(CMP_STRIDE):
            x = x_ref[pl.ds(l * c4 + c, n_half, stride=CMP_STRIDE * c4), :]
            acc = acc + _dot(x.astype(BF16), w_ref[l])
        o_ref[:, c * 2 * HEAD_DIM:(c + 1) * 2 * HEAD_DIM] = acc


def _cmp_finish_kernel(ab_ref, pek_ref, pev_ref, w1k_ref, w1v_ref, w2k_ref, w2v_ref, o_ref, *, n_half):
    nc = n_half - (CMP_BLOCK // CMP_STRIDE - 1)
    rowi = lax.broadcasted_iota(jnp.int32, (n_half, HEAD_DIM), 0)
    for c in range(2 * NSA_KV_HEADS):
        is_k = c < NSA_KV_HEADS
        pe = (pek_ref if is_k else pev_ref)[...]
        w1 = (w1k_ref if is_k else w1v_ref)[...]
        w2 = (w2k_ref if is_k else w2v_ref)[...]
        bias = _dot(pe.astype(BF16), w1)[0:1, :]
        a = ab_ref[:, c * 2 * HEAD_DIM:c * 2 * HEAD_DIM + HEAD_DIM]
        b = ab_ref[:, c * 2 * HEAD_DIM + HEAD_DIM:(c + 1) * 2 * HEAD_DIM]
        hid = a + pltpu.roll(b, n_half - 1, 0) + bias
        tok = _dot(_gelu(hid).astype(BF16), w2)
        o_ref[c] = jnp.where(rowi < nc, tok, 0.0)


def _cmp_half(x, row0, n_seq, seq_len, w1k_cat, w1v_cat):
    rows_per_step = 2048
    assert CMP_BLOCK == 2 * CMP_STRIDE and seq_len % rows_per_step == 0 and row0 % rows_per_step == 0
    n_half_step = rows_per_step // CMP_STRIDE
    n_steps = n_seq * seq_len // rows_per_step
    c4 = 2 * NSA_KV_HEADS
    return pl.pallas_call(
        functools.partial(_cmp_half_kernel, n_half=n_half_step),
        grid=(n_steps,),
        in_specs=[pl.BlockSpec((rows_per_step * c4, HEAD_DIM), lambda i: (row0 // rows_per_step + i, 0)),
                  _whole(), _whole()],
        out_specs=pl.BlockSpec((n_half_step, c4 * 2 * HEAD_DIM), lambda i: (i, 0)),
        out_shape=jax.ShapeDtypeStruct((n_steps * n_half_step, c4 * 2 * HEAD_DIM), F32),
        compiler_params=_cparams(1),
        name="nsa_cmp_half",
    )(x, w1k_cat, w1v_cat)


def _cmp_finish(ab, n_seq, seq_len, pe_k, pe_v, w1k, w1v, w2k, w2v):
    n_half = seq_len // CMP_STRIDE
    c4 = 2 * NSA_KV_HEADS
    return pl.pallas_call(
        functools.partial(_cmp_finish_kernel, n_half=n_half),
        grid=(n_seq,),
        in_specs=[pl.BlockSpec((n_half, c4 * 2 * HEAD_DIM), lambda s: (s, 0)),
                  _whole(), _whole(), _whole(), _whole(), _whole(), _whole()],
        out_specs=pl.BlockSpec((None, c4, n_half, HEAD_DIM), lambda s: (s, 0, 0, 0)),
        out_shape=jax.ShapeDtypeStruct((n_seq, c4, n_half, HEAD_DIM), F32),
        compiler_params=_cparams(1),
        name="nsa_cmp_finish",
    )(ab, pe_k, pe_v, w1k, w1v, w2k, w2v)


def _sel_map(n_rows, n_cols, transposed=False):
    r = lax.broadcasted_iota(jnp.int32, (n_rows, n_cols), 0)
    k = lax.broadcasted_iota(jnp.int32, (n_rows, n_cols), 1)
    c, j = (k, r) if transposed else (r, k)
    d = c - j * (SEL_BLOCK // CMP_STRIDE)
    w = jnp.zeros((n_rows, n_cols), F32)
    for m in range(SEL_BLOCK // CMP_STRIDE):
        for n in range(CMP_BLOCK // CMP_STRIDE):
            w = w + jnp.where(d == m - n, 1.0, 0.0)
    return w


def _split3(x):
    hi = x.astype(BF16)
    r1 = x - hi.astype(F32)
    mid = r1.astype(BF16)
    return hi, mid, (r1 - mid.astype(F32)).astype(BF16)


def _importance(p_sum, sel_map_bf16):
    hi, mid, lo = _split3(p_sum)
    return _dot(hi, sel_map_bf16) + _dot(mid, sel_map_bf16) + _dot(lo, sel_map_bf16)


def _importance_t(sel_map_t_bf16, p_sum_t):
    hi, mid, lo = _split3(p_sum_t)
    return _dot(sel_map_t_bf16, hi) + _dot(sel_map_t_bf16, mid) + _dot(sel_map_t_bf16, lo)


def _select_blocks(imp, cur, lane, n_sel):
    forced = (lane == 0) | (lane == cur) | (lane == cur - 1)
    allowed = lane <= cur
    score = jnp.where(allowed, jnp.where(forced, 1e30, imp), NEG)
    return allowed & (_rank_before(score, lane, n_sel) < SEL_TOPN)


def _masked_softmax_rows(s, mask):
    s = jnp.where(mask, s, NEG)
    m = jnp.max(s, axis=1, keepdims=True)
    e = jnp.where(mask, jnp.exp(s - m), 0.0)
    return e / jnp.maximum(jnp.sum(e, axis=1, keepdims=True), 1e-30)


def _nsa_prompt_kernel(q_ref, g_ref, ck_ref, cv_ref, sel_ref, win_ref, o_ref,
                       ckb_ref, cvt_ref, ksb_ref, vst_ref, kwb_ref, vwt_ref, *, seq, tq):
    i = pl.program_id(2)
    grp = GROUP_C
    kvh = NSA_KV_HEADS
    nq = grp * tq
    n_cmp = ck_ref.shape[0]
    nc = n_cmp - (CMP_BLOCK // CMP_STRIDE - 1)
    n_sel = seq // SEL_BLOCK

    @pl.when(i == 0)
    def _():
        k = pl.program_id(1)
        stride = 2 * kvh
        ckb_ref[...] = ck_ref[...].astype(BF16)
        cvt_ref[...] = cv_ref[...].T.astype(BF16)
        ksb_ref[...] = sel_ref[pl.ds(k, seq, stride=stride), :].astype(BF16)
        kwb_ref[...] = win_ref[pl.ds(k, seq, stride=stride), :].astype(BF16)
        for c in range(seq // (2 * tq)):
            vst_ref[c] = sel_ref[pl.ds(2 * c * tq * stride + kvh + k, 2 * tq, stride=stride), :].T.astype(BF16)
        for b in range(seq // tq):
            vwt_ref[b] = win_ref[pl.ds(b * tq * stride + kvh + k, tq, stride=stride), :].T.astype(BF16)

    q4 = jnp.concatenate([q_ref[:, g * HEAD_DIM:(g + 1) * HEAD_DIM] for g in range(grp)], axis=0)
    pos = i * tq + lax.broadcasted_iota(jnp.int32, (1, tq), 1)
    pos4 = jnp.concatenate([pos] * grp, axis=1)

    c_row = lax.broadcasted_iota(jnp.int32, (n_cmp, nq), 0)
    avail = (c_row * CMP_STRIDE + (CMP_BLOCK - 1) <= pos4) & (c_row < nc)
    s_c = jnp.where(avail, _dot_t(ckb_ref[...], q4), NEG)
    m_c = jnp.max(s_c, axis=0, keepdims=True)
    e_c = jnp.where(avail, jnp.exp(s_c - m_c), 0.0)
    p_c = e_c / jnp.maximum(jnp.sum(e_c, axis=0, keepdims=True), 1e-30)
    o_c = _dot(cvt_ref[...], p_c.astype(BF16))

    p_sum = p_c[:, 0:tq]
    for g in range(1, grp):
        p_sum = p_sum + p_c[:, g * tq:(g + 1) * tq]
    imp = _importance_t(_sel_map(LANES, n_cmp, transposed=True).astype(BF16), p_sum)[0:n_sel, :]
    j_row = lax.broadcasted_iota(jnp.int32, (n_sel, tq), 0)
    cur = pos // SEL_BLOCK
    forced = (j_row == 0) | (j_row == cur) | (j_row == cur - 1)
    allowed = j_row <= cur
    score = jnp.where(allowed, jnp.where(forced, 1e30, imp), NEG)
    selected = allowed & (_rank_before_t(score, n_sel) < SEL_TOPN)
    sel_bf = jnp.concatenate([jnp.where(selected, 1.0, 0.0), jnp.zeros((LANES - n_sel, tq), F32)],
                             axis=0).astype(BF16)

    tk = 2 * tq
    key = lax.broadcasted_iota(jnp.int32, (tk, tq), 0)
    ek = lax.broadcasted_iota(jnp.int32, (tk, LANES), 0)
    ej = lax.broadcasted_iota(jnp.int32, (tk, LANES), 1)

    def sel_trip(c, carry):
        m, l, acc = carry
        expand = jnp.where(ej == c * (tk // SEL_BLOCK) + ek // SEL_BLOCK, 1.0, 0.0).astype(BF16)
        chosen = _dot(expand, sel_bf) > 0.5
        bias = jnp.where(chosen & (c * tk + key <= pos), 0.0, NEG)
        s = _dot_t(ksb_ref[pl.ds(pl.multiple_of(c * tk, tk), tk), :], q4) + jnp.concatenate([bias] * grp, axis=1)
        m_new = jnp.maximum(m, jnp.max(s, axis=0, keepdims=True))
        p = jnp.exp(s - m_new)
        alpha = jnp.exp(m - m_new)
        l = alpha * l + jnp.sum(p, axis=0, keepdims=True)
        return m_new, l, alpha * acc + _dot(vst_ref[c], p.astype(BF16))

    init = (jnp.full((1, nq), NEG, F32), jnp.zeros((1, nq), F32), jnp.zeros((HEAD_DIM, nq), F32))
    m_s, l_s, acc_s = lax.fori_loop(0, i // 2 + 1, sel_trip, init)
    o_s = acc_s / jnp.maximum(l_s, 1e-30)

    n_wb = WINDOW // tq + 1
    start = jnp.minimum(jnp.maximum(i - (n_wb - 1), 0), seq // tq - n_wb)
    wkey = start * tq + lax.broadcasted_iota(jnp.int32, (n_wb * tq, tq), 0)
    wbias = jnp.where((wkey <= pos) & (wkey > pos - WINDOW), 0.0, NEG)
    s_w = _dot_t(kwb_ref[pl.ds(pl.multiple_of(start * tq, tq), n_wb * tq), :], q4)
    s_w = s_w + jnp.concatenate([wbias] * grp, axis=1)
    p_w = jnp.exp(s_w - jnp.max(s_w, axis=0, keepdims=True))
    vwt = jnp.concatenate([vwt_ref[start + b] for b in range(n_wb)], axis=1)
    o_w = _dot(vwt, p_w.astype(BF16)) / jnp.maximum(jnp.sum(p_w, axis=0, keepdims=True), 1e-30)

    gates = g_ref[...].T
    for g in range(grp):
        cs = slice(g * tq, (g + 1) * tq)
        o = (gates[3 * g:3 * g + 1, :] * o_c[:, cs] + gates[3 * g + 1:3 * g + 2, :] * o_s[:, cs]
             + gates[3 * g + 2:3 * g + 3, :] * o_w[:, cs])
        o_ref[:, g * HEAD_DIM:(g + 1) * HEAD_DIM] = o.T.astype(o_ref.dtype)


def _nsa_prompt(q, gates, ckv, sel, win, batch, seq):
    tq = 256
    nq = seq // tq
    kvh = NSA_KV_HEADS
    gw = GROUP_C * HEAD_DIM
    n_cmp = ckv.shape[2]
    assert seq // SEL_BLOCK <= LANES and n_cmp <= LANES and WINDOW % tq == 0
    assert nq % 2 == 0 and nq >= WINDOW // tq + 1
    full = pl.BlockSpec((seq * 2 * kvh, HEAD_DIM), lambda b, k, i: (b, 0))
    return pl.pallas_call(
        functools.partial(_nsa_prompt_kernel, seq=seq, tq=tq),
        grid=(batch, kvh, nq),
        in_specs=[
            pl.BlockSpec((tq, gw), lambda b, k, i: (b * nq + i, k)),
            pl.BlockSpec((tq, LANES), lambda b, k, i: (b * nq + i, k)),
            pl.BlockSpec((None, None, n_cmp, HEAD_DIM), lambda b, k, i: (b, k, 0, 0)),
            pl.BlockSpec((None, None, n_cmp, HEAD_DIM), lambda b, k, i: (b, kvh + k, 0, 0)),
            full, full,
        ],
        out_specs=pl.BlockSpec((tq, gw), lambda b, k, i: (b * nq + i, k)),
        out_shape=jax.ShapeDtypeStruct((batch * seq, N_HEADS * HEAD_DIM), BF16),
        scratch_shapes=[
            pltpu.VMEM((n_cmp, HEAD_DIM), BF16),
            pltpu.VMEM((HEAD_DIM, n_cmp), BF16),
            pltpu.VMEM((seq, HEAD_DIM), BF16),
            pltpu.VMEM((nq // 2, HEAD_DIM, 2 * tq), BF16),
            pltpu.VMEM((seq, HEAD_DIM), BF16),
            pltpu.VMEM((nq, HEAD_DIM, tq), BF16),
        ],
        compiler_params=_cparams(3),
        name="nsa_prompt",
    )(q, gates, ckv, ckv, sel, win)


def _nsa_sample_pre_kernel(q_ref, ckv_ref, wst_ref, wnew_ref, oc_ref, ow_ref, selm_ref, *, t_new, past_len):
    grp = GROUP_C
    kvh = NSA_KV_HEADS
    n_cmp = ckv_ref.shape[1]
    nc = n_cmp - (CMP_BLOCK // CMP_STRIDE - 1)
    wb = wst_ref.shape[0] // (2 * kvh)
    n_sel = (past_len + t_new + SEL_BLOCK - 1) // SEL_BLOCK
    sel_lanes = selm_ref.shape[1]
    rows = grp * t_new
    rowi = lax.broadcasted_iota(jnp.int32, (rows, 1), 0)
    pos = past_len + rowi % t_new
    pos_t = past_len + lax.broadcasted_iota(jnp.int32, (t_new, 1), 0)
    lane_c = lax.broadcasted_iota(jnp.int32, (rows, n_cmp), 1)
    avail = (lane_c * CMP_STRIDE + (CMP_BLOCK - 1) <= pos) & (lane_c < nc)
    lane_s = lax.broadcasted_iota(jnp.int32, (t_new, sel_lanes), 1)
    smap = _sel_map(n_cmp, sel_lanes).astype(BF16)
    lane_w = lax.broadcasted_iota(jnp.int32, (rows, wb), 1)
    wpos = past_len - wb + lane_w
    mask_w = (wpos <= pos) & (wpos > pos - WINDOW)
    lane_n = lax.broadcasted_iota(jnp.int32, (rows, LANES), 1)
    npos = past_len + lane_n
    mask_n = (npos <= pos) & (npos > pos - WINDOW) & (lane_n < t_new)
    for k in range(kvh):
        q4 = jnp.concatenate(
            [q_ref[:, (k * grp + g) * HEAD_DIM:(k * grp + g + 1) * HEAD_DIM] for g in range(grp)], axis=0).astype(BF16)
        p_c = _masked_softmax_rows(_dot_t(q4, ckv_ref[k].astype(BF16)), avail)
        oc_ref[k * rows:(k + 1) * rows, :] = _dot(p_c.astype(BF16), ckv_ref[kvh + k].astype(BF16))
        p_sum = p_c[0:t_new]
        for g in range(1, grp):
            p_sum = p_sum + p_c[g * t_new:(g + 1) * t_new]
        imp = _importance(p_sum, smap)
        selected = _select_blocks(imp, pos_t // SEL_BLOCK, lane_s, n_sel)
        selm_ref[k * t_new:(k + 1) * t_new, :] = jnp.where(selected, 1.0, 0.0)
        kst = wst_ref[pl.ds(k, wb, stride=2 * kvh), :].astype(BF16)
        vst = wst_ref[pl.ds(kvh + k, wb, stride=2 * kvh), :].astype(BF16)
        kn = _pad_rows(wnew_ref[pl.ds(k, t_new, stride=2 * kvh), :], LANES).astype(BF16)
        vn = _pad_rows(wnew_ref[pl.ds(kvh + k, t_new, stride=2 * kvh), :], LANES).astype(BF16)
        s1 = jnp.where(mask_w, _dot_t(q4, kst), NEG)
        s2 = jnp.where(mask_n, _dot_t(q4, kn), NEG)
        m = jnp.maximum(jnp.max(s1, axis=1, keepdims=True), jnp.max(s2, axis=1, keepdims=True))
        e1 = jnp.where(mask_w, jnp.exp(s1 - m), 0.0)
        e2 = jnp.where(mask_n, jnp.exp(s2 - m), 0.0)
        den = jnp.maximum(jnp.sum(e1, axis=1, keepdims=True) + jnp.sum(e2, axis=1, keepdims=True), 1e-30)
        ow_ref[k * rows:(k + 1) * rows, :] = (_dot(e1.astype(BF16), vst) + _dot(e2.astype(BF16), vn)) / den


def _nsa_sample_pre(q_s, ckv, win_state, layer, win_new, past_len):
    n_seq, t_new, _ = q_s.shape
    n_cmp = ckv.shape[2]
    wrows = win_state.shape[2]
    c4 = 2 * NSA_KV_HEADS * HEAD_DIM
    rows = N_HEADS * t_new
    n_sel = (past_len + t_new + SEL_BLOCK - 1) // SEL_BLOCK
    sel_lanes = -(-n_sel // LANES) * LANES
    return pl.pallas_call(
        functools.partial(_nsa_sample_pre_kernel, t_new=t_new, past_len=past_len),
        grid=(n_seq,),
        in_specs=[
            pl.BlockSpec((None, t_new, N_HEADS * HEAD_DIM), lambda s: (s, 0, 0)),
            pl.BlockSpec((None, 2 * NSA_KV_HEADS, n_cmp, HEAD_DIM), lambda s: (s, 0, 0, 0)),
            pl.BlockSpec((None, None, wrows, HEAD_DIM), lambda s: (layer, s, 0, 0)),
            pl.BlockSpec((None, t_new * c4 // HEAD_DIM, HEAD_DIM), lambda s: (s, 0, 0)),
        ],
        out_specs=[
            pl.BlockSpec((None, rows, HEAD_DIM), lambda s: (s, 0, 0)),
            pl.BlockSpec((None, rows, HEAD_DIM), lambda s: (s, 0, 0)),
            pl.BlockSpec((None, NSA_KV_HEADS * t_new, sel_lanes), lambda s: (s, 0, 0)),
        ],
        out_shape=[
            jax.ShapeDtypeStruct((n_seq, rows, HEAD_DIM), F32),
            jax.ShapeDtypeStruct((n_seq, rows, HEAD_DIM), F32),
            jax.ShapeDtypeStruct((n_seq, NSA_KV_HEADS * t_new, sel_lanes), F32),
        ],
        compiler_params=_cparams(1),
        name="nsa_sample_pre",
    )(q_s, ckv, win_state, win_new)


def _nsa_sample_sel_kernel(pt_ref, q_ref, g_ref, snew_ref, selm_ref, oc_ref, ow_ref, *refs,
                           n_pages, n_step, t_new, past_len):
    del pt_ref
    page_refs = refs[:n_step]
    o_ref, qb_ref, selrows_ref, mx_ref, lx_ref, po_ref = refs[n_step:]
    p = pl.program_id(1)
    grp = GROUP_C
    kvh = NSA_KV_HEADS
    rows = grp * t_new
    sel_lanes = selm_ref.shape[1]
    stride = 2 * kvh
    assert PAGE_SIZE == 2 * SEL_BLOCK

    @pl.when(p == 0)
    def _():
        for k in range(kvh):
            qb_ref[k] = _group_rows(q_ref[...], k * grp, grp).astype(BF16)
            selrows_ref[k] = jnp.concatenate([selm_ref[k * t_new:(k + 1) * t_new, :]] * grp, axis=0)
        mx_ref[...] = jnp.zeros(mx_ref.shape, F32)
        lx_ref[...] = jnp.zeros(lx_ref.shape, F32)

    lane = lax.broadcasted_iota(jnp.int32, (rows, LANES), 1)
    lane_s = lax.broadcasted_iota(jnp.int32, (rows, sel_lanes), 1)

    def sel_col(selrows, block):
        return jnp.sum(jnp.where(lane_s == block, selrows, 0.0), axis=1, keepdims=True)

    n_keys = n_step * PAGE_SIZE
    ej = lax.broadcasted_iota(jnp.int32, (sel_lanes, n_keys), 0)
    ec = lax.broadcasted_iota(jnp.int32, (sel_lanes, n_keys), 1)
    expand = jnp.where(ej == p * (n_keys // SEL_BLOCK) + ec // SEL_BLOCK, 1.0, 0.0).astype(BF16)
    here = lane == p
    for k in range(kvh):
        k_all = jnp.concatenate([pr[pl.ds(k, PAGE_SIZE, stride=stride), :] for pr in page_refs], axis=0)
        v_all = jnp.concatenate([pr[pl.ds(kvh + k, PAGE_SIZE, stride=stride), :] for pr in page_refs], axis=0)
        mask = _dot(selrows_ref[k].astype(BF16), expand) > 0.5
        s = jnp.where(mask, _dot_t(qb_ref[k], k_all.astype(BF16)), NEG)
        m = jnp.max(s, axis=1, keepdims=True)
        e = jnp.where(mask, jnp.exp(s - m), 0.0)
        po_ref[p, k] = _dot(e.astype(BF16), v_all.astype(BF16))
        mx_ref[k] = jnp.where(here, m, mx_ref[k])
        lx_ref[k] = jnp.where(here, jnp.sum(e, axis=1, keepdims=True), lx_ref[k])

    @pl.when(p == n_pages // n_step - 1)
    def _():
        rowi = lax.broadcasted_iota(jnp.int32, (rows, LANES), 0)
        causal = lane <= rowi % t_new
        is_page = lane < n_pages // n_step
        gates = g_ref[...]
        for k in range(kvh):
            selrows = selrows_ref[k]
            mask_n = causal & (sel_col(selrows, past_len // SEL_BLOCK) > 0.5)
            kn = _pad_rows(snew_ref[pl.ds(k, t_new, stride=stride), :], LANES).astype(BF16)
            vn = _pad_rows(snew_ref[pl.ds(kvh + k, t_new, stride=stride), :], LANES).astype(BF16)
            sn = jnp.where(mask_n, _dot_t(qb_ref[k], kn), NEG)
            mx = mx_ref[k]
            m_tot = jnp.maximum(jnp.max(sn, axis=1, keepdims=True),
                                jnp.max(jnp.where(is_page, mx, NEG), axis=1, keepdims=True))
            pn = jnp.where(mask_n, jnp.exp(sn - m_tot), 0.0)
            w = jnp.where(is_page, jnp.exp(mx - m_tot), 0.0)
            l_tot = jnp.sum(pn, axis=1, keepdims=True) + jnp.sum(w * lx_ref[k], axis=1, keepdims=True)
            o_s = _dot(pn.astype(BF16), vn)
            for st in range(n_pages // n_step):
                o_s = o_s + w[:, st:st + 1] * po_ref[st, k]
            o_s = o_s / jnp.maximum(l_tot, 1e-30)
            cols = [jnp.concatenate([gates[:, k * LANES + 3 * g + br:k * LANES + 3 * g + br + 1] for g in range(grp)],
                                    axis=0) for br in range(3)]
            rs = slice(k * rows, (k + 1) * rows)
            o = cols[0] * oc_ref[rs, :] + cols[1] * o_s + cols[2] * ow_ref[rs, :]
            for g in range(grp):
                h = k * grp + g
                o_ref[:, h * HEAD_DIM:(h + 1) * HEAD_DIM] = o[g * t_new:(g + 1) * t_new, :]


def _nsa_sample_sel(q_s, g_s, sel_new, selm, o_c, o_w, cache, layer, page_table, past_len):
    n_seq, t_new, _ = q_s.shape
    n_pages = page_table.shape[1]
    kvh = NSA_KV_HEADS
    c4 = 2 * kvh * HEAD_DIM
    rows = GROUP_C * t_new
    sel_lanes = selm.shape[2]
    n_step = PAGES_PER_STEP
    assert past_len % SEL_BLOCK == 0 and t_new <= SEL_BLOCK and past_len == n_pages * PAGE_SIZE
    assert n_pages <= LANES and n_pages % n_step == 0
    per_seq = lambda shape: pl.BlockSpec((None,) + shape, lambda s, p, pt: (s, 0, 0))
    grid_spec = pltpu.PrefetchScalarGridSpec(
        num_scalar_prefetch=1,
        grid=(n_seq, n_pages // n_step),
        in_specs=[
            per_seq((t_new, N_HEADS * HEAD_DIM)),
            per_seq((t_new, kvh * LANES)),
            per_seq((t_new * c4 // HEAD_DIM, HEAD_DIM)),
            per_seq((kvh * t_new, sel_lanes)),
            per_seq((kvh * rows, HEAD_DIM)),
            per_seq((kvh * rows, HEAD_DIM)),
        ] + _page_specs(n_step, PAGE_SIZE * 2 * kvh, layer),
        out_specs=per_seq((t_new, N_HEADS * HEAD_DIM)),
        scratch_shapes=[
            pltpu.VMEM((kvh, rows, HEAD_DIM), BF16),
            pltpu.VMEM((kvh, rows, sel_lanes), F32),
            pltpu.VMEM((kvh, rows, LANES), F32),
            pltpu.VMEM((kvh, rows, LANES), F32),
            pltpu.VMEM((n_pages // n_step, kvh, rows, HEAD_DIM), F32),
        ],
    )
    return pl.pallas_call(
        functools.partial(_nsa_sample_sel_kernel, n_pages=n_pages, n_step=n_step, t_new=t_new, past_len=past_len),
        grid_spec=grid_spec,
        out_shape=jax.ShapeDtypeStruct((n_seq, t_new, N_HEADS * HEAD_DIM), F32),
        compiler_params=_cparams(2),
        name="nsa_sample_sel",
    )(page_table, q_s, g_s, sel_new, selm, o_c, o_w, *([cache] * n_step))


def _rope_tables(seq, t_new, past_len, sample_rows):
    half = HEAD_DIM // 2
    freq = ROPE_THETA ** (-jnp.arange(half, dtype=F32) / half)

    def tab(pos):
        ang = pos.astype(F32)[:, None] * freq[None, :]
        c, s = jnp.cos(ang), jnp.sin(ang)
        return jnp.concatenate([c, c], axis=1), jnp.concatenate([-s, s], axis=1)

    cp, sp = tab(jnp.arange(seq, dtype=jnp.int32))
    cs, ss = tab(past_len + jnp.arange(t_new, dtype=jnp.int32))
    reps = sample_rows // t_new
    return (jnp.concatenate([cp, jnp.tile(cs, (reps, 1))], axis=0),
            jnp.concatenate([sp, jnp.tile(ss, (reps, 1))], axis=0))


def kernel(x_prompt, x_sample, p_prompt, p_sample, cache_moba_kv, cache_nsa_cmp_kv, cache_nsa_sel_kv, state_nsa_win_kv, state_lru_conv, state_lru_h, page_table, norm_mix, norm_ffn, norm_ple, norm_out, moba_w_qkv, moba_w_o, lru_w_in, lru_conv_w, lru_conv_b, lru_w_a, lru_b_a, lru_w_x, lru_b_x, lru_lambda, lru_w_o, nsa_w_in, nsa_w_o, nsa_cmp_pos_k, nsa_cmp_pos_v, nsa_cmp_k_w1, nsa_cmp_k_w2, nsa_cmp_v_w1, nsa_cmp_v_w2, ffn_w_in, ffn_w_out, ple_w_gate, ple_w_proj):
    batch, seq, d = x_prompt.shape
    n_seq, t_new, _ = x_sample.shape
    depth = norm_mix.shape[0]
    n_pages = page_table.shape[1]
    past_len = n_pages * PAGE_SIZE
    mp = batch * seq
    ms = n_seq * t_new
    assert d == D_MODEL and ms == ROW_TILE and seq % ROW_TILE == 0
    hq = N_HEADS * HEAD_DIM

    h_parts = (x_prompt.reshape(mp, d), x_sample.reshape(ms, d))
    p_parts = (p_prompt.reshape(depth, mp, -1), p_sample.reshape(depth, ms, -1))
    cos_t, sin_t = _rope_tables(seq, t_new, past_len, ms)
    proj = functools.partial(_norm_proj, cos_t=cos_t, sin_t=sin_t, seq=seq)

    native = lambda c: c.reshape(c.shape[:2] + (-1, HEAD_DIM))
    moba_cache, cmp_cache, sel_cache, win_state = map(
        native, (cache_moba_kv, cache_nsa_cmp_kv, cache_nsa_sel_kv, state_nsa_win_kv))

    outs = {k: [] for k in ("moba_p", "moba_s", "cmp_p", "cmp_s", "sel_p", "sel_s", "win_p", "win_s",
                            "conv_p", "conv_s", "hh_p", "hh_s")}
    y = None
    moba_kv_p = None
    for i in range(depth):
        kind, j = i % N_MIXERS, i // N_MIXERS
        if kind == 0:
            hk = MOBA_KV_HEADS * HEAD_DIM
            segs = [(0, c, c, 512, "rope_query") for c in range(0, hq, 512)]
            segs += [(1, 0, hq, hk, "rope"), (1, hk, hq + hk, hk, "none")]
            n_moba = moba_w_qkv.shape[0]
            (q_p, q_s), (moba_kv_p, kv_s) = proj(h_parts, norm_mix[i], moba_w_qkv[j].astype(BF16), segs=segs,
                                                 outs=[(hq, BF16, False), (2 * hk, F32, True)],
                                                 stacked=(1, j, n_moba, moba_kv_p))
            a_p = _moba_prompt(q_p, moba_kv_p, j, batch, seq)
            a_s = _moba_sample(q_s.astype(F32).reshape(n_seq, t_new, hq), kv_s.reshape(n_seq, -1, HEAD_DIM),
                               moba_cache, j, page_table, past_len)
            w_o = moba_w_o[j]
            outs["moba_s"].append(kv_s.reshape(n_seq, t_new, 2, MOBA_KV_HEADS, HEAD_DIM))
        elif kind == 1:
            width = lru_w_in.shape[2] // 2
            segs = [(0, c, c, 512, "gelu") for c in range(0, width, 512)]
            segs += [(1, c, width + c, 512, "none") for c in range(0, width, 512)]
            (gact_p, gact_s), (u_p, u_s) = proj(h_parts, norm_mix[i], lru_w_in[j].astype(BF16), segs=segs,
                                                outs=[(width, F32, False), (width, F32, False)])
            lru_w = (lru_conv_w[j], lru_conv_b[j], lru_w_a[j].astype(BF16), lru_b_a[j], lru_w_x[j].astype(BF16),
                     lru_b_x[j], lru_lambda[j])
            zeros_c = jnp.zeros((batch, SUBLANES, width), F32)
            zeros_h = jnp.zeros((batch, 1, width), F32)
            a_p, hh_p = _lru(u_p, gact_p, 0, batch, seq, ROW_TILE, zeros_c, zeros_h, *lru_w, out_dtype=BF16)
            prev_s = jnp.concatenate([jnp.zeros((n_seq, SUBLANES - (CONV_WIDTH - 1), width), F32),
                                      state_lru_conv[j]], axis=1)
            a_s, hh_s = _lru(u_s, gact_s, 0, n_seq, t_new, t_new, prev_s, state_lru_h[j][:, None, :], *lru_w,
                             out_dtype=F32)
            a_s = a_s.reshape(n_seq, t_new, width)
            w_o = lru_w_o[j]
            up_s = jnp.concatenate([state_lru_conv[j], u_s.reshape(n_seq, t_new, width)], axis=1)
            outs["conv_p"].append(
                jnp.stack([u_p[(b + 1) * seq - (CONV_WIDTH - 1):(b + 1) * seq] for b in range(batch)]))
            outs["conv_s"].append(up_s[:, t_new:])
            outs["hh_p"].append(hh_p.reshape(batch, width))
            outs["hh_s"].append(hh_s.reshape(n_seq, width))
        else:
            hk = NSA_KV_HEADS * HEAD_DIM
            w_in = nsa_w_in[j]
            wg = w_in[:, hq + 6 * hk:].reshape(d, NSA_KV_HEADS, GROUP_C * 3)
            wg = jnp.pad(wg, ((0, 0), (0, 0), (0, LANES - GROUP_C * 3))).reshape(d, NSA_KV_HEADS * LANES)
            w_all = jnp.concatenate([w_in[:, :hq + 6 * hk], wg], axis=1).astype(BF16)
            segs = [(0, c, c, 512, "rope_query") for c in range(0, hq, 512)]
            for br in range(3):
                segs += [(1 + br, 0, hq + 2 * br * hk, hk, "rope"), (1 + br, hk, hq + (2 * br + 1) * hk, hk, "none")]
            segs += [(4, 0, hq + 6 * hk, NSA_KV_HEADS * LANES, "sigmoid")]
            (q_p, q_s), (cmp_p, cmp_s), (sel_p, sel_s), (win_p, win_s), (gates_p, gates_s) = proj(
                h_parts, norm_mix[i], w_all, segs=segs,
                outs=[(hq, BF16, False), (2 * hk, F32, True), (2 * hk, F32, True), (2 * hk, F32, True),
                      (NSA_KV_HEADS * LANES, F32, False)])
            w1k, w1v = nsa_cmp_k_w1[j].astype(BF16), nsa_cmp_v_w1[j].astype(BF16)

            def cat(w1):
                w3 = w1.reshape(CMP_BLOCK, HEAD_DIM, w1.shape[1])
                return jnp.concatenate([w3[:CMP_STRIDE], w3[CMP_STRIDE:]], axis=2)

            pe_k = jnp.broadcast_to(nsa_cmp_pos_k[j].reshape(1, -1), (SUBLANES, CMP_BLOCK * HEAD_DIM))
            pe_v = jnp.broadcast_to(nsa_cmp_pos_v[j].reshape(1, -1), (SUBLANES, CMP_BLOCK * HEAD_DIM))
            w1k_cat, w1v_cat = cat(w1k), cat(w1v)
            fin_w = (pe_k, pe_v, w1k, w1v, nsa_cmp_k_w2[j].astype(BF16), nsa_cmp_v_w2[j].astype(BF16))
            ckv_p = _cmp_finish(_cmp_half(cmp_p, 0, batch, seq, w1k_cat, w1v_cat), batch, seq, *fin_w)
            a_p = _nsa_prompt(q_p, gates_p, ckv_p, sel_p, win_p, batch, seq)
            ckv_s = _cmp_finish(_cmp_half_pages(cmp_cache, j, page_table, w1k_cat, w1v_cat), n_seq, past_len, *fin_w)
            q_s = q_s.astype(F32).reshape(n_seq, t_new, hq)
            o_c, o_w, selm = _nsa_sample_pre(q_s, ckv_s, win_state, j, win_s.reshape(n_seq, -1, HEAD_DIM), past_len)
            a_s = _nsa_sample_sel(q_s, gates_s.reshape(n_seq, t_new, -1), sel_s.reshape(n_seq, -1, HEAD_DIM),
                                  selm, o_c, o_w, sel_cache, j, page_table, past_len)
            w_o = nsa_w_o[j]
            shp_p = (batch, seq, 2, NSA_KV_HEADS, HEAD_DIM)
            shp_s = (n_seq, t_new, 2, NSA_KV_HEADS, HEAD_DIM)
            outs["cmp_p"].append(cmp_p.reshape(shp_p))
            outs["cmp_s"].append(cmp_s.reshape(shp_s))
            outs["sel_p"].append(sel_p.reshape(shp_p))
            outs["sel_s"].append(sel_s.reshape(shp_s))
            wb = min(WINDOW, seq)
            outs["win_p"].append(win_p.reshape(shp_p)[:, seq - wb:])
            outs["win_s"].append(jnp.concatenate([state_nsa_win_kv[j], win_s.reshape(shp_s)], axis=1)[:, t_new:])
        tail_w = (w_o.astype(BF16), norm_ffn[i], ffn_w_in[i].astype(BF16), ffn_w_out[i].astype(BF16), norm_ple[i],
                  ple_w_gate[i].astype(BF16), ple_w_proj[i].astype(BF16), norm_out)
        final = i == depth - 1
        res_p = _tail_rows(a_p, h_parts[0], p_parts[0], i, PROMPT_TILE, *tail_w, final=final)
        res_s = _tail_rows(a_s.reshape(ms, -1), h_parts[1], p_parts[1], i, ROW_TILE, *tail_w, final=final)
        h_parts = (res_p[0], res_s[0])
        if final:
            y = (res_p[1], res_s[1])
    return (y[0].reshape(batch, seq, d), y[1].reshape(n_seq, t_new, d),
            moba_kv_p.reshape(moba_w_qkv.shape[0], batch, seq, 2, MOBA_KV_HEADS, HEAD_DIM), jnp.stack(outs["moba_s"]),
            jnp.stack(outs["cmp_p"]), jnp.stack(outs["cmp_s"]),
            jnp.stack(outs["sel_p"]), jnp.stack(outs["sel_s"]),
            jnp.stack(outs["win_p"]), jnp.stack(outs["win_s"]),
            jnp.stack(outs["conv_p"]), jnp.stack(outs["conv_s"]),
            jnp.stack(outs["hh_p"]), jnp.stack(outs["hh_s"]))
```
